```python
import jax, jax.numpy as jnp
from jax import lax
import numpy as np

D_MODEL = 1024
BATCH = 4
SEQ = 8192
DEPTH = 4

N_EVEN = (DEPTH + 1) // 2
N_ODD = DEPTH // 2
EPS = 1e-6

GDN_HEADS = D_MODEL // 256
GDN_DK = 128
GDN_DV = 128
GDN_CONV = 4
GDN_CHUNK = 64
SB_HEADS = D_MODEL // 128
SB_DH = 64
SB_BLOCK = 128
RET_HEADS = D_MODEL // 256
RET_DK = 256
RET_DV = 512
RET_CHUNK = 128
ROPE_BASE = 10000.0
FFN_HIDDEN = ((8 * D_MODEL // 3 + 255) // 256) * 256

GDN_QK = GDN_HEADS * GDN_DK
GDN_VW = GDN_HEADS * GDN_DV
SB_W = SB_HEADS * SB_DH
HYB_SPLITS = (GDN_QK, GDN_QK, GDN_VW, GDN_VW, GDN_HEADS, GDN_HEADS, SB_W, SB_W, SB_W)
HYB_IN = sum(HYB_SPLITS)
CONV_CH = 2 * GDN_QK + GDN_VW
MIX_W = GDN_VW + SB_W
RET_QK = RET_HEADS * RET_DK
RET_VW = RET_HEADS * RET_DV
RET_SPLITS = (RET_QK, RET_QK, RET_VW, RET_VW)
RET_IN = sum(RET_SPLITS)

kernel_name = "hybrid_gdn_stickbreak_retention_adaln"


def _split(x, sizes):
    out, start = [], 0
    for s in sizes:
        out.append(x[..., start:start + s])
        start += s
    return out


def _heads(x, n_heads):
    b, t, _ = x.shape
    return x.reshape(b, t, n_heads, -1).transpose(0, 2, 1, 3)


def _merge(x):
    b, h, t, d = x.shape
    return x.transpose(0, 2, 1, 3).reshape(b, t, h * d)


def rms_norm(x, gain=None):
    xf = x.astype(jnp.float32)
    y = xf * lax.rsqrt(jnp.mean(xf * xf, axis=-1, keepdims=True) + EPS)
    if gain is not None:
        y = y * gain.astype(jnp.float32)
    return y.astype(x.dtype)


def l2_norm(x):
    return x * lax.rsqrt(jnp.sum(x * x, axis=-1, keepdims=True) + EPS)


def causal_short_conv(x, w):
    k_w = w.shape[0]
    t = x.shape[1]
    xp = jnp.pad(x, ((0, 0), (k_w - 1, 0), (0, 0)))
    y = sum(xp[:, j:j + t] * w[j] for j in range(k_w))
    return jax.nn.silu(y)


def rotary(x, pos):
    d = x.shape[-1]
    inv = 1.0 / (ROPE_BASE ** (jnp.arange(0, d, 2, dtype=jnp.float32) / d))
    ang = pos[:, None] * inv[None, :]
    cos, sin = jnp.cos(ang), jnp.sin(ang)
    x1, x2 = x[..., : d // 2], x[..., d // 2:]
    return jnp.concatenate([x1 * cos - x2 * sin, x1 * sin + x2 * cos], axis=-1)


def gated_delta_rule(q, k, v, g, beta):
    b_, h, t, dk = q.shape
    dv = v.shape[-1]
    c = GDN_CHUNK
    n = t // c
    q = (q * dk ** -0.5).reshape(b_, h, n, c, dk)
    k = k.reshape(b_, h, n, c, dk)
    v = v.reshape(b_, h, n, c, dv)
    beta = beta.reshape(b_, h, n, c)
    g = lax.cumsum(g.reshape(b_, h, n, c), axis=3)
    idx = jnp.arange(c)
    strict = idx[:, None] > idx[None, :]
    causal = idx[:, None] >= idx[None, :]
    gdiff = g[..., :, None] - g[..., None, :]
    dec_strict = jnp.exp(jnp.where(strict, gdiff, -jnp.inf))
    dec_causal = jnp.exp(jnp.where(causal, gdiff, -jnp.inf))
    kb = k * beta[..., None]
    lower = jnp.einsum('bhnid,bhnjd->bhnij', kb, k) * dec_strict
    tri = jnp.eye(c, dtype=lower.dtype) + lower
    rhs = jnp.concatenate([v * beta[..., None], kb * jnp.exp(g)[..., None]], axis=-1)
    sol = lax.linalg.triangular_solve(tri, rhs, left_side=True, lower=True, unit_diagonal=True)
    u, w = sol[..., :dv], sol[..., dv:]
    qk_intra = jnp.einsum('bhnid,bhnjd->bhnij', q, k) * dec_causal
    g_last = g[..., -1]
    q_dec = q * jnp.exp(g)[..., None]
    k_dec = k * jnp.exp(g_last[..., None] - g)[..., None]
    xs = tuple(jnp.moveaxis(a, 2, 0) for a in (u, w, qk_intra, q_dec, k_dec, g_last))

    def step(state, inp):
        u_n, w_n, a_n, q_n, k_n, gl_n = inp
        v_new = u_n - jnp.einsum('bhcd,bhde->bhce', w_n, state)
        o_n = jnp.einsum('bhcd,bhde->bhce', q_n, state) + jnp.einsum('bhij,bhje->bhie', a_n, v_new)
        state = state * jnp.exp(gl_n)[..., None, None] + jnp.einsum('bhcd,bhce->bhde', k_n, v_new)
        return state, o_n

    s0 = jnp.zeros((b_, h, dk, dv), q.dtype)
    _, o = lax.scan(step, s0, xs)
    return jnp.moveaxis(o, 0, 2).reshape(b_, h, t, dv)


def stick_breaking_attention(q, k, v):
    b_, h, t, d = q.shape
    nb = t // SB_BLOCK
    scale = d ** -0.5
    qb = jnp.moveaxis(q.reshape(b_, h, nb, SB_BLOCK, d), 2, 0)
    kpos = jnp.arange(t)

    def block(args):
        q_i, i = args
        z = jnp.einsum('bhqd,bhkd->bhqk', q_i, k) * scale
        qpos = i * SB_BLOCK + jnp.arange(SB_BLOCK)
        mask = kpos[None, :] < qpos[:, None]
        log_fail = jnp.where(mask, jax.nn.log_sigmoid(-z), 0.0)
        later = lax.cumsum(log_fail, axis=3, reverse=True) - log_fail
        a = jnp.where(mask, jnp.exp(jax.nn.log_sigmoid(z) + later), 0.0)
        return jnp.einsum('bhqk,bhkd->bhqd', a, v)

    o = lax.map(block, (qb, jnp.arange(nb)))
    return jnp.moveaxis(o, 0, 2).reshape(b_, h, t, d)


def retention_chunkwise(q, k, v):
    b_, h, t, dk = q.shape
    dv = v.shape[-1]
    c = RET_CHUNK
    n = t // c
    lg = jnp.log1p(-jnp.exp2(-5.0 - jnp.arange(h, dtype=jnp.float32)))
    idx = jnp.arange(c, dtype=jnp.float32)
    diff = idx[:, None] - idx[None, :]
    dmat = jnp.exp(jnp.where(diff >= 0, diff[None] * lg[:, None, None], -jnp.inf))
    qc = q.reshape(b_, h, n, c, dk)
    kc = k.reshape(b_, h, n, c, dk)
    vc = v.reshape(b_, h, n, c, dv)
    intra = jnp.einsum('bhnid,bhnjd->bhnij', qc, kc) * dmat[None, :, None]
    o_intra = jnp.einsum('bhnij,bhnje->bhnie', intra, vc)
    q_dec = qc * jnp.exp(lg[:, None] * (idx + 1.0)[None, :])[None, :, None, :, None]
    k_dec = kc * jnp.exp(lg[:, None] * (c - 1.0 - idx)[None, :])[None, :, None, :, None]
    chunk_decay = jnp.exp(lg * c)[None, :, None, None]
    xs = tuple(jnp.moveaxis(a, 2, 0) for a in (q_dec, k_dec, vc))

    def step(state, inp):
        q_n, k_n, v_n = inp
        o_n = jnp.einsum('bhcd,bhde->bhce', q_n, state)
        state = state * chunk_decay + jnp.einsum('bhcd,bhce->bhde', k_n, v_n)
        return state, o_n

    s0 = jnp.zeros((b_, h, dk, dv), q.dtype)
    _, o_cross = lax.scan(step, s0, xs)
    o = o_intra + jnp.moveaxis(o_cross, 0, 2)
    return o.reshape(b_, h, t, dv)


def hybrid_mixer(h, w_in, conv_w, a_log, dt_bias, gdn_norm, sb_q_norm, sb_k_norm, w_out):
    f32 = jnp.float32
    proj = h @ w_in
    gq, gk, gv, gate, ga, gb, sq, sk, sv = _split(proj, HYB_SPLITS)
    qkv = causal_short_conv(jnp.concatenate([gq, gk, gv], axis=-1), conv_w).astype(f32)
    cq, ck, cv = _split(qkv, (GDN_QK, GDN_QK, GDN_VW))
    qa = l2_norm(_heads(cq, GDN_HEADS))
    ka = l2_norm(_heads(ck, GDN_HEADS))
    va = _heads(cv, GDN_HEADS)
    g = (-jnp.exp(a_log.astype(f32)) * jax.nn.softplus(ga.astype(f32) + dt_bias.astype(f32))).transpose(0, 2, 1)
    beta = jax.nn.sigmoid(gb.astype(f32)).transpose(0, 2, 1)
    o_a = gated_delta_rule(qa, ka, va, g, beta)
    o_a = rms_norm(o_a, gdn_norm) * jax.nn.silu(_heads(gate, GDN_HEADS).astype(f32))
    qb = rms_norm(_heads(sq, SB_HEADS).astype(f32), sb_q_norm)
    kb = rms_norm(_heads(sk, SB_HEADS).astype(f32), sb_k_norm)
    vb = _heads(sv, SB_HEADS).astype(f32)
    o_b = stick_breaking_attention(qb, kb, vb)
    o = jnp.concatenate([_merge(o_a), _merge(o_b)], axis=-1).astype(h.dtype)
    return o @ w_out


def retention_mixer(h, w_in, w_out):
    f32 = jnp.float32
    proj = h @ w_in
    rq, rk, rv, rg = _split(proj, RET_SPLITS)
    pos = jnp.arange(h.shape[1], dtype=f32)
    q = rotary(_heads(rq, RET_HEADS).astype(f32), pos)
    k = rotary(_heads(rk, RET_HEADS).astype(f32), pos) * RET_DK ** -0.5
    v = _heads(rv, RET_HEADS).astype(f32)
    o = retention_chunkwise(q, k, v)
    o = rms_norm(o) * jax.nn.silu(_heads(rg, RET_HEADS).astype(f32))
    return _merge(o).astype(h.dtype) @ w_out


def swiglu(h, w_in, w_out):
    gu = h @ w_in
    g, u = gu[..., :FFN_HIDDEN], gu[..., FFN_HIDDEN:]
    return (jax.nn.silu(g) * u) @ w_out


def setup_inputs(seed: int = 0) -> dict:
    key = jax.random.key(seed)
    ks = jax.random.split(key, 20)
    f32 = jnp.float32

    def nrm(k, shape, scale):
        return jax.random.normal(k, shape, f32) * scale

    dt = jnp.exp(jax.random.uniform(ks[8], (N_EVEN, GDN_HEADS), f32, np.log(1e-3), np.log(1e-1)))
    return {
        "x": nrm(ks[0], (BATCH, SEQ, D_MODEL), 1.0),
        "c": nrm(ks[1], (BATCH, D_MODEL), 1.0),
        "ada_w": nrm(ks[2], (DEPTH, D_MODEL, 6 * D_MODEL), 0.5 * D_MODEL ** -0.5),
        "ada_b": nrm(ks[3], (DEPTH, 6 * D_MODEL), 0.02),
        "norm_mix": 1.0 + nrm(ks[4], (DEPTH, D_MODEL), 0.1),
        "norm_ffn": 1.0 + nrm(ks[5], (DEPTH, D_MODEL), 0.1),
        "hyb_w_in": nrm(ks[6], (N_EVEN, D_MODEL, HYB_IN), D_MODEL ** -0.5),
        "hyb_conv": nrm(ks[7], (N_EVEN, GDN_CONV, CONV_CH), GDN_CONV ** -0.5),
        "gdn_a_log": jnp.log(jax.random.uniform(ks[9], (N_EVEN, GDN_HEADS), f32, 1.0, 16.0)),
        "gdn_dt_bias": dt + jnp.log(-jnp.expm1(-dt)),
        "gdn_norm": 1.0 + nrm(ks[10], (N_EVEN, GDN_DV), 0.1),
        "sb_q_norm": 1.0 + nrm(ks[11], (N_EVEN, SB_DH), 0.1),
        "sb_k_norm": 1.0 + nrm(ks[12], (N_EVEN, SB_DH), 0.1),
        "hyb_w_out": nrm(ks[13], (N_EVEN, MIX_W, D_MODEL), MIX_W ** -0.5),
        "ret_w_in": nrm(ks[14], (N_ODD, D_MODEL, RET_IN), D_MODEL ** -0.5),
        "ret_w_out": nrm(ks[15], (N_ODD, RET_VW, D_MODEL), RET_VW ** -0.5),
        "ffn_w_in": nrm(ks[16], (DEPTH, D_MODEL, 2 * FFN_HIDDEN), D_MODEL ** -0.5),
        "ffn_w_out": nrm(ks[17], (DEPTH, FFN_HIDDEN, D_MODEL), FFN_HIDDEN ** -0.5),
    }


def reference(x, c, ada_w, ada_b, norm_mix, norm_ffn, hyb_w_in, hyb_conv, gdn_a_log,
              gdn_dt_bias, gdn_norm, sb_q_norm, sb_k_norm, hyb_w_out, ret_w_in, ret_w_out,
              ffn_w_in, ffn_w_out):
    c_act = jax.nn.silu(c)
    for l in range(DEPTH):
        mod = (c_act @ ada_w[l] + ada_b[l])[:, None, :]
        sh_m, sc_m, gt_m, sh_f, sc_f, gt_f = jnp.split(mod, 6, axis=-1)
        h = rms_norm(x, norm_mix[l]) * (1 + sc_m) + sh_m
        i = l // 2
        if l % 2 == 0:
            y = hybrid_mixer(h, hyb_w_in[i], hyb_conv[i], gdn_a_log[i], gdn_dt_bias[i],
                             gdn_norm[i], sb_q_norm[i], sb_k_norm[i], hyb_w_out[i])
        else:
            y = retention_mixer(h, ret_w_in[i], ret_w_out[i])
        x = x + gt_m * y
        h = rms_norm(x, norm_ffn[l]) * (1 + sc_f) + sh_f
        x = x + gt_f * swiglu(h, ffn_w_in[l], ffn_w_out[l])
    return x
```

```python
import functools

import jax
import jax.numpy as jnp
from jax import lax
from jax.experimental import pallas as pl
from jax.experimental.pallas import tpu as pltpu

F32 = jnp.float32
BF16 = jnp.bfloat16
EPS = 1e-6
ROPE_BASE = 10000.0

LANES = 128
MXU_DIM = 256
VMEM_LIMIT = 56 * 1024 * 1024

GDN_CONV = 4
GDN_BLOCK = 128
GDN_STEP = 512
SB_TILE = 256
SB_DH = 64
RET_CHUNK = 256
ROW_TILE_PROJ = 256
ROW_TILE_FFN = 512
FFN_CHUNK = 256
CONV_PAD = 8


def _dot(a, b):
    return jnp.dot(a, b, preferred_element_type=F32)


def _dot_nt(a, b):
    return lax.dot_general(a, b, (((1,), (1,)), ((), ())), preferred_element_type=F32)


def _split2(a):
    hi = a.astype(BF16)
    lo = (a - hi.astype(F32)).astype(BF16)
    return hi, lo


def _dot3(a, b):
    ah, al = _split2(a)
    bh, bl = _split2(b)
    return _dot(ah, bh) + (_dot(ah, bl) + _dot(al, bh))


def _dot_exact_rhs(a, m):
    ah, al = _split2(a)
    return _dot(ah, m) + _dot(al, m)


def _dot_exact_lhs(m, a):
    ah, al = _split2(a)
    return _dot(m, ah) + _dot(m, al)


def _silu(x):
    return x * jax.nn.sigmoid(x)


def _softplus(x):
    return jnp.maximum(x, 0.0) + jnp.log1p(jnp.exp(-jnp.abs(x)))


def _params(n_axes):
    return pltpu.CompilerParams(dimension_semantics=("arbitrary",) * n_axes,
                                vmem_limit_bytes=VMEM_LIMIT)


def _resident(shape, index_map):
    return pl.BlockSpec(shape, index_map, pipeline_mode=pl.Buffered(1))


def _mod_kernel(c_ref, w_ref, b_ref, o_ref):
    c = c_ref[...]
    o_ref[0] = _dot(_silu(c).astype(BF16), w_ref[0].astype(BF16)) + b_ref[0]


def _modulation(c, ada_w, ada_b):
    depth, d, n = ada_w.shape
    b = c.shape[0]
    rows = 8
    c_pad = jnp.zeros((rows, d), F32).at[:b].set(c)
    tn = 1024
    out = pl.pallas_call(
        _mod_kernel,
        grid=(depth, n // tn),
        in_specs=[pl.BlockSpec((rows, d), lambda l, j: (0, 0)),
                  pl.BlockSpec((1, d, tn), lambda l, j: (l, 0, j)),
                  pl.BlockSpec((1, 1, tn), lambda l, j: (l, 0, j))],
        out_specs=pl.BlockSpec((1, rows, tn), lambda l, j: (l, 0, j)),
        out_shape=jax.ShapeDtypeStruct((depth, rows, n), F32),
        compiler_params=_params(2),
        name="adaln_modulation",
    )(c_pad, ada_w, ada_b.reshape(depth, 1, n))
    return out[:, :b]


def _norm_mod(x, gain, sc, sh):
    ms = jnp.mean(x * x, axis=-1, keepdims=True)
    y = x * lax.rsqrt(ms + EPS)
    y = y * gain
    return y * (1.0 + sc) + sh


def _hyb_proj_kernel(x_ref, g_ref, sc_ref, sh_ref, w_ref, gm_ref, qg_ref, kg_ref,
                     gdn_ref, ab_ref, q_ref, k_ref, v_ref):
    hb = _norm_mod(x_ref[...], g_ref[...], sc_ref[0], sh_ref[0]).astype(BF16)
    n_gdn = gdn_ref.shape[1]
    w_sb = q_ref.shape[1]
    step = 512
    for off in range(0, n_gdn, step):
        gdn_ref[:, off:off + step] = _dot(hb, w_ref[:, off:off + step])
    off = n_gdn
    ab_ref[...] = _dot(hb, w_ref[:, off:off + LANES])
    off += LANES

    def head_rms(xf, gain):
        ms = _dot_exact_rhs(xf * xf, gm_ref[...])
        return xf * lax.rsqrt(ms + EPS) * gain

    sq = _dot(hb, w_ref[:, off:off + w_sb])
    q_ref[...] = (head_rms(sq, qg_ref[...]) * (SB_DH ** -0.5)).astype(BF16)
    off += w_sb
    sk = _dot(hb, w_ref[:, off:off + w_sb])
    k_ref[...] = head_rms(sk, kg_ref[...]).astype(BF16)
    off += w_sb
    v_ref[...] = _dot(hb, w_ref[:, off:off + w_sb]).astype(BF16)


def _hyb_proj(x, gain, sc, sh, w, gmat, qg, kg, seq):
    n, d = x.shape
    tm = ROW_TILE_PROJ
    per_b = seq // tm
    w_sb = qg.shape[1]
    n_gdn = w.shape[1] - LANES - 3 * w_sb
    bmap = lambda i: (i // per_b, 0, 0)
    row = lambda i: (i, 0)
    const = lambda i: (0, 0)
    return pl.pallas_call(
        _hyb_proj_kernel,
        grid=(n // tm,),
        in_specs=[pl.BlockSpec((tm, d), row),
                  pl.BlockSpec((1, d), const),
                  pl.BlockSpec((1, 1, d), bmap),
                  pl.BlockSpec((1, 1, d), bmap),
                  _resident(w.shape, const),
                  _resident(gmat.shape, const),
                  pl.BlockSpec((1, w_sb), const),
                  pl.BlockSpec((1, w_sb), const)],
        out_specs=[pl.BlockSpec((tm, n_gdn), row),
                   pl.BlockSpec((tm, LANES), row),
                   pl.BlockSpec((tm, w_sb), row),
                   pl.BlockSpec((tm, w_sb), row),
                   pl.BlockSpec((tm, w_sb), row)],
        out_shape=[jax.ShapeDtypeStruct((n, n_gdn), F32),
                   jax.ShapeDtypeStruct((n, LANES), F32),
                   jax.ShapeDtypeStruct((n, w_sb), BF16),
                   jax.ShapeDtypeStruct((n, w_sb), BF16),
                   jax.ShapeDtypeStruct((n, w_sb), BF16)],
        compiler_params=_params(1),
        name="hybrid_in_proj",
    )(x, gain, sc, sh, w, gmat, qg, kg)


def _unit_lower_inverse(lower, eye, row, col):
    bits = 3
    diag = jnp.where((row >> bits) == (col >> bits), lower, 0.0)
    x = eye - diag
    p = _dot3(diag, diag)
    x = x + _dot3(x, p)
    p = _dot3(p, p)
    x = x + _dot3(x, p)
    while (1 << bits) < lower.shape[0]:
        same_parent = (row >> (bits + 1)) == (col >> (bits + 1))
        off = jnp.where(same_parent, jnp.where((row >> bits) == (col >> bits), 0.0, lower), 0.0)
        x = x - _dot3(_dot3(x, off), x)
        bits += 1
    return x


def _gdn_kernel(q_ref, k_ref, v_ref, gate_ref, ab_ref, wq_ref, wk_ref, wv_ref,
                alog_ref, dtb_ref, gn_ref, o_ref,
                state_ref, hist_ref, ext_ref):
    h = pl.program_id(1)
    t = pl.program_id(2)
    tb = q_ref.shape[0]
    c = GDN_BLOCK
    dk = q_ref.shape[1]

    @pl.when(t == 0)
    def _():
        state_ref[...] = jnp.zeros_like(state_ref)
        hist_ref[...] = jnp.zeros_like(hist_ref)

    def conv(idx, x_ref, w_ref):
        x = x_ref[...]
        ext_ref[0:CONV_PAD, :] = hist_ref[idx]
        ext_ref[CONV_PAD:CONV_PAD + tb, :] = x
        hist_ref[idx] = x[tb - CONV_PAD:tb, :]
        w = w_ref[...]
        y = x * w[GDN_CONV - 1:GDN_CONV, :]
        for j in range(GDN_CONV - 1):
            shift = GDN_CONV - 1 - j
            y = y + ext_ref[CONV_PAD - shift:CONV_PAD - shift + tb, :] * w[j:j + 1, :]
        return _silu(y)

    cq = conv(0, q_ref, wq_ref)
    ck = conv(1, k_ref, wk_ref)
    cv = conv(2, v_ref, wv_ref)

    lane = lax.broadcasted_iota(jnp.int32, (tb, LANES), 1)
    ab = ab_ref[...]
    n_heads = pl.num_programs(1)
    ga = jnp.sum(jnp.where(lane == h, ab, 0.0), axis=1, keepdims=True)
    gb = jnp.sum(jnp.where(lane == h + n_heads, ab, 0.0), axis=1, keepdims=True)
    neg_a = -jnp.exp(alog_ref[0])
    g_all = neg_a * _softplus(ga + dtb_ref[0])
    beta_all = jnp.broadcast_to(jax.nn.sigmoid(gb), (tb, LANES))

    row = lax.broadcasted_iota(jnp.int32, (c, c), 0)
    col = lax.broadcasted_iota(jnp.int32, (c, c), 1)
    incl = row >= col
    strict = row > col
    eye = jnp.where(row == col, 1.0, 0.0).astype(F32)
    tri = jnp.where(incl, 1.0, 0.0).astype(BF16)
    scale = dk ** -0.5

    def l2n(x):
        return x * lax.rsqrt(jnp.sum(x * x, axis=-1, keepdims=True) + EPS)

    prepped = []
    for n in range(tb // c):
        sl = slice(n * c, (n + 1) * c)
        q = l2n(cq[sl]) * scale
        k = l2n(ck[sl])
        v = cv[sl]
        beta = beta_all[sl]
        gcol = _dot_exact_lhs(tri, g_all[sl])
        grow = gcol.T
        dec = jnp.exp(jnp.where(incl, gcol - grow, 0.0))
        eg = jnp.exp(gcol)
        g_last = gcol[c - 1:c, :]
        kb = k * beta
        kbb = kb.astype(BF16)
        kk = k.astype(BF16)
        lower = jnp.where(strict, _dot_nt(kbb, kk) * dec, 0.0)
        tinv = _unit_lower_inverse(lower, eye, row, col)
        u = _dot3(tinv, v * beta)
        w = _dot3(tinv, kb * eg)
        a = jnp.where(incl, _dot_nt(q.astype(BF16), kk) * dec, 0.0)
        q_dec = q * eg
        k_dec = k * jnp.exp(g_last - gcol)
        prepped.append((u, w.astype(BF16), a.astype(BF16), q_dec.astype(BF16),
                        k_dec.T.astype(BF16), jnp.exp(g_last)))

    gn = gn_ref[...]
    state = state_ref[...]
    for n, (u, w, a, q_dec, k_dec_t, decay) in enumerate(prepped):
        sl = slice(n * c, (n + 1) * c)
        sb = state.astype(BF16)
        v_new = u - _dot(w, sb)
        o = _dot(q_dec, sb) + _dot(a, v_new.astype(BF16))
        state = state * decay + _dot(k_dec_t, v_new.astype(BF16))
        ms = jnp.mean(o * o, axis=-1, keepdims=True)
        o = o * lax.rsqrt(ms + EPS) * gn
        o_ref[sl, :] = (o * _silu(gate_ref[sl, :])).astype(o_ref.dtype)
    state_ref[...] = state


def _gdn(gdn_proj, ab, conv_w, a_log, dt_bias, gdn_norm, batch, seq):
    n, width = gdn_proj.shape
    n_heads = a_log.shape[0]
    dk = width // (4 * n_heads)
    tb = GDN_STEP
    nt = seq // tb
    rows = lambda b, h, t: b * nt + t
    alog_b = jnp.broadcast_to(a_log.reshape(n_heads, 1, 1), (n_heads, 1, LANES))
    dtb_b = jnp.broadcast_to(dt_bias.reshape(n_heads, 1, 1), (n_heads, 1, LANES))
    col_spec = lambda k: pl.BlockSpec((tb, dk), lambda b, h, t: (rows(b, h, t), k * n_heads + h))
    w_spec = lambda k: pl.BlockSpec((GDN_CONV, dk), lambda b, h, t: (0, k * n_heads + h))
    head_vec = pl.BlockSpec((1, 1, LANES), lambda b, h, t: (h, 0, 0))
    return pl.pallas_call(
        _gdn_kernel,
        grid=(batch, n_heads, nt),
        in_specs=[col_spec(0), col_spec(1), col_spec(2), col_spec(3),
                  pl.BlockSpec((tb, LANES), lambda b, h, t: (rows(b, h, t), 0)),
                  w_spec(0), w_spec(1), w_spec(2),
                  head_vec, head_vec,
                  pl.BlockSpec((1, dk), lambda b, h, t: (0, 0))],
        out_specs=pl.BlockSpec((tb, dk), lambda b, h, t: (rows(b, h, t), h)),
        out_shape=jax.ShapeDtypeStruct((n, n_heads * dk), BF16),
        scratch_shapes=[pltpu.VMEM((dk, dk), F32),
                        pltpu.VMEM((3, CONV_PAD, dk), F32),
                        pltpu.VMEM((CONV_PAD + tb, dk), F32)],
        compiler_params=_params(3),
        name="gated_delta_rule",
    )(gdn_proj, gdn_proj, gdn_proj, gdn_proj, ab, conv_w, conv_w, conv_w,
      alog_b, dtb_b, gdn_norm.reshape(1, dk))


def _sb_kernel(q_ref, k_ref, v_ref, m_ref, o_ref, q2_ref, acc_ref, carry_ref):
    i = pl.program_id(2)
    tq = q_ref.shape[0]
    lane = lax.broadcasted_iota(jnp.int32, (tq, LANES), 1)
    q = q_ref[...]
    zero = jnp.zeros_like(q)
    q2_ref[0:tq, :] = jnp.where(lane < SB_DH, q, zero)
    q2_ref[tq:2 * tq, :] = jnp.where(lane >= SB_DH, q, zero)
    q2 = q2_ref[...]
    suffix = m_ref[...]

    def tile(j, diagonal):
        start = pl.multiple_of(j * tq, tq)
        k = k_ref[pl.ds(start, tq), :]
        v = v_ref[pl.ds(start, tq), :]
        z = _dot_nt(q2, k)
        sp = jnp.maximum(z, 0.0) + jnp.log(1.0 + jnp.exp(-jnp.abs(z)))
        if diagonal:
            r = lax.broadcasted_iota(jnp.int32, (2 * tq, tq), 0) & (tq - 1)
            s = lax.broadcasted_iota(jnp.int32, (2 * tq, tq), 1)
            mask = s < r
            lf = jnp.where(mask, -sp, 0.0)
        else:
            lf = -sp
        later = _dot_exact_rhs(lf, suffix)
        if diagonal:
            a = jnp.where(mask, jnp.exp(z - sp + later), 0.0)
            acc_ref[...] = _dot(a.astype(BF16), v)
            carry_ref[...] = jnp.sum(lf, axis=1, keepdims=True)
        else:
            a = jnp.exp(z - sp + (later + carry_ref[...]))
            acc_ref[...] += _dot(a.astype(BF16), v)
            carry_ref[...] += jnp.sum(lf, axis=1, keepdims=True)

    tile(i, True)

    def body(step, _):
        tile(i - 1 - step, False)
        return 0

    lax.fori_loop(0, i, body, 0)
    acc = acc_ref[...]
    o_ref[...] = jnp.where(lane < SB_DH, acc[0:tq], acc[tq:2 * tq]).astype(o_ref.dtype)


def _sb_attention(qn, kn, vn, suffix, batch, seq):
    n, width = qn.shape
    pairs = width // LANES
    tq = SB_TILE
    nq = seq // tq
    return pl.pallas_call(
        _sb_kernel,
        grid=(batch, pairs, nq),
        in_specs=[pl.BlockSpec((tq, LANES), lambda b, p, i: (b * nq + i, p)),
                  pl.BlockSpec((seq, LANES), lambda b, p, i: (b, p)),
                  pl.BlockSpec((seq, LANES), lambda b, p, i: (b, p)),
                  pl.BlockSpec((tq, tq), lambda b, p, i: (0, 0))],
        out_specs=pl.BlockSpec((tq, LANES), lambda b, p, i: (b * nq + i, p)),
        out_shape=jax.ShapeDtypeStruct((n, width), BF16),
        scratch_shapes=[pltpu.VMEM((2 * tq, LANES), BF16),
                        pltpu.VMEM((2 * tq, LANES), F32),
                        pltpu.VMEM((2 * tq, 1), F32)],
        compiler_params=_params(3),
        name="stick_breaking_attention",
    )(qn, kn, vn, suffix)


def _ret_proj_kernel(x_ref, g_ref, sc_ref, sh_ref, w_ref, qk_ref, v_ref, gate_ref):
    hb = _norm_mod(x_ref[...], g_ref[...], sc_ref[0], sh_ref[0]).astype(BF16)
    step = 512
    off = 0
    for ref in (qk_ref, v_ref, gate_ref):
        for o in range(0, ref.shape[1], step):
            ref[:, o:o + step] = _dot(hb, w_ref[:, off + o:off + o + step]).astype(ref.dtype)
        off += ref.shape[1]


def _ret_proj(x, gain, sc, sh, w, n_qk, n_v, seq):
    n, d = x.shape
    tm = ROW_TILE_PROJ
    per_b = seq // tm
    bmap = lambda i: (i // per_b, 0, 0)
    row = lambda i: (i, 0)
    const = lambda i: (0, 0)
    return pl.pallas_call(
        _ret_proj_kernel,
        grid=(n // tm,),
        in_specs=[pl.BlockSpec((tm, d), row),
                  pl.BlockSpec((1, d), const),
                  pl.BlockSpec((1, 1, d), bmap),
                  pl.BlockSpec((1, 1, d), bmap),
                  _resident(w.shape, const)],
        out_specs=[pl.BlockSpec((tm, n_qk), row),
                   pl.BlockSpec((tm, n_v), row),
                   pl.BlockSpec((tm, n_v), row)],
        out_shape=[jax.ShapeDtypeStruct((n, n_qk), F32),
                   jax.ShapeDtypeStruct((n, n_v), BF16),
                   jax.ShapeDtypeStruct((n, n_v), F32)],
        compiler_params=_params(1),
        name="retention_in_proj",
    )(x, gain, sc, sh, w)


def _ret_kernel(q_ref, k_ref, v_ref, gate_ref, cos_ref, sin_ref, o_ref, state_ref):
    h = pl.program_id(1)
    t = pl.program_id(2)
    c, dk = q_ref.shape
    half = dk // 2

    @pl.when(t == 0)
    def _():
        state_ref[...] = jnp.zeros_like(state_ref)

    cos = cos_ref[...]
    sin = sin_ref[...]

    def rot(x):
        x1, x2 = x[:, :half], x[:, half:]
        return jnp.concatenate([x1 * cos - x2 * sin, x1 * sin + x2 * cos], axis=-1)

    q = rot(q_ref[...])
    k = rot(k_ref[...]) * (dk ** -0.5)
    v = v_ref[...]

    hf = jnp.full((c, c), h, jnp.int32).astype(F32)
    lg = jnp.log1p(-jnp.exp2(-5.0 - hf))
    row = lax.broadcasted_iota(jnp.int32, (c, c), 0)
    col = lax.broadcasted_iota(jnp.int32, (c, c), 1)
    causal = row >= col
    rowf = row.astype(F32)
    diff = rowf - col.astype(F32)
    dmat = jnp.where(causal, jnp.exp(jnp.where(causal, diff, 0.0) * lg), 0.0)
    lg_k = lg[:, :dk] if dk <= c else jnp.broadcast_to(lg[:, :1], (c, dk))
    rowk = lax.broadcasted_iota(jnp.int32, (c, dk), 0).astype(F32)
    q_dec = q * jnp.exp(lg_k * (rowk + 1.0))
    k_dec = k * jnp.exp(lg_k * (c - 1.0 - rowk))
    chunk_decay = jnp.exp(lg[0:1, 0:1] * c)

    qb = q.astype(BF16)
    kb = k.astype(BF16)
    intra = _dot_nt(qb, kb) * dmat
    state = state_ref[...]
    o = _dot(intra.astype(BF16), v) + _dot(q_dec.astype(BF16), state.astype(BF16))
    state_ref[...] = state * chunk_decay + _dot(k_dec.T.astype(BF16), v)
    ms = jnp.mean(o * o, axis=-1, keepdims=True)
    o = o * lax.rsqrt(ms + EPS)
    o_ref[...] = (o * _silu(gate_ref[...])).astype(o_ref.dtype)


def _retention(qk, v, gate, cos, sin, n_heads, batch, seq):
    n = qk.shape[0]
    dk = qk.shape[1] // (2 * n_heads)
    dv = v.shape[1] // n_heads
    c = RET_CHUNK
    nt = seq // c
    rows = lambda b, h, t: b * nt + t
    return pl.pallas_call(
        _ret_kernel,
        grid=(batch, n_heads, nt),
        in_specs=[pl.BlockSpec((c, dk), lambda b, h, t: (rows(b, h, t), h)),
                  pl.BlockSpec((c, dk), lambda b, h, t: (rows(b, h, t), n_heads + h)),
                  pl.BlockSpec((c, dv), lambda b, h, t: (rows(b, h, t), h)),
                  pl.BlockSpec((c, dv), lambda b, h, t: (rows(b, h, t), h)),
                  pl.BlockSpec((c, dk // 2), lambda b, h, t: (t, 0)),
                  pl.BlockSpec((c, dk // 2), lambda b, h, t: (t, 0))],
        out_specs=pl.BlockSpec((c, dv), lambda b, h, t: (rows(b, h, t), h)),
        out_shape=jax.ShapeDtypeStruct((n, n_heads * dv), BF16),
        scratch_shapes=[pltpu.VMEM((dk, dv), F32)],
        compiler_params=_params(3),
        name="retention_chunkwise",
    )(qk, qk, v, gate, cos, sin)


def _post_kernel(*refs, n_mix):
    x_ref = refs[0]
    o_refs = refs[1:1 + n_mix]
    wm_refs = refs[1 + n_mix:1 + 2 * n_mix]
    gtm_ref, g_ref, sc_ref, sh_ref, gtf_ref, win_ref, wout_ref, out_ref, acc_ref = refs[1 + 2 * n_mix:]
    y = _dot(o_refs[0][...], wm_refs[0][...])
    for o_ref, w_ref in zip(o_refs[1:], wm_refs[1:]):
        y = y + _dot(o_ref[...], w_ref[...])
    x1 = x_ref[...] + gtm_ref[0] * y
    hb = _norm_mod(x1, g_ref[...], sc_ref[0], sh_ref[0]).astype(BF16)
    hidden = wout_ref.shape[0]
    for idx, off in enumerate(range(0, hidden, FFN_CHUNK)):
        g = _dot(hb, win_ref[:, off:off + FFN_CHUNK])
        u = _dot(hb, win_ref[:, hidden + off:hidden + off + FFN_CHUNK])
        part = _dot((_silu(g) * u).astype(BF16), wout_ref[off:off + FFN_CHUNK, :])
        if idx == 0:
            acc_ref[...] = part
        else:
            acc_ref[...] += part
    out_ref[...] = x1 + gtf_ref[0] * acc_ref[...]


def _post(x, mix_outs, mix_ws, gt_m, gain, sc, sh, gt_f, w_in, w_out, seq):
    n, d = x.shape
    tm = ROW_TILE_FFN
    per_b = seq // tm
    bmap = lambda i: (i // per_b, 0, 0)
    row = lambda i: (i, 0)
    const = lambda i: (0, 0)
    vec = pl.BlockSpec((1, 1, d), bmap)
    n_mix = len(mix_outs)
    return pl.pallas_call(
        functools.partial(_post_kernel, n_mix=n_mix),
        grid=(n // tm,),
        in_specs=([pl.BlockSpec((tm, d), row)]
                  + [pl.BlockSpec((tm, o.shape[1]), row) for o in mix_outs]
                  + [_resident(w.shape, const) for w in mix_ws]
                  + [vec, pl.BlockSpec((1, d), const), vec, vec, vec,
                     _resident(w_in.shape, const), _resident(w_out.shape, const)]),
        out_specs=pl.BlockSpec((tm, d), row),
        out_shape=jax.ShapeDtypeStruct((n, d), F32),
        scratch_shapes=[pltpu.VMEM((tm, d), F32)],
        compiler_params=_params(1),
        name="out_proj_swiglu",
    )(x, *mix_outs, *mix_ws, gt_m, gain, sc, sh, gt_f, w_in, w_out)


def kernel(x, c, ada_w, ada_b, norm_mix, norm_ffn, hyb_w_in, hyb_conv, gdn_a_log, gdn_dt_bias, gdn_norm, sb_q_norm, sb_k_norm, hyb_w_out, ret_w_in, ret_w_out, ffn_w_in, ffn_w_out):
    batch, seq, d = x.shape
    depth = ada_w.shape[0]
    gdn_heads = gdn_a_log.shape[1]
    gdn_dv = gdn_norm.shape[1]
    gdn_w = gdn_heads * gdn_dv
    sb_dh = sb_q_norm.shape[1]
    assert sb_dh == SB_DH
    sb_w = hyb_w_out.shape[1] - gdn_w
    sb_heads = sb_w // sb_dh
    ret_heads = 4
    ret_qk = ret_w_in.shape[2] // 6
    ret_v = 2 * ret_qk
    ret_dk = ret_qk // ret_heads

    xf = x.reshape(batch * seq, d)
    mod = _modulation(c, ada_w, ada_b)

    head_mean = jnp.kron(jnp.eye(sb_heads, dtype=F32), jnp.full((sb_dh, sb_dh), 1.0 / sb_dh, F32)).astype(BF16)
    idx = jnp.arange(SB_TILE)
    suffix = (idx[:, None] > idx[None, :]).astype(BF16)
    pos = jnp.arange(seq, dtype=F32)
    inv = 1.0 / (ROPE_BASE ** (jnp.arange(0, ret_dk, 2, dtype=F32) / ret_dk))
    ang = pos[:, None] * inv[None, :]
    cos, sin = jnp.cos(ang), jnp.sin(ang)

    for l in range(depth):
        sh_m, sc_m, gt_m, sh_f, sc_f, gt_f = [m.reshape(batch, 1, d) for m in jnp.split(mod[l], 6, axis=-1)]
        i = l // 2
        gain_m = norm_mix[l].reshape(1, d)
        if l % 2 == 0:
            w_in = hyb_w_in[i]
            n_main = 4 * gdn_w
            n_ab = 2 * gdn_heads
            ab_cols = jnp.zeros((d, LANES), F32).at[:, :n_ab].set(w_in[:, n_main:n_main + n_ab])
            w_re = jnp.concatenate([w_in[:, :n_main], ab_cols, w_in[:, n_main + n_ab:]], axis=1).astype(BF16)
            gdn_proj, ab, qn, kn, vn = _hyb_proj(
                xf, gain_m, sc_m, sh_m, w_re, head_mean,
                jnp.tile(sb_q_norm[i], sb_heads).reshape(1, sb_w),
                jnp.tile(sb_k_norm[i], sb_heads).reshape(1, sb_w), seq)
            o_a = _gdn(gdn_proj, ab, hyb_conv[i], gdn_a_log[i], gdn_dt_bias[i], gdn_norm[i], batch, seq)
            o_b = _sb_attention(qn, kn, vn, suffix, batch, seq)
            w_out = hyb_w_out[i].astype(BF16)
            mix_outs = [o_a, o_b]
            mix_ws = [w_out[:gdn_w], w_out[gdn_w:]]
        else:
            qk, v, gate = _ret_proj(xf, gain_m, sc_m, sh_m, ret_w_in[i].astype(BF16), 2 * ret_qk, ret_v, seq)
            o_r = _retention(qk, v, gate, cos, sin, ret_heads, batch, seq)
            mix_outs = [o_r]
            mix_ws = [ret_w_out[i].astype(BF16)]
        xf = _post(xf, mix_outs, mix_ws, gt_m, norm_ffn[l].reshape(1, d), sc_f, sh_f, gt_f,
                   ffn_w_in[l].astype(BF16), ffn_w_out[l].astype(BF16), seq)
    return xf.reshape(batch, seq, d)
```

```python
import functools

import jax
import jax.numpy as jnp
from jax import lax
from jax.experimental import pallas as pl
from jax.experimental.pallas import tpu as pltpu

F32 = jnp.float32
BF16 = jnp.bfloat16
EPS = 1e-6
ROPE_BASE = 10000.0

LANES = 128
MXU_DIM = 256
VMEM_LIMIT = 56 * 1024 * 1024

GDN_CONV = 4
GDN_BLOCK = 128
GDN_STEP = 512
SB_TILE = 256
SB_DH = 64
LOG2_E = 1.4426950408889634
SB_UNDERFLOW_LOG2 = 180.0
RET_CHUNK = 256
ROW_TILE_PROJ = 256
ROW_TILE_FFN = 512
FFN_CHUNK = 256
CONV_PAD = 8


def _dot(a, b):
    return jnp.dot(a, b, preferred_element_type=F32)


def _dot_nt(a, b):
    return lax.dot_general(a, b, (((1,), (1,)), ((), ())), preferred_element_type=F32)


def _split2(a):
    hi = a.astype(BF16)
    lo = (a - hi.astype(F32)).astype(BF16)
    return hi, lo


def _dot3(a, b):
    ah, al = _split2(a)
    bh, bl = _split2(b)
    return _dot(ah, bh) + (_dot(ah, bl) + _dot(al, bh))


def _dot_exact_rhs(a, m):
    ah, al = _split2(a)
    return _dot(ah, m) + _dot(al, m)


def _dot_exact_lhs(m, a):
    ah, al = _split2(a)
    return _dot(m, ah) + _dot(m, al)


def _silu(x):
    return x * jax.nn.sigmoid(x)


def _softplus(x):
    return jnp.maximum(x, 0.0) + jnp.log1p(jnp.exp(-jnp.abs(x)))


def _params(n_axes):
    return pltpu.CompilerParams(dimension_semantics=("arbitrary",) * n_axes,
                                vmem_limit_bytes=VMEM_LIMIT)


def _resident(shape, index_map):
    return pl.BlockSpec(shape, index_map, pipeline_mode=pl.Buffered(1))


def _mod_kernel(c_ref, w_ref, b_ref, o_ref):
    c = c_ref[...]
    o_ref[0] = _dot(_silu(c).astype(BF16), w_ref[0].astype(BF16)) + b_ref[0]


def _modulation(c, ada_w, ada_b):
    depth, d, n = ada_w.shape
    b = c.shape[0]
    rows = 8
    c_pad = jnp.zeros((rows, d), F32).at[:b].set(c)
    tn = 1024
    out = pl.pallas_call(
        _mod_kernel,
        grid=(depth, n // tn),
        in_specs=[pl.BlockSpec((rows, d), lambda l, j: (0, 0)),
                  pl.BlockSpec((1, d, tn), lambda l, j: (l, 0, j)),
                  pl.BlockSpec((1, 1, tn), lambda l, j: (l, 0, j))],
        out_specs=pl.BlockSpec((1, rows, tn), lambda l, j: (l, 0, j)),
        out_shape=jax.ShapeDtypeStruct((depth, rows, n), F32),
        compiler_params=_params(2),
        name="adaln_modulation",
    )(c_pad, ada_w, ada_b.reshape(depth, 1, n))
    return out[:, :b]


def _norm_mod(x, gain, sc, sh):
    ms = jnp.mean(x * x, axis=-1, keepdims=True)
    y = x * lax.rsqrt(ms + EPS)
    y = y * gain
    return y * (1.0 + sc) + sh


def _hyb_proj_kernel(x_ref, g_ref, sc_ref, sh_ref, w_ref, gm_ref, qg_ref, kg_ref,
                     gdn_ref, ab_ref, q_ref, k_ref, v_ref):
    hb = _norm_mod(x_ref[...], g_ref[...], sc_ref[0], sh_ref[0]).astype(BF16)
    n_gdn = gdn_ref.shape[1]
    w_sb = q_ref.shape[1]
    step = 512
    for off in range(0, n_gdn, step):
        gdn_ref[:, off:off + step] = _dot(hb, w_ref[:, off:off + step])
    off = n_gdn
    ab_ref[...] = _dot(hb, w_ref[:, off:off + LANES])
    off += LANES

    def head_rms(xf, gain):
        ms = _dot_exact_rhs(xf * xf, gm_ref[...])
        return xf * lax.rsqrt(ms + EPS) * gain

    sq = _dot(hb, w_ref[:, off:off + w_sb])
    q_ref[...] = (head_rms(sq, qg_ref[...]) * (SB_DH ** -0.5)).astype(BF16)
    off += w_sb
    sk = _dot(hb, w_ref[:, off:off + w_sb])
    k_ref[...] = head_rms(sk, kg_ref[...]).astype(BF16)
    off += w_sb
    v_ref[...] = _dot(hb, w_ref[:, off:off + w_sb]).astype(BF16)


def _hyb_proj(x, gain, sc, sh, w, gmat, qg, kg, seq):
    n, d = x.shape
    tm = ROW_TILE_PROJ
    per_b = seq // tm
    w_sb = qg.shape[1]
    n_gdn = w.shape[1] - LANES - 3 * w_sb
    bmap = lambda i: (i // per_b, 0, 0)
    row = lambda i: (i, 0)
    const = lambda i: (0, 0)
    return pl.pallas_call(
        _hyb_proj_kernel,
        grid=(n // tm,),
        in_specs=[pl.BlockSpec((tm, d), row),
                  pl.BlockSpec((1, d), const),
                  pl.BlockSpec((1, 1, d), bmap),
                  pl.BlockSpec((1, 1, d), bmap),
                  _resident(w.shape, const),
                  _resident(gmat.shape, const),
                  pl.BlockSpec((1, w_sb), const),
                  pl.BlockSpec((1, w_sb), const)],
        out_specs=[pl.BlockSpec((tm, n_gdn), row),
                   pl.BlockSpec((tm, LANES), row),
                   pl.BlockSpec((tm, w_sb), row),
                   pl.BlockSpec((tm, w_sb), row),
                   pl.BlockSpec((tm, w_sb), row)],
        out_shape=[jax.ShapeDtypeStruct((n, n_gdn), F32),
                   jax.ShapeDtypeStruct((n, LANES), F32),
                   jax.ShapeDtypeStruct((n, w_sb), BF16),
                   jax.ShapeDtypeStruct((n, w_sb), BF16),
                   jax.ShapeDtypeStruct((n, w_sb), BF16)],
        compiler_params=_params(1),
        name="hybrid_in_proj",
    )(x, gain, sc, sh, w, gmat, qg, kg)


def _unit_lower_inverse(lower, eye, row, col):
    bits = 3
    diag = jnp.where((row >> bits) == (col >> bits), lower, 0.0)
    x = eye - diag
    p = _dot3(diag, diag)
    x = x + _dot3(x, p)
    p = _dot3(p, p)
    x = x + _dot3(x, p)
    while (1 << bits) < lower.shape[0]:
        same_parent = (row >> (bits + 1)) == (col >> (bits + 1))
        off = jnp.where(same_parent, jnp.where((row >> bits) == (col >> bits), 0.0, lower), 0.0)
        x = x - _dot3(_dot3(x, off), x)
        bits += 1
    return x


def _gdn_kernel(q_ref, k_ref, v_ref, gate_ref, ab_ref, wq_ref, wk_ref, wv_ref,
                alog_ref, dtb_ref, gn_ref, o_ref,
                state_ref, hist_ref, ext_ref):
    h = pl.program_id(1)
    t = pl.program_id(2)
    tb = q_ref.shape[0]
    c = GDN_BLOCK
    dk = q_ref.shape[1]

    @pl.when(t == 0)
    def _():
        state_ref[...] = jnp.zeros_like(state_ref)
        hist_ref[...] = jnp.zeros_like(hist_ref)

    def conv(idx, x_ref, w_ref):
        x = x_ref[...]
        ext_ref[0:CONV_PAD, :] = hist_ref[idx]
        ext_ref[CONV_PAD:CONV_PAD + tb, :] = x
        hist_ref[idx] = x[tb - CONV_PAD:tb, :]
        w = w_ref[...]
        y = x * w[GDN_CONV - 1:GDN_CONV, :]
        for j in range(GDN_CONV - 1):
            shift = GDN_CONV - 1 - j
            y = y + ext_ref[CONV_PAD - shift:CONV_PAD - shift + tb, :] * w[j:j + 1, :]
        return _silu(y)

    cq = conv(0, q_ref, wq_ref)
    ck = conv(1, k_ref, wk_ref)
    cv = conv(2, v_ref, wv_ref)

    lane = lax.broadcasted_iota(jnp.int32, (tb, LANES), 1)
    ab = ab_ref[...]
    n_heads = pl.num_programs(1)
    ga = jnp.sum(jnp.where(lane == h, ab, 0.0), axis=1, keepdims=True)
    gb = jnp.sum(jnp.where(lane == h + n_heads, ab, 0.0), axis=1, keepdims=True)
    neg_a = -jnp.exp(alog_ref[0])
    g_all = neg_a * _softplus(ga + dtb_ref[0])
    beta_all = jnp.broadcast_to(jax.nn.sigmoid(gb), (tb, LANES))

    row = lax.broadcasted_iota(jnp.int32, (c, c), 0)
    col = lax.broadcasted_iota(jnp.int32, (c, c), 1)
    incl = row >= col
    strict = row > col
    eye = jnp.where(row == col, 1.0, 0.0).astype(F32)
    tri = jnp.where(incl, 1.0, 0.0).astype(BF16)
    scale = dk ** -0.5

    def l2n(x):
        return x * lax.rsqrt(jnp.sum(x * x, axis=-1, keepdims=True) + EPS)

    prepped = []
    for n in range(tb // c):
        sl = slice(n * c, (n + 1) * c)
        q = l2n(cq[sl]) * scale
        k = l2n(ck[sl])
        v = cv[sl]
        beta = beta_all[sl]
        gcol = _dot_exact_lhs(tri, g_all[sl])
        grow = gcol.T
        dec = jnp.exp(jnp.where(incl, gcol - grow, 0.0))
        eg = jnp.exp(gcol)
        g_last = gcol[c - 1:c, :]
        kb = k * beta
        kbb = kb.astype(BF16)
        kk = k.astype(BF16)
        lower = jnp.where(strict, _dot_nt(kbb, kk) * dec, 0.0)
        tinv = _unit_lower_inverse(lower, eye, row, col)
        u = _dot3(tinv, v * beta)
        w = _dot3(tinv, kb * eg)
        a = jnp.where(incl, _dot_nt(q.astype(BF16), kk) * dec, 0.0)
        q_dec = q * eg
        k_dec = k * jnp.exp(g_last - gcol)
        prepped.append((u, w.astype(BF16), a.astype(BF16), q_dec.astype(BF16),
                        k_dec.T.astype(BF16), jnp.exp(g_last)))

    gn = gn_ref[...]
    state = state_ref[...]
    for n, (u, w, a, q_dec, k_dec_t, decay) in enumerate(prepped):
        sl = slice(n * c, (n + 1) * c)
        sb = state.astype(BF16)
        v_new = u - _dot(w, sb)
        o = _dot(q_dec, sb) + _dot(a, v_new.astype(BF16))
        state = state * decay + _dot(k_dec_t, v_new.astype(BF16))
        ms = jnp.mean(o * o, axis=-1, keepdims=True)
        o = o * lax.rsqrt(ms + EPS) * gn
        o_ref[sl, :] = (o * _silu(gate_ref[sl, :])).astype(o_ref.dtype)
    state_ref[...] = state


def _gdn(gdn_proj, ab, conv_w, a_log, dt_bias, gdn_norm, batch, seq):
    n, width = gdn_proj.shape
    n_heads = a_log.shape[0]
    dk = width // (4 * n_heads)
    tb = GDN_STEP
    nt = seq // tb
    rows = lambda b, h, t: b * nt + t
    alog_b = jnp.broadcast_to(a_log.reshape(n_heads, 1, 1), (n_heads, 1, LANES))
    dtb_b = jnp.broadcast_to(dt_bias.reshape(n_heads, 1, 1), (n_heads, 1, LANES))
    col_spec = lambda k: pl.BlockSpec((tb, dk), lambda b, h, t: (rows(b, h, t), k * n_heads + h))
    w_spec = lambda k: pl.BlockSpec((GDN_CONV, dk), lambda b, h, t: (0, k * n_heads + h))
    head_vec = pl.BlockSpec((1, 1, LANES), lambda b, h, t: (h, 0, 0))
    return pl.pallas_call(
        _gdn_kernel,
        grid=(batch, n_heads, nt),
        in_specs=[col_spec(0), col_spec(1), col_spec(2), col_spec(3),
                  pl.BlockSpec((tb, LANES), lambda b, h, t: (rows(b, h, t), 0)),
                  w_spec(0), w_spec(1), w_spec(2),
                  head_vec, head_vec,
                  pl.BlockSpec((1, dk), lambda b, h, t: (0, 0))],
        out_specs=pl.BlockSpec((tb, dk), lambda b, h, t: (rows(b, h, t), h)),
        out_shape=jax.ShapeDtypeStruct((n, n_heads * dk), BF16),
        scratch_shapes=[pltpu.VMEM((dk, dk), F32),
                        pltpu.VMEM((3, CONV_PAD, dk), F32),
                        pltpu.VMEM((CONV_PAD + tb, dk), F32)],
        compiler_params=_params(3),
        name="gated_delta_rule",
    )(gdn_proj, gdn_proj, gdn_proj, gdn_proj, ab, conv_w, conv_w, conv_w,
      alog_b, dtb_b, gdn_norm.reshape(1, dk))


def _sb_kernel(q_ref, k_ref, v_ref, m_ref, o_ref, q2_ref, acc_ref, carry_ref):
    i = pl.program_id(2)
    tq = q_ref.shape[0]
    lane = lax.broadcasted_iota(jnp.int32, (tq, LANES), 1)
    q = q_ref[...]
    zero = jnp.zeros_like(q)
    q2_ref[0:tq, :] = jnp.where(lane < SB_DH, q, zero)
    q2_ref[tq:2 * tq, :] = jnp.where(lane >= SB_DH, q, zero)
    q2 = q2_ref[...]
    suffix = m_ref[...]

    def tile(j, diagonal):
        start = pl.multiple_of(j * tq, tq)
        k = k_ref[pl.ds(start, tq), :]
        v = v_ref[pl.ds(start, tq), :]
        y = _dot_nt(q2, k) * LOG2_E
        cost = jnp.maximum(y, 0.0) + jnp.log2(1.0 + jnp.exp2(-jnp.abs(y)))
        if diagonal:
            r = lax.broadcasted_iota(jnp.int32, (2 * tq, tq), 0) & (tq - 1)
            s = lax.broadcasted_iota(jnp.int32, (2 * tq, tq), 1)
            mask = s < r
            masked_cost = jnp.where(mask, cost, 0.0)
            later = _dot_exact_rhs(masked_cost, suffix)
            a = jnp.where(mask, jnp.exp2(y - cost - later), 0.0)
            acc_ref[...] = _dot(a.astype(BF16), v)
            carry_ref[...] = jnp.sum(masked_cost, axis=1, keepdims=True)
        else:
            later = _dot_exact_rhs(cost, suffix)
            a = jnp.exp2(y - cost - later)
            carry = carry_ref[...]
            acc_ref[...] += _dot(a.astype(BF16), v) * jnp.exp2(-carry)
            carry_ref[...] = carry + jnp.sum(cost, axis=1, keepdims=True)

    tile(i, True)

    def cond(state):
        step, least = state
        return jnp.logical_and(step < i, least < SB_UNDERFLOW_LOG2)

    def body(state):
        step, _ = state
        tile(i - 1 - step, False)
        return step + 1, jnp.min(carry_ref[...])

    lax.while_loop(cond, body, (jnp.int32(0), jnp.min(carry_ref[...])))
    acc = acc_ref[...]
    o_ref[...] = jnp.where(lane < SB_DH, acc[0:tq], acc[tq:2 * tq]).astype(o_ref.dtype)


def _sb_attention(qn, kn, vn, suffix, batch, seq):
    n, width = qn.shape
    pairs = width // LANES
    tq = SB_TILE
    nq = seq // tq
    return pl.pallas_call(
        _sb_kernel,
        grid=(batch, pairs, nq),
        in_specs=[pl.BlockSpec((tq, LANES), lambda b, p, i: (b * nq + i, p)),
                  pl.BlockSpec((seq, LANES), lambda b, p, i: (b, p)),
                  pl.BlockSpec((seq, LANES), lambda b, p, i: (b, p)),
                  pl.BlockSpec((tq, tq), lambda b, p, i: (0, 0))],
        out_specs=pl.BlockSpec((tq, LANES), lambda b, p, i: (b * nq + i, p)),
        out_shape=jax.ShapeDtypeStruct((n, width), BF16),
        scratch_shapes=[pltpu.VMEM((2 * tq, LANES), BF16),
                        pltpu.VMEM((2 * tq, LANES), F32),
                        pltpu.VMEM((2 * tq, 1), F32)],
        compiler_params=_params(3),
        name="stick_breaking_attention",
    )(qn, kn, vn, suffix)


def _ret_proj_kernel(x_ref, g_ref, sc_ref, sh_ref, w_ref, qk_ref, v_ref, gate_ref):
    hb = _norm_mod(x_ref[...], g_ref[...], sc_ref[0], sh_ref[0]).astype(BF16)
    step = 512
    off = 0
    for ref in (qk_ref, v_ref, gate_ref):
        for o in range(0, ref.shape[1], step):
            ref[:, o:o + step] = _dot(hb, w_ref[:, off + o:off + o + step]).astype(ref.dtype)
        off += ref.shape[1]


def _ret_proj(x, gain, sc, sh, w, n_qk, n_v, seq):
    n, d = x.shape
    tm = ROW_TILE_PROJ
    per_b = seq // tm
    bmap = lambda i: (i // per_b, 0, 0)
    row = lambda i: (i, 0)
    const = lambda i: (0, 0)
    return pl.pallas_call(
        _ret_proj_kernel,
        grid=(n // tm,),
        in_specs=[pl.BlockSpec((tm, d), row),
                  pl.BlockSpec((1, d), const),
                  pl.BlockSpec((1, 1, d), bmap),
                  pl.BlockSpec((1, 1, d), bmap),
                  _resident(w.shape, const)],
        out_specs=[pl.BlockSpec((tm, n_qk), row),
                   pl.BlockSpec((tm, n_v), row),
                   pl.BlockSpec((tm, n_v), row)],
        out_shape=[jax.ShapeDtypeStruct((n, n_qk), F32),
                   jax.ShapeDtypeStruct((n, n_v), BF16),
                   jax.ShapeDtypeStruct((n, n_v), F32)],
        compiler_params=_params(1),
        name="retention_in_proj",
    )(x, gain, sc, sh, w)


def _ret_kernel(q_ref, k_ref, v_ref, gate_ref, cos_ref, sin_ref, o_ref, state_ref):
    h = pl.program_id(1)
    t = pl.program_id(2)
    c, dk = q_ref.shape
    half = dk // 2

    @pl.when(t == 0)
    def _():
        state_ref[...] = jnp.zeros_like(state_ref)

    cos = cos_ref[...]
    sin = sin_ref[...]

    def rot(x):
        x1, x2 = x[:, :half], x[:, half:]
        return jnp.concatenate([x1 * cos - x2 * sin, x1 * sin + x2 * cos], axis=-1)

    q = rot(q_ref[...])
    k = rot(k_ref[...]) * (dk ** -0.5)
    v = v_ref[...]

    hf = jnp.full((c, c), h, jnp.int32).astype(F32)
    lg = jnp.log1p(-jnp.exp2(-5.0 - hf))
    row = lax.broadcasted_iota(jnp.int32, (c, c), 0)
    col = lax.broadcasted_iota(jnp.int32, (c, c), 1)
    causal = row >= col
    rowf = row.astype(F32)
    diff = rowf - col.astype(F32)
    dmat = jnp.where(causal, jnp.exp(jnp.where(causal, diff, 0.0) * lg), 0.0)
    lg_k = lg[:, :dk] if dk <= c else jnp.broadcast_to(lg[:, :1], (c, dk))
    rowk = lax.broadcasted_iota(jnp.int32, (c, dk), 0).astype(F32)
    q_dec = q * jnp.exp(lg_k * (rowk + 1.0))
    k_dec = k * jnp.exp(lg_k * (c - 1.0 - rowk))
    chunk_decay = jnp.exp(lg[0:1, 0:1] * c)

    qb = q.astype(BF16)
    kb = k.astype(BF16)
    intra = _dot_nt(qb, kb) * dmat
    state = state_ref[...]
    o = _dot(intra.astype(BF16), v) + _dot(q_dec.astype(BF16), state.astype(BF16))
    state_ref[...] = state * chunk_decay + _dot(k_dec.T.astype(BF16), v)
    ms = jnp.mean(o * o, axis=-1, keepdims=True)
    o = o * lax.rsqrt(ms + EPS)
    o_ref[...] = (o * _silu(gate_ref[...])).astype(o_ref.dtype)


def _retention(qk, v, gate, cos, sin, n_heads, batch, seq):
    n = qk.shape[0]
    dk = qk.shape[1] // (2 * n_heads)
    dv = v.shape[1] // n_heads
    c = RET_CHUNK
    nt = seq // c
    rows = lambda b, h, t: b * nt + t
    return pl.pallas_call(
        _ret_kernel,
        grid=(batch, n_heads, nt),
        in_specs=[pl.BlockSpec((c, dk), lambda b, h, t: (rows(b, h, t), h)),
                  pl.BlockSpec((c, dk), lambda b, h, t: (rows(b, h, t), n_heads + h)),
                  pl.BlockSpec((c, dv), lambda b, h, t: (rows(b, h, t), h)),
                  pl.BlockSpec((c, dv), lambda b, h, t: (rows(b, h, t), h)),
                  pl.BlockSpec((c, dk // 2), lambda b, h, t: (t, 0)),
                  pl.BlockSpec((c, dk // 2), lambda b, h, t: (t, 0))],
        out_specs=pl.BlockSpec((c, dv), lambda b, h, t: (rows(b, h, t), h)),
        out_shape=jax.ShapeDtypeStruct((n, n_heads * dv), BF16),
        scratch_shapes=[pltpu.VMEM((dk, dv), F32)],
        compiler_params=_params(3),
        name="retention_chunkwise",
    )(qk, qk, v, gate, cos, sin)


def _post_kernel(*refs, n_mix):
    x_ref = refs[0]
    o_refs = refs[1:1 + n_mix]
    wm_refs = refs[1 + n_mix:1 + 2 * n_mix]
    gtm_ref, g_ref, sc_ref, sh_ref, gtf_ref, win_ref, wout_ref, out_ref, acc_ref = refs[1 + 2 * n_mix:]
    y = _dot(o_refs[0][...], wm_refs[0][...])
    for o_ref, w_ref in zip(o_refs[1:], wm_refs[1:]):
        y = y + _dot(o_ref[...], w_ref[...])
    x1 = x_ref[...] + gtm_ref[0] * y
    hb = _norm_mod(x1, g_ref[...], sc_ref[0], sh_ref[0]).astype(BF16)
    hidden = wout_ref.shape[0]
    for idx, off in enumerate(range(0, hidden, FFN_CHUNK)):
        g = _dot(hb, win_ref[:, off:off + FFN_CHUNK])
        u = _dot(hb, win_ref[:, hidden + off:hidden + off + FFN_CHUNK])
        part = _dot((_silu(g) * u).astype(BF16), wout_ref[off:off + FFN_CHUNK, :])
        if idx == 0:
            acc_ref[...] = part
        else:
            acc_ref[...] += part
    out_ref[...] = x1 + gtf_ref[0] * acc_ref[...]


def _post(x, mix_outs, mix_ws, gt_m, gain, sc, sh, gt_f, w_in, w_out, seq):
    n, d = x.shape
    tm = ROW_TILE_FFN
    per_b = seq // tm
    bmap = lambda i: (i // per_b, 0, 0)
    row = lambda i: (i, 0)
    const = lambda i: (0, 0)
    vec = pl.BlockSpec((1, 1, d), bmap)
    n_mix = len(mix_outs)
    return pl.pallas_call(
        functools.partial(_post_kernel, n_mix=n_mix),
        grid=(n // tm,),
        in_specs=([pl.BlockSpec((tm, d), row)]
                  + [pl.BlockSpec((tm, o.shape[1]), row) for o in mix_outs]
                  + [_resident(w.shape, const) for w in mix_ws]
                  + [vec, pl.BlockSpec((1, d), const), vec, vec, vec,
                     _resident(w_in.shape, const), _resident(w_out.shape, const)]),
        out_specs=pl.BlockSpec((tm, d), row),
        out_shape=jax.ShapeDtypeStruct((n, d), F32),
        scratch_shapes=[pltpu.VMEM((tm, d), F32)],
        compiler_params=_params(1),
        name="out_proj_swiglu",
    )(x, *mix_outs, *mix_ws, gt_m, gain, sc, sh, gt_f, w_in, w_out)


def kernel(x, c, ada_w, ada_b, norm_mix, norm_ffn, hyb_w_in, hyb_conv, gdn_a_log, gdn_dt_bias, gdn_norm, sb_q_norm, sb_k_norm, hyb_w_out, ret_w_in, ret_w_out, ffn_w_in, ffn_w_out):
    batch, seq, d = x.shape
    depth = ada_w.shape[0]
    gdn_heads = gdn_a_log.shape[1]
    gdn_dv = gdn_norm.shape[1]
    gdn_w = gdn_heads * gdn_dv
    sb_dh = sb_q_norm.shape[1]
    assert sb_dh == SB_DH
    sb_w = hyb_w_out.shape[1] - gdn_w
    sb_heads = sb_w // sb_dh
    ret_heads = 4
    ret_qk = ret_w_in.shape[2] // 6
    ret_v = 2 * ret_qk
    ret_dk = ret_qk // ret_heads

    xf = x.reshape(batch * seq, d)
    mod = _modulation(c, ada_w, ada_b)

    head_mean = jnp.kron(jnp.eye(sb_heads, dtype=F32), jnp.full((sb_dh, sb_dh), 1.0 / sb_dh, F32)).astype(BF16)
    idx = jnp.arange(SB_TILE)
    suffix = (idx[:, None] > idx[None, :]).astype(BF16)
    pos = jnp.arange(seq, dtype=F32)
    inv = 1.0 / (ROPE_BASE ** (jnp.arange(0, ret_dk, 2, dtype=F32) / ret_dk))
    ang = pos[:, None] * inv[None, :]
    cos, sin = jnp.cos(ang), jnp.sin(ang)

    for l in range(depth):
        sh_m, sc_m, gt_m, sh_f, sc_f, gt_f = [m.reshape(batch, 1, d) for m in jnp.split(mod[l], 6, axis=-1)]
        i = l // 2
        gain_m = norm_mix[l].reshape(1, d)
        if l % 2 == 0:
            w_in = hyb_w_in[i]
            n_main = 4 * gdn_w
            n_ab = 2 * gdn_heads
            ab_cols = jnp.zeros((d, LANES), F32).at[:, :n_ab].set(w_in[:, n_main:n_main + n_ab])
            w_re = jnp.concatenate([w_in[:, :n_main], ab_cols, w_in[:, n_main + n_ab:]], axis=1).astype(BF16)
            gdn_proj, ab, qn, kn, vn = _hyb_proj(
                xf, gain_m, sc_m, sh_m, w_re, head_mean,
                jnp.tile(sb_q_norm[i], sb_heads).reshape(1, sb_w),
                jnp.tile(sb_k_norm[i], sb_heads).reshape(1, sb_w), seq)
            o_a = _gdn(gdn_proj, ab, hyb_conv[i], gdn_a_log[i], gdn_dt_bias[i], gdn_norm[i], batch, seq)
            o_b = _sb_attention(qn, kn, vn, suffix, batch, seq)
            w_out = hyb_w_out[i].astype(BF16)
            mix_outs = [o_a, o_b]
            mix_ws = [w_out[:gdn_w], w_out[gdn_w:]]
        else:
            qk, v, gate = _ret_proj(xf, gain_m, sc_m, sh_m, ret_w_in[i].astype(BF16), 2 * ret_qk, ret_v, seq)
            o_r = _retention(qk, v, gate, cos, sin, ret_heads, batch, seq)
            mix_outs = [o_r]
            mix_ws = [ret_w_out[i].astype(BF16)]
        xf = _post(xf, mix_outs, mix_ws, gt_m, norm_ffn[l].reshape(1, d), sc_f, sh_f, gt_f,
                   ffn_w_in[l].astype(BF16), ffn_w_out[l].astype(BF16), seq)
    return xf.reshape(batch, seq, d)
```

```python
import functools

import jax
import jax.numpy as jnp
from jax import lax
from jax.experimental import pallas as pl
from jax.experimental.pallas import tpu as pltpu

F32 = jnp.float32
BF16 = jnp.bfloat16
EPS = 1e-6
ROPE_BASE = 10000.0

LANES = 128
MXU_DIM = 256
VMEM_LIMIT = 56 * 1024 * 1024

GDN_CONV = 4
GDN_BLOCK = 128
GDN_STEP = 512
SB_TILE = 256
SB_DH = 64
LOG2_E = 1.4426950408889634
SB_UNDERFLOW_LOG2 = 180.0
RET_CHUNK = 256
ROW_TILE_PROJ = 256
ROW_TILE_FFN = 512
FFN_CHUNK = 256
CONV_PAD = 8


def _dot(a, b):
    return jnp.dot(a, b, preferred_element_type=F32)


def _dot_nt(a, b):
    return lax.dot_general(a, b, (((1,), (1,)), ((), ())), preferred_element_type=F32)


def _split2(a):
    hi = a.astype(BF16)
    lo = (a - hi.astype(F32)).astype(BF16)
    return hi, lo


def _dot3(a, b):
    ah, al = _split2(a)
    bh, bl = _split2(b)
    n = b.shape[1]
    if 2 * n <= MXU_DIM:
        both = _dot(ah, jnp.concatenate([bh, bl], axis=1))
        return (both[:, :n] + both[:, n:]) + _dot(al, bh)
    return _dot(ah, bh) + (_dot(ah, bl) + _dot(al, bh))


def _dot_exact_rhs(a, m):
    ah, al = _split2(a)
    return _dot(ah, m) + _dot(al, m)


def _dot_exact_lhs(m, a):
    ah, al = _split2(a)
    n = a.shape[1]
    if 2 * n <= MXU_DIM:
        both = _dot(m, jnp.concatenate([ah, al], axis=1))
        return both[:, :n] + both[:, n:]
    return _dot(m, ah) + _dot(m, al)


def _silu(x):
    return x * jax.nn.sigmoid(x)


def _softplus(x):
    return jnp.maximum(x, 0.0) + jnp.log1p(jnp.exp(-jnp.abs(x)))


def _params(n_axes):
    return pltpu.CompilerParams(dimension_semantics=("arbitrary",) * n_axes,
                                vmem_limit_bytes=VMEM_LIMIT)


def _resident(shape, index_map):
    return pl.BlockSpec(shape, index_map, pipeline_mode=pl.Buffered(1))


def _mod_kernel(c_ref, w_ref, b_ref, o_ref):
    c = c_ref[...]
    o_ref[0] = _dot(_silu(c).astype(BF16), w_ref[0].astype(BF16)) + b_ref[0]


def _modulation(c, ada_w, ada_b):
    depth, d, n = ada_w.shape
    b = c.shape[0]
    rows = 8
    c_pad = jnp.zeros((rows, d), F32).at[:b].set(c)
    tn = 1024
    out = pl.pallas_call(
        _mod_kernel,
        grid=(depth, n // tn),
        in_specs=[pl.BlockSpec((rows, d), lambda l, j: (0, 0)),
                  pl.BlockSpec((1, d, tn), lambda l, j: (l, 0, j)),
                  pl.BlockSpec((1, 1, tn), lambda l, j: (l, 0, j))],
        out_specs=pl.BlockSpec((1, rows, tn), lambda l, j: (l, 0, j)),
        out_shape=jax.ShapeDtypeStruct((depth, rows, n), F32),
        compiler_params=_params(2),
        name="adaln_modulation",
    )(c_pad, ada_w, ada_b.reshape(depth, 1, n))
    return out[:, :b]


def _norm_mod(x, gain, sc, sh):
    ms = jnp.mean(x * x, axis=-1, keepdims=True)
    y = x * lax.rsqrt(ms + EPS)
    y = y * gain
    return y * (1.0 + sc) + sh


def _hyb_proj_kernel(x_ref, g_ref, sc_ref, sh_ref, w_ref, cw_ref, gm_ref, qg_ref, kg_ref,
                     qkv_ref, gate_ref, ab_ref, q_ref, k_ref, v_ref, ext_ref, *, per_batch, dk):
    i = pl.program_id(0)
    tm = x_ref.shape[0]
    hb = _norm_mod(x_ref[...], g_ref[...], sc_ref[0], sh_ref[0]).astype(BF16)
    n_conv = qkv_ref.shape[1]
    n_gate = gate_ref.shape[1]
    w_sb = q_ref.shape[1]

    @pl.when(i % per_batch == 0)
    def _():
        ext_ref[0:CONV_PAD, :] = jnp.zeros((CONV_PAD, n_conv), F32)

    def l2n(y):
        return y * lax.rsqrt(jnp.sum(y * y, axis=-1, keepdims=True) + EPS)

    for c0 in range(0, n_conv, MXU_DIM):
        ext_ref[CONV_PAD:CONV_PAD + tm, c0:c0 + MXU_DIM] = _dot(hb, w_ref[:, c0:c0 + MXU_DIM])

    def conv_chunk(c0):
        cols = slice(c0, c0 + MXU_DIM)
        w = cw_ref[:, cols]
        y = ext_ref[CONV_PAD:CONV_PAD + tm, cols] * w[GDN_CONV - 1:GDN_CONV, :]
        for j in range(GDN_CONV - 1):
            shift = GDN_CONV - 1 - j
            y = y + ext_ref[CONV_PAD - shift:CONV_PAD - shift + tm, cols] * w[j:j + 1, :]
        y = _silu(y)
        if c0 < 2 * n_gate:
            y = jnp.concatenate([l2n(y[:, h0:h0 + dk]) for h0 in range(0, MXU_DIM, dk)], axis=1)
            if c0 < n_gate:
                y = y * (dk ** -0.5)
        qkv_ref[:, cols] = y

    def head_rms(xf, gain):
        ms = _dot_exact_rhs(xf * xf, gm_ref[...])
        return xf * lax.rsqrt(ms + EPS) * gain

    off_gate = n_conv
    off_ab = off_gate + n_gate
    off_q = off_ab + LANES
    off_k = off_q + w_sb
    off_v = off_k + w_sb
    conv_starts = list(range(0, n_conv, MXU_DIM))
    gate_ref[...] = _dot(hb, w_ref[:, off_gate:off_gate + n_gate])
    for c0 in conv_starts[0:2]:
        conv_chunk(c0)
    sq = _dot(hb, w_ref[:, off_q:off_q + w_sb])
    q_ref[...] = (head_rms(sq, qg_ref[...]) * (SB_DH ** -0.5)).astype(BF16)
    for c0 in conv_starts[2:4]:
        conv_chunk(c0)
    sk = _dot(hb, w_ref[:, off_k:off_k + w_sb])
    k_ref[...] = head_rms(sk, kg_ref[...]).astype(BF16)
    for c0 in conv_starts[4:]:
        conv_chunk(c0)
    v_ref[...] = _dot(hb, w_ref[:, off_v:off_v + w_sb]).astype(BF16)
    ab_ref[...] = _dot(hb, w_ref[:, off_ab:off_ab + LANES])
    ext_ref[0:CONV_PAD, :] = ext_ref[tm:tm + CONV_PAD, :]


def _hyb_proj(x, gain, sc, sh, w, conv_w, gmat, qg, kg, seq, gdn_dk):
    n, d = x.shape
    tm = ROW_TILE_PROJ
    per_b = seq // tm
    w_sb = qg.shape[1]
    n_conv = conv_w.shape[1]
    n_gate = w.shape[1] - n_conv - LANES - 3 * w_sb
    bmap = lambda i: (i // per_b, 0, 0)
    row = lambda i: (i, 0)
    const = lambda i: (0, 0)
    return pl.pallas_call(
        functools.partial(_hyb_proj_kernel, per_batch=per_b, dk=gdn_dk),
        grid=(n // tm,),
        in_specs=[pl.BlockSpec((tm, d), row),
                  pl.BlockSpec((1, d), const),
                  pl.BlockSpec((1, 1, d), bmap),
                  pl.BlockSpec((1, 1, d), bmap),
                  _resident(w.shape, const),
                  pl.BlockSpec(conv_w.shape, const),
                  _resident(gmat.shape, const),
                  pl.BlockSpec((1, w_sb), const),
                  pl.BlockSpec((1, w_sb), const)],
        out_specs=[pl.BlockSpec((tm, n_conv), row),
                   pl.BlockSpec((tm, n_gate), row),
                   pl.BlockSpec((tm, LANES), row),
                   pl.BlockSpec((tm, w_sb), row),
                   pl.BlockSpec((tm, w_sb), row),
                   pl.BlockSpec((tm, w_sb), row)],
        out_shape=[jax.ShapeDtypeStruct((n, n_conv), F32),
                   jax.ShapeDtypeStruct((n, n_gate), F32),
                   jax.ShapeDtypeStruct((n, LANES), F32),
                   jax.ShapeDtypeStruct((n, w_sb), BF16),
                   jax.ShapeDtypeStruct((n, w_sb), BF16),
                   jax.ShapeDtypeStruct((n, w_sb), BF16)],
        scratch_shapes=[pltpu.VMEM((CONV_PAD + tm, n_conv), F32)],
        compiler_params=_params(1),
        name="hybrid_in_proj",
    )(x, gain, sc, sh, w, conv_w, gmat, qg, kg)


def _dot3s(a, b):
    ah, al = a
    bh, bl = b
    n = bh.shape[1]
    both = _dot(ah, jnp.concatenate([bh, bl], axis=1))
    return (both[:, :n] + both[:, n:]) + _dot(al, bh)


def _lower_child_rows(x, size):
    return jnp.concatenate([x[r + size:r + 2 * size] for r in range(0, x.shape[0], 2 * size)], axis=0)


def _unit_lower_inverse_all(lowers, eye, row, col):
    bits = 3
    same = (row >> bits) == (col >> bits)
    diags = [jnp.where(same, l, 0.0) for l in lowers]
    d_s = [_split2(d) for d in diags]
    x0_s = [_split2(eye - d) for d in diags]
    p2_s = [_split2(_dot3s(d, d)) for d in d_s]
    xs = [(eye - d) + _dot3s(x, p) for d, x, p in zip(diags, x0_s, p2_s)]
    p4_s = [_split2(_dot3s(p, p)) for p in p2_s]
    xs = [x + _dot3s(_split2(x), p) for x, p in zip(xs, p4_s)]
    n = lowers[0].shape[0]
    size = 1 << bits
    while size < n:
        parent = (row >> (bits + 1)) == (col >> (bits + 1))
        child = (row >> bits) == (col >> bits)
        off_s = [_split2(jnp.where(parent, jnp.where(child, 0.0, l), 0.0)) for l in lowers]
        x_s = [_split2(x) for x in xs]
        ys = [_dot3s(_split2(_lower_child_rows(x, size)), o) for x, o in zip(xs, off_s)]
        zs = [_dot3s(_split2(y), x) for y, x in zip(ys, x_s)]
        new_xs = []
        for x, z in zip(xs, zs):
            pieces = []
            for k, r in enumerate(range(0, n, 2 * size)):
                pieces.append(x[r:r + size])
                pieces.append(x[r + size:r + 2 * size] - z[k * size:(k + 1) * size])
            new_xs.append(jnp.concatenate(pieces, axis=0))
        xs = new_xs
        bits += 1
        size *= 2
    return xs


def _gdn_kernel(q_ref, k_ref, v_ref, gate_ref, ab_ref, alog_ref, dtb_ref, gn_ref, o_ref, state_ref):
    t = pl.program_id(1)
    tb, width = q_ref.shape
    c = GDN_BLOCK
    n_heads = state_ref.shape[0]
    dk = width // n_heads
    n_chunks = tb // c

    @pl.when(t == 0)
    def _():
        state_ref[...] = jnp.zeros_like(state_ref)

    lane = lax.broadcasted_iota(jnp.int32, (tb, LANES), 1)
    ab = ab_ref[...]
    g_lanes = -jnp.exp(alog_ref[...]) * _softplus(ab + dtb_ref[...])
    beta_lanes = jax.nn.sigmoid(ab)

    def lane_column(x, idx):
        return jnp.broadcast_to(jnp.sum(jnp.where(lane == idx, x, 0.0), axis=1, keepdims=True), (tb, LANES))

    row = lax.broadcasted_iota(jnp.int32, (c, c), 0)
    col = lax.broadcasted_iota(jnp.int32, (c, c), 1)
    incl = row >= col
    strict = row > col
    eye = jnp.where(row == col, 1.0, 0.0).astype(F32)
    tri = jnp.where(incl, 1.0, 0.0).astype(BF16)

    items = []
    for h in range(n_heads):
        g_h = lane_column(g_lanes, h)
        beta_h = lane_column(beta_lanes, h + n_heads)
        cols = slice(h * dk, (h + 1) * dk)
        for n in range(n_chunks):
            rows = slice(n * c, (n + 1) * c)
            items.append(dict(q=q_ref[rows, cols], k=k_ref[rows, cols], v=v_ref[rows, cols],
                              beta=beta_h[rows], g=g_h[rows]))

    def prepare(group):
        for it in group:
            it["gcol"] = _dot_exact_lhs(tri, it["g"])
        for it in group:
            gcol = it["gcol"]
            it["dec"] = jnp.exp(jnp.where(incl, gcol - gcol.T, 0.0))
            it["eg"] = jnp.exp(gcol)
            it["g_last"] = gcol[c - 1:c, :]
            it["kb"] = it["k"] * it["beta"]
            it["kk"] = it["k"].astype(BF16)
        for it in group:
            it["lower"] = jnp.where(strict, _dot_nt(it["kb"].astype(BF16), it["kk"]) * it["dec"], 0.0)
        tinvs = _unit_lower_inverse_all([it["lower"] for it in group], eye, row, col)
        for it, tinv in zip(group, tinvs):
            rhs = jnp.concatenate([it["v"] * it["beta"], it["kb"] * it["eg"]], axis=1)
            uw = _dot3(tinv, rhs)
            it["u"] = uw[:, :dk]
            a = jnp.where(incl, _dot_nt(it["q"].astype(BF16), it["kk"]) * it["dec"], 0.0)
            q_dec = it["q"] * it["eg"]
            k_dec = it["k"] * jnp.exp(it["g_last"] - it["gcol"])
            it["wq"] = jnp.concatenate([uw[:, dk:], q_dec], axis=0).astype(BF16)
            it["ak"] = jnp.concatenate([a, k_dec.T], axis=0).astype(BF16)
            it["decay"] = jnp.exp(it["g_last"])

    early = n_chunks // 2
    prepare([it for i, it in enumerate(items) if i % n_chunks < early])
    prepare([it for i, it in enumerate(items) if i % n_chunks >= early])

    gn = gn_ref[...]
    states = [state_ref[h] for h in range(n_heads)]
    for n in range(n_chunks):
        rows = slice(n * c, (n + 1) * c)
        for h in range(n_heads):
            it = items[h * n_chunks + n]
            cols = slice(h * dk, (h + 1) * dk)
            ws_qs = _dot(it["wq"], states[h].astype(BF16))
            v_new = it["u"] - ws_qs[:c]
            av_kv = _dot(it["ak"], v_new.astype(BF16))
            o = ws_qs[c:] + av_kv[:c]
            states[h] = states[h] * it["decay"] + av_kv[c:]
            ms = jnp.mean(o * o, axis=-1, keepdims=True)
            o = o * lax.rsqrt(ms + EPS) * gn
            o_ref[rows, cols] = (o * _silu(gate_ref[rows, cols])).astype(o_ref.dtype)
    for h in range(n_heads):
        state_ref[h] = states[h]


def _gdn(qkv, gate, ab, a_log, dt_bias, gdn_norm, batch, seq):
    n, width = gate.shape
    n_heads = a_log.shape[0]
    dk = width // n_heads
    tb = GDN_STEP
    nt = seq // tb
    rows = lambda b, t: b * nt + t
    pad = lambda x: jnp.zeros((1, LANES), F32).at[0, :n_heads].set(x)
    col_spec = lambda k: pl.BlockSpec((tb, width), lambda b, t: (rows(b, t), k))
    const = lambda b, t: (0, 0)
    return pl.pallas_call(
        _gdn_kernel,
        grid=(batch, nt),
        in_specs=[col_spec(0), col_spec(1), col_spec(2), col_spec(0),
                  pl.BlockSpec((tb, LANES), lambda b, t: (rows(b, t), 0)),
                  pl.BlockSpec((1, LANES), const), pl.BlockSpec((1, LANES), const),
                  pl.BlockSpec((1, dk), const)],
        out_specs=pl.BlockSpec((tb, width), lambda b, t: (rows(b, t), 0)),
        out_shape=jax.ShapeDtypeStruct((n, width), BF16),
        scratch_shapes=[pltpu.VMEM((n_heads, dk, dk), F32)],
        compiler_params=_params(2),
        name="gated_delta_rule",
    )(qkv, qkv, qkv, gate, ab, pad(a_log), pad(dt_bias), gdn_norm.reshape(1, dk))


def _sb_kernel(q_ref, k_ref, v_ref, m_ref, o_ref, q2_ref, acc_ref, carry_ref):
    i = pl.program_id(2)
    tq = q_ref.shape[0]
    lane = lax.broadcasted_iota(jnp.int32, (tq, LANES), 1)
    q = q_ref[...]
    zero = jnp.zeros_like(q)
    q2_ref[0:tq, :] = jnp.where(lane < SB_DH, q, zero)
    q2_ref[tq:2 * tq, :] = jnp.where(lane >= SB_DH, q, zero)
    q2 = q2_ref[...]
    suffix = m_ref[...]

    def tile(j, diagonal):
        start = pl.multiple_of(j * tq, tq)
        k = k_ref[pl.ds(start, tq), :]
        v = v_ref[pl.ds(start, tq), :]
        y = _dot_nt(q2, k) * LOG2_E
        cost = jnp.maximum(y, 0.0) + jnp.log2(1.0 + jnp.exp2(-jnp.abs(y)))
        if diagonal:
            r = lax.broadcasted_iota(jnp.int32, (2 * tq, tq), 0) & (tq - 1)
            s = lax.broadcasted_iota(jnp.int32, (2 * tq, tq), 1)
            mask = s < r
            masked_cost = jnp.where(mask, cost, 0.0)
            later = _dot_exact_rhs(masked_cost, suffix)
            a = jnp.where(mask, jnp.exp2(y - cost - later), 0.0)
            acc_ref[...] = _dot(a.astype(BF16), v)
            carry_ref[...] = jnp.sum(masked_cost, axis=1, keepdims=True)
        else:
            later = _dot_exact_rhs(cost, suffix)
            a = jnp.exp2(y - cost - later)
            carry = carry_ref[...]
            acc_ref[...] += _dot(a.astype(BF16), v) * jnp.exp2(-carry)
            carry_ref[...] = carry + jnp.sum(cost, axis=1, keepdims=True)

    tile(i, True)

    def cond(state):
        step, least = state
        return jnp.logical_and(step < i, least < SB_UNDERFLOW_LOG2)

    def body(state):
        step, _ = state
        tile(i - 1 - step, False)
        return step + 1, jnp.min(carry_ref[...])

    lax.while_loop(cond, body, (jnp.int32(0), jnp.min(carry_ref[...])))
    acc = acc_ref[...]
    o_ref[...] = jnp.where(lane < SB_DH, acc[0:tq], acc[tq:2 * tq]).astype(o_ref.dtype)


def _sb_attention(qn, kn, vn, suffix, batch, seq):
    n, width = qn.shape
    pairs = width // LANES
    tq = SB_TILE
    nq = seq // tq
    return pl.pallas_call(
        _sb_kernel,
        grid=(batch, pairs, nq),
        in_specs=[pl.BlockSpec((tq, LANES), lambda b, p, i: (b * nq + i, p)),
                  pl.BlockSpec((seq, LANES), lambda b, p, i: (b, p)),
                  pl.BlockSpec((seq, LANES), lambda b, p, i: (b, p)),
                  pl.BlockSpec((tq, tq), lambda b, p, i: (0, 0))],
        out_specs=pl.BlockSpec((tq, LANES), lambda b, p, i: (b * nq + i, p)),
        out_shape=jax.ShapeDtypeStruct((n, width), BF16),
        scratch_shapes=[pltpu.VMEM((2 * tq, LANES), BF16),
                        pltpu.VMEM((2 * tq, LANES), F32),
                        pltpu.VMEM((2 * tq, 1), F32)],
        compiler_params=_params(3),
        name="stick_breaking_attention",
    )(qn, kn, vn, suffix)


def _ret_proj_kernel(x_ref, g_ref, sc_ref, sh_ref, w_ref, qk_ref, v_ref, gate_ref):
    hb = _norm_mod(x_ref[...], g_ref[...], sc_ref[0], sh_ref[0]).astype(BF16)
    step = 512
    off = 0
    for ref in (qk_ref, v_ref, gate_ref):
        for o in range(0, ref.shape[1], step):
            ref[:, o:o + step] = _dot(hb, w_ref[:, off + o:off + o + step]).astype(ref.dtype)
        off += ref.shape[1]


def _ret_proj(x, gain, sc, sh, w, n_qk, n_v, seq):
    n, d = x.shape
    tm = ROW_TILE_PROJ
    per_b = seq // tm
    bmap = lambda i: (i // per_b, 0, 0)
    row = lambda i: (i, 0)
    const = lambda i: (0, 0)
    return pl.pallas_call(
        _ret_proj_kernel,
        grid=(n // tm,),
        in_specs=[pl.BlockSpec((tm, d), row),
                  pl.BlockSpec((1, d), const),
                  pl.BlockSpec((1, 1, d), bmap),
                  pl.BlockSpec((1, 1, d), bmap),
                  _resident(w.shape, const)],
        out_specs=[pl.BlockSpec((tm, n_qk), row),
                   pl.BlockSpec((tm, n_v), row),
                   pl.BlockSpec((tm, n_v), row)],
        out_shape=[jax.ShapeDtypeStruct((n, n_qk), F32),
                   jax.ShapeDtypeStruct((n, n_v), BF16),
                   jax.ShapeDtypeStruct((n, n_v), F32)],
        compiler_params=_params(1),
        name="retention_in_proj",
    )(x, gain, sc, sh, w)


def _ret_kernel(q_ref, k_ref, v_ref, gate_ref, cos_ref, sin_ref, qs_ref, ks_ref, o_ref, state_ref):
    t = pl.program_id(1)
    c = q_ref.shape[0]
    n_heads = state_ref.shape[0]
    dk = q_ref.shape[1] // n_heads
    dv = v_ref.shape[1] // n_heads
    half = dk // 2

    @pl.when(t == 0)
    def _():
        state_ref[...] = jnp.zeros_like(state_ref)

    cos = cos_ref[...]
    sin = sin_ref[...]
    row = lax.broadcasted_iota(jnp.int32, (c, c), 0)
    col = lax.broadcasted_iota(jnp.int32, (c, c), 1)
    causal = row >= col

    def rot(x, scale):
        x1, x2 = x[:, :half], x[:, half:]
        return jnp.concatenate([(x1 * cos - x2 * sin) * scale, (x1 * sin + x2 * cos) * scale], axis=-1)

    for h in range(n_heads):
        q = rot(q_ref[:, h * dk:(h + 1) * dk], qs_ref[h]).astype(BF16)
        k = rot(k_ref[:, h * dk:(h + 1) * dk], ks_ref[h])
        v = v_ref[:, h * dv:(h + 1) * dv]
        intra = jnp.where(causal, _dot_nt(q, k.astype(BF16)), 0.0)
        state = state_ref[h]
        o = _dot(intra.astype(BF16), v) + _dot(q, state.astype(BF16))
        chunk_decay = qs_ref[h, c - 1:c, 0:1]
        state_ref[h] = (state + _dot(k.T.astype(BF16), v)) * chunk_decay
        ms = jnp.mean(o * o, axis=-1, keepdims=True)
        o = o * lax.rsqrt(ms + EPS)
        o_ref[:, h * dv:(h + 1) * dv] = (o * _silu(gate_ref[:, h * dv:(h + 1) * dv])).astype(o_ref.dtype)


def _retention(qk, v, gate, cos, sin, q_scale, k_scale, batch, seq):
    n = qk.shape[0]
    n_heads = q_scale.shape[0]
    w_qk = qk.shape[1] // 2
    w_v = v.shape[1]
    c = RET_CHUNK
    nt = seq // c
    rows = lambda b, t: b * nt + t
    const3 = lambda b, t: (0, 0, 0)
    return pl.pallas_call(
        _ret_kernel,
        grid=(batch, nt),
        in_specs=[pl.BlockSpec((c, w_qk), lambda b, t: (rows(b, t), 0)),
                  pl.BlockSpec((c, w_qk), lambda b, t: (rows(b, t), 1)),
                  pl.BlockSpec((c, w_v), lambda b, t: (rows(b, t), 0)),
                  pl.BlockSpec((c, w_v), lambda b, t: (rows(b, t), 0)),
                  pl.BlockSpec((c, cos.shape[1]), lambda b, t: (t, 0)),
                  pl.BlockSpec((c, cos.shape[1]), lambda b, t: (t, 0)),
                  pl.BlockSpec(q_scale.shape, const3),
                  pl.BlockSpec(k_scale.shape, const3)],
        out_specs=pl.BlockSpec((c, w_v), lambda b, t: (rows(b, t), 0)),
        out_shape=jax.ShapeDtypeStruct((n, w_v), BF16),
        scratch_shapes=[pltpu.VMEM((n_heads, w_qk // n_heads, w_v // n_heads), F32)],
        compiler_params=_params(2),
        name="retention_chunkwise",
    )(qk, qk, v, gate, cos, sin, q_scale, k_scale)


def _post_kernel(*refs, n_mix):
    x_ref = refs[0]
    o_refs = refs[1:1 + n_mix]
    wm_refs = refs[1 + n_mix:1 + 2 * n_mix]
    gtm_ref, g_ref, sc_ref, sh_ref, gtf_ref, win_ref, wout_ref, out_ref, acc_ref = refs[1 + 2 * n_mix:]
    y = _dot(o_refs[0][...], wm_refs[0][...])
    for o_ref, w_ref in zip(o_refs[1:], wm_refs[1:]):
        y = y + _dot(o_ref[...], w_ref[...])
    x1 = x_ref[...] + gtm_ref[0] * y
    hb = _norm_mod(x1, g_ref[...], sc_ref[0], sh_ref[0]).astype(BF16)
    hidden = wout_ref.shape[0]
    for idx, off in enumerate(range(0, hidden, FFN_CHUNK)):
        g = _dot(hb, win_ref[:, off:off + FFN_CHUNK])
        u = _dot(hb, win_ref[:, hidden + off:hidden + off + FFN_CHUNK])
        part = _dot((_silu(g) * u).astype(BF16), wout_ref[off:off + FFN_CHUNK, :])
        if idx == 0:
            acc_ref[...] = part
        else:
            acc_ref[...] += part
    out_ref[...] = x1 + gtf_ref[0] * acc_ref[...]


def _post(x, mix_outs, mix_ws, gt_m, gain, sc, sh, gt_f, w_in, w_out, seq):
    n, d = x.shape
    tm = ROW_TILE_FFN
    per_b = seq // tm
    bmap = lambda i: (i // per_b, 0, 0)
    row = lambda i: (i, 0)
    const = lambda i: (0, 0)
    vec = pl.BlockSpec((1, 1, d), bmap)
    n_mix = len(mix_outs)
    return pl.pallas_call(
        functools.partial(_post_kernel, n_mix=n_mix),
        grid=(n // tm,),
        in_specs=([pl.BlockSpec((tm, d), row)]
                  + [pl.BlockSpec((tm, o.shape[1]), row) for o in mix_outs]
                  + [_resident(w.shape, const) for w in mix_ws]
                  + [vec, pl.BlockSpec((1, d), const), vec, vec, vec,
                     _resident(w_in.shape, const), _resident(w_out.shape, const)]),
        out_specs=pl.BlockSpec((tm, d), row),
        out_shape=jax.ShapeDtypeStruct((n, d), F32),
        scratch_shapes=[pltpu.VMEM((tm, d), F32)],
        compiler_params=_params(1),
        name="out_proj_swiglu",
    )(x, *mix_outs, *mix_ws, gt_m, gain, sc, sh, gt_f, w_in, w_out)


def kernel(x, c, ada_w, ada_b, norm_mix, norm_ffn, hyb_w_in, hyb_conv, gdn_a_log, gdn_dt_bias, gdn_norm, sb_q_norm, sb_k_norm, hyb_w_out, ret_w_in, ret_w_out, ffn_w_in, ffn_w_out):
    batch, seq, d = x.shape
    depth = ada_w.shape[0]
    gdn_heads = gdn_a_log.shape[1]
    gdn_dv = gdn_norm.shape[1]
    gdn_w = gdn_heads * gdn_dv
    sb_dh = sb_q_norm.shape[1]
    assert sb_dh == SB_DH
    sb_w = hyb_w_out.shape[1] - gdn_w
    sb_heads = sb_w // sb_dh
    ret_heads = 4
    ret_qk = ret_w_in.shape[2] // 6
    ret_v = 2 * ret_qk
    ret_dk = ret_qk // ret_heads

    xf = x.reshape(batch * seq, d)
    mod = _modulation(c, ada_w, ada_b)

    head_mean = jnp.kron(jnp.eye(sb_heads, dtype=F32), jnp.full((sb_dh, sb_dh), 1.0 / sb_dh, F32)).astype(BF16)
    idx = jnp.arange(SB_TILE)
    suffix = (idx[:, None] > idx[None, :]).astype(BF16)
    pos = jnp.arange(seq, dtype=F32)
    inv = 1.0 / (ROPE_BASE ** (jnp.arange(0, ret_dk, 2, dtype=F32) / ret_dk))
    ang = pos[:, None] * inv[None, :]
    cos, sin = jnp.cos(ang), jnp.sin(ang)
    log_gamma = jnp.log1p(-jnp.exp2(-5.0 - jnp.arange(ret_heads, dtype=F32)))
    steps = jnp.arange(1, RET_CHUNK + 1, dtype=F32)
    ret_q_scale = jnp.broadcast_to(jnp.exp(log_gamma[:, None] * steps[None, :])[:, :, None],
                                   (ret_heads, RET_CHUNK, ret_dk // 2))
    ret_k_scale = jnp.broadcast_to((jnp.exp(-log_gamma[:, None] * steps[None, :]) * ret_dk ** -0.5)[:, :, None],
                                   (ret_heads, RET_CHUNK, ret_dk // 2))

    for l in range(depth):
        sh_m, sc_m, gt_m, sh_f, sc_f, gt_f = [m.reshape(batch, 1, d) for m in jnp.split(mod[l], 6, axis=-1)]
        i = l // 2
        gain_m = norm_mix[l].reshape(1, d)
        if l % 2 == 0:
            w_in = hyb_w_in[i]
            n_main = 4 * gdn_w
            n_ab = 2 * gdn_heads
            ab_cols = jnp.zeros((d, LANES), F32).at[:, :n_ab].set(w_in[:, n_main:n_main + n_ab])
            w_re = jnp.concatenate([w_in[:, :n_main], ab_cols, w_in[:, n_main + n_ab:]], axis=1).astype(BF16)
            gdn_qkv, gdn_gate, ab, qn, kn, vn = _hyb_proj(
                xf, gain_m, sc_m, sh_m, w_re, hyb_conv[i], head_mean,
                jnp.tile(sb_q_norm[i], sb_heads).reshape(1, sb_w),
                jnp.tile(sb_k_norm[i], sb_heads).reshape(1, sb_w), seq, gdn_dv)
            o_a = _gdn(gdn_qkv, gdn_gate, ab, gdn_a_log[i], gdn_dt_bias[i], gdn_norm[i], batch, seq)
            o_b = _sb_attention(qn, kn, vn, suffix, batch, seq)
            w_out = hyb_w_out[i].astype(BF16)
            mix_outs = [o_a, o_b]
            mix_ws = [w_out[:gdn_w], w_out[gdn_w:]]
        else:
            qk, v, gate = _ret_proj(xf, gain_m, sc_m, sh_m, ret_w_in[i].astype(BF16), 2 * ret_qk, ret_v, seq)
            o_r = _retention(qk, v, gate, cos, sin, ret_q_scale, ret_k_scale, batch, seq)
            mix_outs = [o_r]
            mix_ws = [ret_w_out[i].astype(BF16)]
        xf = _post(xf, mix_outs, mix_ws, gt_m, norm_ffn[l].reshape(1, d), sc_f, sh_f, gt_f,
                   ffn_w_in[l].astype(BF16), ffn_w_out[l].astype(BF16), seq)
    return xf.reshape(batch, seq, d)
```

```python
import functools

import jax
import jax.numpy as jnp
from jax import lax
from jax.experimental import pallas as pl
from jax.experimental.pallas import tpu as pltpu

F32 = jnp.float32
BF16 = jnp.bfloat16
EPS = 1e-6
ROPE_BASE = 10000.0

LANES = 128
MXU_DIM = 256
VMEM_LIMIT = 56 * 1024 * 1024

GDN_CONV = 4
GDN_BLOCK = 128
GDN_STEP = 512
SB_TILE = 256
SB_DH = 64
LOG2_E = 1.4426950408889634
SB_UNDERFLOW_LOG2 = 180.0
RET_CHUNK = 256
ROW_TILE_PROJ = 256
ROW_TILE_FFN = 512
FFN_CHUNK = 256
CONV_PAD = 8


def _dot(a, b):
    return jnp.dot(a, b, preferred_element_type=F32)


def _dot_nt(a, b):
    return lax.dot_general(a, b, (((1,), (1,)), ((), ())), preferred_element_type=F32)


def _split2(a):
    hi = a.astype(BF16)
    lo = (a - hi.astype(F32)).astype(BF16)
    return hi, lo


def _dot3(a, b):
    ah, al = _split2(a)
    bh, bl = _split2(b)
    n = b.shape[1]
    if 2 * n <= MXU_DIM:
        both = _dot(ah, jnp.concatenate([bh, bl], axis=1))
        return (both[:, :n] + both[:, n:]) + _dot(al, bh)
    return _dot(ah, bh) + (_dot(ah, bl) + _dot(al, bh))


def _dot_exact_rhs(a, m):
    ah, al = _split2(a)
    return _dot(ah, m) + _dot(al, m)


def _dot_exact_lhs(m, a):
    ah, al = _split2(a)
    n = a.shape[1]
    if 2 * n <= MXU_DIM:
        both = _dot(m, jnp.concatenate([ah, al], axis=1))
        return both[:, :n] + both[:, n:]
    return _dot(m, ah) + _dot(m, al)


def _silu(x):
    return x * jax.nn.sigmoid(x)


def _softplus(x):
    return jnp.maximum(x, 0.0) + jnp.log1p(jnp.exp(-jnp.abs(x)))


def _params(n_axes):
    return pltpu.CompilerParams(dimension_semantics=("arbitrary",) * n_axes,
                                vmem_limit_bytes=VMEM_LIMIT)


def _resident(shape, index_map):
    return pl.BlockSpec(shape, index_map, pipeline_mode=pl.Buffered(1))


def _mod_kernel(c_ref, w_ref, b_ref, o_ref):
    c = c_ref[...]
    o_ref[0] = _dot(_silu(c).astype(BF16), w_ref[0].astype(BF16)) + b_ref[0]


def _modulation(c, ada_w, ada_b):
    depth, d, n = ada_w.shape
    b = c.shape[0]
    rows = 8
    c_pad = jnp.zeros((rows, d), F32).at[:b].set(c)
    tn = 1024
    out = pl.pallas_call(
        _mod_kernel,
        grid=(depth, n // tn),
        in_specs=[pl.BlockSpec((rows, d), lambda l, j: (0, 0)),
                  pl.BlockSpec((1, d, tn), lambda l, j: (l, 0, j)),
                  pl.BlockSpec((1, 1, tn), lambda l, j: (l, 0, j))],
        out_specs=pl.BlockSpec((1, rows, tn), lambda l, j: (l, 0, j)),
        out_shape=jax.ShapeDtypeStruct((depth, rows, n), F32),
        compiler_params=_params(2),
        name="adaln_modulation",
    )(c_pad, ada_w, ada_b.reshape(depth, 1, n))
    return out[:, :b]


def _norm_mod(x, gain, sc, sh):
    ms = jnp.mean(x * x, axis=-1, keepdims=True)
    y = x * lax.rsqrt(ms + EPS)
    y = y * gain
    return y * (1.0 + sc) + sh


def _hyb_proj_kernel(x_ref, g_ref, sc_ref, sh_ref, w_ref, cw_ref, gm_ref, qg_ref, kg_ref,
                     qkv_ref, gate_ref, ab_ref, q_ref, k_ref, v_ref, ext_ref, *, per_batch, dk):
    i = pl.program_id(0)
    tm = x_ref.shape[0]
    hb = _norm_mod(x_ref[...], g_ref[...], sc_ref[0], sh_ref[0]).astype(BF16)
    n_conv = qkv_ref.shape[1]
    n_gate = gate_ref.shape[1]
    w_sb = q_ref.shape[1]

    @pl.when(i % per_batch == 0)
    def _():
        ext_ref[0:CONV_PAD, :] = jnp.zeros((CONV_PAD, n_conv), F32)

    def l2n(y):
        return y * lax.rsqrt(jnp.sum(y * y, axis=-1, keepdims=True) + EPS)

    for c0 in range(0, n_conv, MXU_DIM):
        ext_ref[CONV_PAD:CONV_PAD + tm, c0:c0 + MXU_DIM] = _dot(hb, w_ref[:, c0:c0 + MXU_DIM])

    def conv_chunk(c0):
        cols = slice(c0, c0 + MXU_DIM)
        w = cw_ref[:, cols]
        y = ext_ref[CONV_PAD:CONV_PAD + tm, cols] * w[GDN_CONV - 1:GDN_CONV, :]
        for j in range(GDN_CONV - 1):
            shift = GDN_CONV - 1 - j
            y = y + ext_ref[CONV_PAD - shift:CONV_PAD - shift + tm, cols] * w[j:j + 1, :]
        y = _silu(y)
        if c0 < 2 * n_gate:
            y = jnp.concatenate([l2n(y[:, h0:h0 + dk]) for h0 in range(0, MXU_DIM, dk)], axis=1)
            if c0 < n_gate:
                y = y * (dk ** -0.5)
        qkv_ref[:, cols] = y

    def head_rms(xf, gain):
        ms = _dot_exact_rhs(xf * xf, gm_ref[...])
        return xf * lax.rsqrt(ms + EPS) * gain

    off_gate = n_conv
    off_ab = off_gate + n_gate
    off_q = off_ab + LANES
    off_k = off_q + w_sb
    off_v = off_k + w_sb
    conv_starts = list(range(0, n_conv, MXU_DIM))
    gate_ref[...] = _dot(hb, w_ref[:, off_gate:off_gate + n_gate])
    for c0 in conv_starts[0:2]:
        conv_chunk(c0)
    sq = _dot(hb, w_ref[:, off_q:off_q + w_sb])
    q_ref[...] = (head_rms(sq, qg_ref[...]) * (SB_DH ** -0.5)).astype(BF16)
    for c0 in conv_starts[2:4]:
        conv_chunk(c0)
    sk = _dot(hb, w_ref[:, off_k:off_k + w_sb])
    k_ref[...] = head_rms(sk, kg_ref[...]).astype(BF16)
    for c0 in conv_starts[4:]:
        conv_chunk(c0)
    v_ref[...] = _dot(hb, w_ref[:, off_v:off_v + w_sb]).astype(BF16)
    ab_ref[...] = _dot(hb, w_ref[:, off_ab:off_ab + LANES])
    ext_ref[0:CONV_PAD, :] = ext_ref[tm:tm + CONV_PAD, :]


def _hyb_proj(x, gain, sc, sh, w, conv_w, gmat, qg, kg, seq, gdn_dk):
    n, d = x.shape
    tm = ROW_TILE_PROJ
    per_b = seq // tm
    w_sb = qg.shape[1]
    n_conv = conv_w.shape[1]
    n_gate = w.shape[1] - n_conv - LANES - 3 * w_sb
    bmap = lambda i: (i // per_b, 0, 0)
    row = lambda i: (i, 0)
    const = lambda i: (0, 0)
    return pl.pallas_call(
        functools.partial(_hyb_proj_kernel, per_batch=per_b, dk=gdn_dk),
        grid=(n // tm,),
        in_specs=[pl.BlockSpec((tm, d), row),
                  pl.BlockSpec((1, d), const),
                  pl.BlockSpec((1, 1, d), bmap),
                  pl.BlockSpec((1, 1, d), bmap),
                  _resident(w.shape, const),
                  pl.BlockSpec(conv_w.shape, const),
                  _resident(gmat.shape, const),
                  pl.BlockSpec((1, w_sb), const),
                  pl.BlockSpec((1, w_sb), const)],
        out_specs=[pl.BlockSpec((tm, n_conv), row),
                   pl.BlockSpec((tm, n_gate), row),
                   pl.BlockSpec((tm, LANES), row),
                   pl.BlockSpec((tm, w_sb), row),
                   pl.BlockSpec((tm, w_sb), row),
                   pl.BlockSpec((tm, w_sb), row)],
        out_shape=[jax.ShapeDtypeStruct((n, n_conv), F32),
                   jax.ShapeDtypeStruct((n, n_gate), F32),
                   jax.ShapeDtypeStruct((n, LANES), F32),
                   jax.ShapeDtypeStruct((n, w_sb), BF16),
                   jax.ShapeDtypeStruct((n, w_sb), BF16),
                   jax.ShapeDtypeStruct((n, w_sb), BF16)],
        scratch_shapes=[pltpu.VMEM((CONV_PAD + tm, n_conv), F32)],
        compiler_params=_params(1),
        name="hybrid_in_proj",
    )(x, gain, sc, sh, w, conv_w, gmat, qg, kg)


def _dot3s(a, b):
    ah, al = a
    bh, bl = b
    n = bh.shape[1]
    both = _dot(ah, jnp.concatenate([bh, bl], axis=1))
    return (both[:, :n] + both[:, n:]) + _dot(al, bh)


def _lower_child_rows(x, size):
    return jnp.concatenate([x[r + size:r + 2 * size] for r in range(0, x.shape[0], 2 * size)], axis=0)


def _unit_lower_inverse_all(lowers, eye, row, col):
    bits = 3
    same = (row >> bits) == (col >> bits)
    diags = [jnp.where(same, l, 0.0) for l in lowers]
    d_s = [_split2(d) for d in diags]
    x0_s = [_split2(eye - d) for d in diags]
    p2_s = [_split2(_dot3s(d, d)) for d in d_s]
    xs = [(eye - d) + _dot3s(x, p) for d, x, p in zip(diags, x0_s, p2_s)]
    p4_s = [_split2(_dot3s(p, p)) for p in p2_s]
    xs = [x + _dot3s(_split2(x), p) for x, p in zip(xs, p4_s)]
    n = lowers[0].shape[0]
    size = 1 << bits
    while size < n:
        parent = (row >> (bits + 1)) == (col >> (bits + 1))
        child = (row >> bits) == (col >> bits)
        off_s = [_split2(jnp.where(parent, jnp.where(child, 0.0, l), 0.0)) for l in lowers]
        x_s = [_split2(x) for x in xs]
        ys = [_dot3s(_split2(_lower_child_rows(x, size)), o) for x, o in zip(xs, off_s)]
        zs = [_dot3s(_split2(y), x) for y, x in zip(ys, x_s)]
        new_xs = []
        for x, z in zip(xs, zs):
            pieces = []
            for k, r in enumerate(range(0, n, 2 * size)):
                pieces.append(x[r:r + size])
                pieces.append(x[r + size:r + 2 * size] - z[k * size:(k + 1) * size])
            new_xs.append(jnp.concatenate(pieces, axis=0))
        xs = new_xs
        bits += 1
        size *= 2
    return xs


def _gdn_kernel(q_ref, k_ref, v_ref, gate_ref, ab_ref, alog_ref, dtb_ref, gn_ref, o_ref, state_ref):
    t = pl.program_id(1)
    tb, width = q_ref.shape
    c = GDN_BLOCK
    n_heads = state_ref.shape[0]
    dk = width // n_heads
    n_chunks = tb // c

    @pl.when(t == 0)
    def _():
        state_ref[...] = jnp.zeros_like(state_ref)

    lane = lax.broadcasted_iota(jnp.int32, (tb, LANES), 1)
    ab = ab_ref[...]
    g_lanes = -jnp.exp(alog_ref[...]) * _softplus(ab + dtb_ref[...])
    beta_lanes = jax.nn.sigmoid(ab)

    def lane_column(x, idx):
        return jnp.broadcast_to(jnp.sum(jnp.where(lane == idx, x, 0.0), axis=1, keepdims=True), (tb, LANES))

    row = lax.broadcasted_iota(jnp.int32, (c, c), 0)
    col = lax.broadcasted_iota(jnp.int32, (c, c), 1)
    incl = row >= col
    strict = row > col
    eye = jnp.where(row == col, 1.0, 0.0).astype(F32)
    tri = jnp.where(incl, 1.0, 0.0).astype(BF16)

    items = []
    for h in range(n_heads):
        g_h = lane_column(g_lanes, h)
        beta_h = lane_column(beta_lanes, h + n_heads)
        cols = slice(h * dk, (h + 1) * dk)
        for n in range(n_chunks):
            rows = slice(n * c, (n + 1) * c)
            items.append(dict(q=q_ref[rows, cols], k=k_ref[rows, cols], v=v_ref[rows, cols],
                              beta=beta_h[rows], g=g_h[rows]))

    def prepare(group):
        for it in group:
            it["gcol"] = _dot_exact_lhs(tri, it["g"])
        for it in group:
            gcol = it["gcol"]
            it["dec"] = jnp.exp(jnp.where(incl, gcol - gcol.T, 0.0))
            it["eg"] = jnp.exp(gcol)
            it["g_last"] = gcol[c - 1:c, :]
            it["kb"] = it["k"] * it["beta"]
            it["kk"] = it["k"].astype(BF16)
        for it in group:
            it["lower"] = jnp.where(strict, _dot_nt(it["kb"].astype(BF16), it["kk"]) * it["dec"], 0.0)
        tinvs = _unit_lower_inverse_all([it["lower"] for it in group], eye, row, col)
        for it, tinv in zip(group, tinvs):
            rhs = jnp.concatenate([it["v"] * it["beta"], it["kb"] * it["eg"]], axis=1)
            uw = _dot3(tinv, rhs)
            it["u"] = uw[:, :dk]
            a = jnp.where(incl, _dot_nt(it["q"].astype(BF16), it["kk"]) * it["dec"], 0.0)
            q_dec = it["q"] * it["eg"]
            k_dec = it["k"] * jnp.exp(it["g_last"] - it["gcol"])
            it["wq"] = jnp.concatenate([uw[:, dk:], q_dec], axis=0).astype(BF16)
            it["ak"] = jnp.concatenate([a, k_dec.T], axis=0).astype(BF16)
            it["decay"] = jnp.exp(it["g_last"])

    early = n_chunks // 2
    prepare([it for i, it in enumerate(items) if i % n_chunks < early])
    prepare([it for i, it in enumerate(items) if i % n_chunks >= early])

    gn = gn_ref[...]
    states = [state_ref[h] for h in range(n_heads)]
    for n in range(n_chunks):
        rows = slice(n * c, (n + 1) * c)
        for h in range(n_heads):
            it = items[h * n_chunks + n]
            cols = slice(h * dk, (h + 1) * dk)
            ws_qs = _dot(it["wq"], states[h].astype(BF16))
            v_new = it["u"] - ws_qs[:c]
            av_kv = _dot(it["ak"], v_new.astype(BF16))
            o = ws_qs[c:] + av_kv[:c]
            states[h] = states[h] * it["decay"] + av_kv[c:]
            ms = jnp.mean(o * o, axis=-1, keepdims=True)
            o = o * lax.rsqrt(ms + EPS) * gn
            o_ref[rows, cols] = (o * _silu(gate_ref[rows, cols])).astype(o_ref.dtype)
    for h in range(n_heads):
        state_ref[h] = states[h]


def _gdn(qkv, gate, ab, a_log, dt_bias, gdn_norm, batch, seq):
    n, width = gate.shape
    n_heads = a_log.shape[0]
    dk = width // n_heads
    tb = GDN_STEP
    nt = seq // tb
    rows = lambda b, t: b * nt + t
    pad = lambda x: jnp.zeros((1, LANES), F32).at[0, :n_heads].set(x)
    col_spec = lambda k: pl.BlockSpec((tb, width), lambda b, t: (rows(b, t), k))
    const = lambda b, t: (0, 0)
    return pl.pallas_call(
        _gdn_kernel,
        grid=(batch, nt),
        in_specs=[col_spec(0), col_spec(1), col_spec(2), col_spec(0),
                  pl.BlockSpec((tb, LANES), lambda b, t: (rows(b, t), 0)),
                  pl.BlockSpec((1, LANES), const), pl.BlockSpec((1, LANES), const),
                  pl.BlockSpec((1, dk), const)],
        out_specs=pl.BlockSpec((tb, width), lambda b, t: (rows(b, t), 0)),
        out_shape=jax.ShapeDtypeStruct((n, width), BF16),
        scratch_shapes=[pltpu.VMEM((n_heads, dk, dk), F32)],
        compiler_params=_params(2),
        name="gated_delta_rule",
    )(qkv, qkv, qkv, gate, ab, pad(a_log), pad(dt_bias), gdn_norm.reshape(1, dk))


def _sb_kernel(q_ref, k_ref, v_ref, m_ref, o_ref, q2_ref, acc_ref, carry_ref):
    i = pl.program_id(2)
    tq = q_ref.shape[0]
    hq = tq // 2
    lane = lax.broadcasted_iota(jnp.int32, (hq, LANES), 1)
    suffix = m_ref[...]

    for blk in range(4):
        q = q_ref[(blk // 2) * hq:(blk // 2 + 1) * hq, :]
        keep = (lane < SB_DH) if blk % 2 == 0 else (lane >= SB_DH)
        q2_ref[blk * hq:(blk + 1) * hq, :] = jnp.where(keep, q, jnp.zeros_like(q))

    def scores(j, rows):
        start = pl.multiple_of(j * tq, tq)
        k = k_ref[pl.ds(start, tq), :]
        y = _dot_nt(q2_ref[rows, :], k) * LOG2_E
        cost = jnp.maximum(y, 0.0) + jnp.log2(1.0 + jnp.exp2(-jnp.abs(y)))
        return y, cost

    def values(j):
        return v_ref[pl.ds(pl.multiple_of(j * tq, tq), tq), :]

    def diagonal_tile():
        y, cost = scores(i, slice(0, 2 * tq))
        p = lax.broadcasted_iota(jnp.int32, (2 * tq, tq), 0)
        r = (p & (hq - 1)) + hq * (p >> (tq.bit_length() - 1))
        s = lax.broadcasted_iota(jnp.int32, (2 * tq, tq), 1)
        mask = s < r
        masked_cost = jnp.where(mask, cost, 0.0)
        later = _dot_exact_rhs(masked_cost, suffix)
        a = jnp.where(mask, jnp.exp2(y - cost - later), 0.0)
        return _dot(a.astype(BF16), values(i)), jnp.sum(masked_cost, axis=1, keepdims=True)

    def full_tile(j, rows, carry):
        y, cost = scores(j, rows)
        later = _dot_exact_rhs(cost, suffix)
        a = jnp.exp2(y - cost - later)
        return _dot(a.astype(BF16), values(j)) * jnp.exp2(-carry), carry + jnp.sum(cost, axis=1, keepdims=True)

    early = slice(0, tq)
    late = slice(tq, 2 * tq)

    @pl.when(i == 0)
    def _():
        acc, carry = diagonal_tile()
        acc_ref[...] = acc
        carry_ref[...] = carry

    @pl.when(i > 0)
    def _():
        acc, carry = diagonal_tile()
        part, carry_early = full_tile(i - 1, early, carry[early])
        acc_ref[early, :] = acc[early] + part
        carry_ref[early, :] = carry_early
        acc_ref[late, :] = acc[late]
        carry_ref[late, :] = carry[late]

        @pl.when(jnp.min(carry[late]) < SB_UNDERFLOW_LOG2)
        def _():
            part, carry_late = full_tile(i - 1, late, carry_ref[late, :])
            acc_ref[late, :] += part
            carry_ref[late, :] = carry_late

    def cond(state):
        step, least = state
        return jnp.logical_and(step < i - 1, least < SB_UNDERFLOW_LOG2)

    def body(state):
        step, _ = state
        part, carry = full_tile(i - 2 - step, slice(0, 2 * tq), carry_ref[...])
        acc_ref[...] += part
        carry_ref[...] = carry
        return step + 1, jnp.min(carry)

    lax.while_loop(cond, body, (jnp.int32(0), jnp.min(carry_ref[...])))
    for half in range(2):
        acc0 = acc_ref[(2 * half) * hq:(2 * half + 1) * hq, :]
        acc1 = acc_ref[(2 * half + 1) * hq:(2 * half + 2) * hq, :]
        o_ref[half * hq:(half + 1) * hq, :] = jnp.where(lane < SB_DH, acc0, acc1).astype(o_ref.dtype)


def _sb_attention(qn, kn, vn, suffix, batch, seq):
    n, width = qn.shape
    pairs = width // LANES
    tq = SB_TILE
    nq = seq // tq
    return pl.pallas_call(
        _sb_kernel,
        grid=(batch, pairs, nq),
        in_specs=[pl.BlockSpec((tq, LANES), lambda b, p, i: (b * nq + i, p)),
                  pl.BlockSpec((seq, LANES), lambda b, p, i: (b, p)),
                  pl.BlockSpec((seq, LANES), lambda b, p, i: (b, p)),
                  pl.BlockSpec((tq, tq), lambda b, p, i: (0, 0))],
        out_specs=pl.BlockSpec((tq, LANES), lambda b, p, i: (b * nq + i, p)),
        out_shape=jax.ShapeDtypeStruct((n, width), BF16),
        scratch_shapes=[pltpu.VMEM((2 * tq, LANES), BF16),
                        pltpu.VMEM((2 * tq, LANES), F32),
                        pltpu.VMEM((2 * tq, 1), F32)],
        compiler_params=_params(3),
        name="stick_breaking_attention",
    )(qn, kn, vn, suffix)


def _ret_proj_kernel(x_ref, g_ref, sc_ref, sh_ref, w_ref, cos_ref, sin_ref, qs_ref, ks_ref,
                     q_ref, k_ref, v_ref, gate_ref):
    hb = _norm_mod(x_ref[...], g_ref[...], sc_ref[0], sh_ref[0]).astype(BF16)
    n_heads = qs_ref.shape[0]
    w_qk = q_ref.shape[1]
    dk = w_qk // n_heads
    half = dk // 2
    cos = cos_ref[...]
    sin = sin_ref[...]

    def rot(x, scale):
        x1, x2 = x[:, :half], x[:, half:]
        return jnp.concatenate([(x1 * cos - x2 * sin) * scale, (x1 * sin + x2 * cos) * scale], axis=-1)

    for h in range(n_heads):
        cols = slice(h * dk, (h + 1) * dk)
        q_ref[:, cols] = rot(_dot(hb, w_ref[:, cols]), qs_ref[h]).astype(BF16)
    for h in range(n_heads):
        cols = slice(h * dk, (h + 1) * dk)
        k_ref[:, cols] = rot(_dot(hb, w_ref[:, w_qk + h * dk:w_qk + (h + 1) * dk]), ks_ref[h]).astype(BF16)
    step = 512
    off = 2 * w_qk
    for ref in (v_ref, gate_ref):
        for o in range(0, ref.shape[1], step):
            ref[:, o:o + step] = _dot(hb, w_ref[:, off + o:off + o + step]).astype(ref.dtype)
        off += ref.shape[1]


def _ret_proj(x, gain, sc, sh, w, cos, sin, q_scale, k_scale, n_qk, n_v, seq):
    n, d = x.shape
    tm = RET_CHUNK
    per_b = seq // tm
    bmap = lambda i: (i // per_b, 0, 0)
    row = lambda i: (i, 0)
    const = lambda i: (0, 0)
    pos = lambda i: (i % per_b, 0)
    return pl.pallas_call(
        _ret_proj_kernel,
        grid=(n // tm,),
        in_specs=[pl.BlockSpec((tm, d), row),
                  pl.BlockSpec((1, d), const),
                  pl.BlockSpec((1, 1, d), bmap),
                  pl.BlockSpec((1, 1, d), bmap),
                  _resident(w.shape, const),
                  pl.BlockSpec((tm, cos.shape[1]), pos),
                  pl.BlockSpec((tm, cos.shape[1]), pos),
                  pl.BlockSpec(q_scale.shape, lambda i: (0, 0, 0)),
                  pl.BlockSpec(k_scale.shape, lambda i: (0, 0, 0))],
        out_specs=[pl.BlockSpec((tm, n_qk), row),
                   pl.BlockSpec((tm, n_qk), row),
                   pl.BlockSpec((tm, n_v), row),
                   pl.BlockSpec((tm, n_v), row)],
        out_shape=[jax.ShapeDtypeStruct((n, n_qk), BF16),
                   jax.ShapeDtypeStruct((n, n_qk), BF16),
                   jax.ShapeDtypeStruct((n, n_v), BF16),
                   jax.ShapeDtypeStruct((n, n_v), F32)],
        compiler_params=_params(1),
        name="retention_in_proj",
    )(x, gain, sc, sh, w, cos, sin, q_scale, k_scale)


def _ret_kernel(q_ref, k_ref, v_ref, gate_ref, decay_ref, o_ref, state_ref):
    t = pl.program_id(1)
    c = q_ref.shape[0]
    n_heads = state_ref.shape[0]
    dk = q_ref.shape[1] // n_heads
    dv = v_ref.shape[1] // n_heads

    @pl.when(t == 0)
    def _():
        state_ref[...] = jnp.zeros_like(state_ref)

    row = lax.broadcasted_iota(jnp.int32, (c, c), 0)
    col = lax.broadcasted_iota(jnp.int32, (c, c), 1)
    causal = row >= col

    for h in range(n_heads):
        q = q_ref[:, h * dk:(h + 1) * dk]
        k = k_ref[:, h * dk:(h + 1) * dk]
        v = v_ref[:, h * dv:(h + 1) * dv]
        intra = jnp.where(causal, _dot_nt(q, k), 0.0)
        state = state_ref[h]
        o = _dot(intra.astype(BF16), v) + _dot(q, state.astype(BF16))
        k_t = k.astype(F32).T.astype(BF16)
        state_ref[h] = (state + _dot(k_t, v)) * decay_ref[h]
        ms = jnp.mean(o * o, axis=-1, keepdims=True)
        o = o * lax.rsqrt(ms + EPS)
        o_ref[:, h * dv:(h + 1) * dv] = (o * _silu(gate_ref[:, h * dv:(h + 1) * dv])).astype(o_ref.dtype)


def _retention(q, k, v, gate, chunk_decay, batch, seq):
    n, w_qk = q.shape
    n_heads = chunk_decay.shape[0]
    w_v = v.shape[1]
    c = RET_CHUNK
    nt = seq // c
    rows = lambda b, t: (b * nt + t, 0)
    return pl.pallas_call(
        _ret_kernel,
        grid=(batch, nt),
        in_specs=[pl.BlockSpec((c, w_qk), rows),
                  pl.BlockSpec((c, w_qk), rows),
                  pl.BlockSpec((c, w_v), rows),
                  pl.BlockSpec((c, w_v), rows),
                  pl.BlockSpec(chunk_decay.shape, lambda b, t: (0, 0, 0))],
        out_specs=pl.BlockSpec((c, w_v), rows),
        out_shape=jax.ShapeDtypeStruct((n, w_v), BF16),
        scratch_shapes=[pltpu.VMEM((n_heads, w_qk // n_heads, w_v // n_heads), F32)],
        compiler_params=_params(2),
        name="retention_chunkwise",
    )(q, k, v, gate, chunk_decay)


def _post_kernel(*refs, n_mix):
    x_ref = refs[0]
    o_refs = refs[1:1 + n_mix]
    wm_refs = refs[1 + n_mix:1 + 2 * n_mix]
    gtm_ref, g_ref, sc_ref, sh_ref, gtf_ref, win_ref, wout_ref, out_ref, acc_ref = refs[1 + 2 * n_mix:]
    y = _dot(o_refs[0][...], wm_refs[0][...])
    for o_ref, w_ref in zip(o_refs[1:], wm_refs[1:]):
        y = y + _dot(o_ref[...], w_ref[...])
    x1 = x_ref[...] + gtm_ref[0] * y
    hb = _norm_mod(x1, g_ref[...], sc_ref[0], sh_ref[0]).astype(BF16)
    hidden = wout_ref.shape[0]
    for idx, off in enumerate(range(0, hidden, FFN_CHUNK)):
        g = _dot(hb, win_ref[:, off:off + FFN_CHUNK])
        u = _dot(hb, win_ref[:, hidden + off:hidden + off + FFN_CHUNK])
        part = _dot((_silu(g) * u).astype(BF16), wout_ref[off:off + FFN_CHUNK, :])
        if idx == 0:
            acc_ref[...] = part
        else:
            acc_ref[...] += part
    out_ref[...] = x1 + gtf_ref[0] * acc_ref[...]


def _post(x, mix_outs, mix_ws, gt_m, gain, sc, sh, gt_f, w_in, w_out, seq):
    n, d = x.shape
    tm = ROW_TILE_FFN
    per_b = seq // tm
    bmap = lambda i: (i // per_b, 0, 0)
    row = lambda i: (i, 0)
    const = lambda i: (0, 0)
    vec = pl.BlockSpec((1, 1, d), bmap)
    n_mix = len(mix_outs)
    return pl.pallas_call(
        functools.partial(_post_kernel, n_mix=n_mix),
        grid=(n // tm,),
        in_specs=([pl.BlockSpec((tm, d), row)]
                  + [pl.BlockSpec((tm, o.shape[1]), row) for o in mix_outs]
                  + [_resident(w.shape, const) for w in mix_ws]
                  + [vec, pl.BlockSpec((1, d), const), vec, vec, vec,
                     _resident(w_in.shape, const), _resident(w_out.shape, const)]),
        out_specs=pl.BlockSpec((tm, d), row),
        out_shape=jax.ShapeDtypeStruct((n, d), F32),
        scratch_shapes=[pltpu.VMEM((tm, d), F32)],
        compiler_params=_params(1),
        name="out_proj_swiglu",
    )(x, *mix_outs, *mix_ws, gt_m, gain, sc, sh, gt_f, w_in, w_out)


def kernel(x, c, ada_w, ada_b, norm_mix, norm_ffn, hyb_w_in, hyb_conv, gdn_a_log, gdn_dt_bias, gdn_norm, sb_q_norm, sb_k_norm, hyb_w_out, ret_w_in, ret_w_out, ffn_w_in, ffn_w_out):
    batch, seq, d = x.shape
    depth = ada_w.shape[0]
    gdn_heads = gdn_a_log.shape[1]
    gdn_dv = gdn_norm.shape[1]
    gdn_w = gdn_heads * gdn_dv
    sb_dh = sb_q_norm.shape[1]
    assert sb_dh == SB_DH
    sb_w = hyb_w_out.shape[1] - gdn_w
    sb_heads = sb_w // sb_dh
    ret_heads = 4
    ret_qk = ret_w_in.shape[2] // 6
    ret_v = 2 * ret_qk
    ret_dk = ret_qk // ret_heads

    xf = x.reshape(batch * seq, d)
    mod = _modulation(c, ada_w, ada_b)

    head_mean = jnp.kron(jnp.eye(sb_heads, dtype=F32), jnp.full((sb_dh, sb_dh), 1.0 / sb_dh, F32)).astype(BF16)
    idx = jnp.arange(SB_TILE)
    suffix = (idx[:, None] > idx[None, :]).astype(BF16)
    pos = jnp.arange(seq, dtype=F32)
    inv = 1.0 / (ROPE_BASE ** (jnp.arange(0, ret_dk, 2, dtype=F32) / ret_dk))
    ang = pos[:, None] * inv[None, :]
    cos, sin = jnp.cos(ang), jnp.sin(ang)
    log_gamma = jnp.log1p(-jnp.exp2(-5.0 - jnp.arange(ret_heads, dtype=F32)))
    steps = jnp.arange(1, RET_CHUNK + 1, dtype=F32)
    ret_q_scale = jnp.broadcast_to(jnp.exp(log_gamma[:, None] * steps[None, :])[:, :, None],
                                   (ret_heads, RET_CHUNK, ret_dk // 2))
    ret_k_scale = jnp.broadcast_to((jnp.exp(-log_gamma[:, None] * steps[None, :]) * ret_dk ** -0.5)[:, :, None],
                                   (ret_heads, RET_CHUNK, ret_dk // 2))
    ret_chunk_decay = jnp.broadcast_to(jnp.exp(log_gamma * RET_CHUNK)[:, None, None], (ret_heads, 1, ret_v // ret_heads))

    for l in range(depth):
        sh_m, sc_m, gt_m, sh_f, sc_f, gt_f = [m.reshape(batch, 1, d) for m in jnp.split(mod[l], 6, axis=-1)]
        i = l // 2
        gain_m = norm_mix[l].reshape(1, d)
        if l % 2 == 0:
            w_in = hyb_w_in[i]
            n_main = 4 * gdn_w
            n_ab = 2 * gdn_heads
            ab_cols = jnp.zeros((d, LANES), F32).at[:, :n_ab].set(w_in[:, n_main:n_main + n_ab])
            w_re = jnp.concatenate([w_in[:, :n_main], ab_cols, w_in[:, n_main + n_ab:]], axis=1).astype(BF16)
            gdn_qkv, gdn_gate, ab, qn, kn, vn = _hyb_proj(
                xf, gain_m, sc_m, sh_m, w_re, hyb_conv[i], head_mean,
                jnp.tile(sb_q_norm[i], sb_heads).reshape(1, sb_w),
                jnp.tile(sb_k_norm[i], sb_heads).reshape(1, sb_w), seq, gdn_dv)
            o_a = _gdn(gdn_qkv, gdn_gate, ab, gdn_a_log[i], gdn_dt_bias[i], gdn_norm[i], batch, seq)
            o_b = _sb_attention(qn, kn, vn, suffix, batch, seq)
            w_out = hyb_w_out[i].astype(BF16)
            mix_outs = [o_a, o_b]
            mix_ws = [w_out[:gdn_w], w_out[gdn_w:]]
        else:
            rq, rk, rv, gate = _ret_proj(xf, gain_m, sc_m, sh_m, ret_w_in[i].astype(BF16), cos, sin,
                                         ret_q_scale, ret_k_scale, ret_qk, ret_v, seq)
            o_r = _retention(rq, rk, rv, gate, ret_chunk_decay, batch, seq)
            mix_outs = [o_r]
            mix_ws = [ret_w_out[i].astype(BF16)]
        xf = _post(xf, mix_outs, mix_ws, gt_m, norm_ffn[l].reshape(1, d), sc_f, sh_f, gt_f,
                   ffn_w_in[l].astype(BF16), ffn_w_out[l].astype(BF16), seq)
    return xf.reshape(batch, seq, d)
```

```python
import functools

import jax
import jax.numpy as jnp
from jax import lax
from jax.experimental import pallas as pl
from jax.experimental.pallas import tpu as pltpu

F32 = jnp.float32
BF16 = jnp.bfloat16
EPS = 1e-6
ROPE_BASE = 10000.0

LANES = 128
MXU_DIM = 256
VMEM_LIMIT = 56 * 1024 * 1024

GDN_CONV = 4
GDN_BLOCK = 128
GDN_STEP = 512
SB_TILE = 256
SB_DH = 64
SB_STRIPS = 2
LOG2_E = 1.4426950408889634
SB_UNDERFLOW_LOG2 = 180.0
RET_CHUNK = 256
ROW_TILE_PROJ = 512
PROJ_GROUP = 256
ROW_TILE_FFN = 512
FFN_CHUNK = 256
CONV_PAD = 8


def _dot(a, b):
    return jnp.dot(a, b, preferred_element_type=F32)


def _dot_nt(a, b):
    return lax.dot_general(a, b, (((1,), (1,)), ((), ())), preferred_element_type=F32)


def _split2(a):
    hi = a.astype(BF16)
    lo = (a - hi.astype(F32)).astype(BF16)
    return hi, lo


def _dot3(a, b):
    ah, al = _split2(a)
    bh, bl = _split2(b)
    n = b.shape[1]
    if 2 * n <= MXU_DIM:
        both = _dot(ah, jnp.concatenate([bh, bl], axis=1))
        return (both[:, :n] + both[:, n:]) + _dot(al, bh)
    return _dot(ah, bh) + (_dot(ah, bl) + _dot(al, bh))


def _dot_exact_rhs(a, m):
    ah, al = _split2(a)
    return _dot(ah, m) + _dot(al, m)


def _dot_exact_lhs(m, a):
    ah, al = _split2(a)
    n = a.shape[1]
    if 2 * n <= MXU_DIM:
        both = _dot(m, jnp.concatenate([ah, al], axis=1))
        return both[:, :n] + both[:, n:]
    return _dot(m, ah) + _dot(m, al)


def _silu(x):
    return x * jax.nn.sigmoid(x)


def _softplus(x):
    return jnp.maximum(x, 0.0) + jnp.log1p(jnp.exp(-jnp.abs(x)))


def _params(n_axes):
    return pltpu.CompilerParams(dimension_semantics=("arbitrary",) * n_axes,
                                vmem_limit_bytes=VMEM_LIMIT)


def _resident(shape, index_map):
    return pl.BlockSpec(shape, index_map, pipeline_mode=pl.Buffered(1))


def _mod_kernel(c_ref, w_ref, b_ref, o_ref):
    c = c_ref[...]
    o_ref[0] = _dot(_silu(c).astype(BF16), w_ref[0].astype(BF16)) + b_ref[0]


def _modulation(c, ada_w, ada_b):
    depth, d, n = ada_w.shape
    b = c.shape[0]
    rows = 8
    c_pad = jnp.zeros((rows, d), F32).at[:b].set(c)
    tn = 1024
    out = pl.pallas_call(
        _mod_kernel,
        grid=(depth, n // tn),
        in_specs=[pl.BlockSpec((rows, d), lambda l, j: (0, 0)),
                  pl.BlockSpec((1, d, tn), lambda l, j: (l, 0, j)),
                  pl.BlockSpec((1, 1, tn), lambda l, j: (l, 0, j))],
        out_specs=pl.BlockSpec((1, rows, tn), lambda l, j: (l, 0, j)),
        out_shape=jax.ShapeDtypeStruct((depth, rows, n), F32),
        compiler_params=_params(2),
        name="adaln_modulation",
    )(c_pad, ada_w, ada_b.reshape(depth, 1, n))
    return out[:, :b]


def _norm_mod(x, gain, sc, sh):
    ms = jnp.mean(x * x, axis=-1, keepdims=True)
    y = x * lax.rsqrt(ms + EPS)
    y = y * gain
    return y * (1.0 + sc) + sh


def _hyb_proj_kernel(x_ref, g_ref, sc_ref, sh_ref, w_ref, cw_ref, gm_ref, qg_ref, kg_ref,
                     qkv_ref, gate_ref, ab_ref, q_ref, k_ref, v_ref, ext_ref, *, per_batch, dk):
    i = pl.program_id(0)
    tm = x_ref.shape[0]
    n_conv = qkv_ref.shape[1]
    n_gate = gate_ref.shape[1]
    w_sb = q_ref.shape[1]
    groups = [slice(r0, r0 + PROJ_GROUP) for r0 in range(0, tm, PROJ_GROUP)]
    hbs = [_norm_mod(x_ref[rows, :], g_ref[...], sc_ref[0], sh_ref[0]).astype(BF16) for rows in groups]

    @pl.when(i % per_batch == 0)
    def _():
        ext_ref[0:CONV_PAD, :] = jnp.zeros((CONV_PAD, n_conv), F32)

    def l2n(y):
        return y * lax.rsqrt(jnp.sum(y * y, axis=-1, keepdims=True) + EPS)

    for c0 in range(0, n_conv, MXU_DIM):
        for rows, hb in zip(groups, hbs):
            ext_ref[CONV_PAD + rows.start:CONV_PAD + rows.stop, c0:c0 + MXU_DIM] = _dot(hb, w_ref[:, c0:c0 + MXU_DIM])

    def conv_chunk(c0, rows):
        cols = slice(c0, c0 + MXU_DIM)
        w = cw_ref[:, cols]
        y = ext_ref[CONV_PAD + rows.start:CONV_PAD + rows.stop, cols] * w[GDN_CONV - 1:GDN_CONV, :]
        for j in range(GDN_CONV - 1):
            shift = GDN_CONV - 1 - j
            y = y + ext_ref[CONV_PAD + rows.start - shift:CONV_PAD + rows.stop - shift, cols] * w[j:j + 1, :]
        y = _silu(y)
        if c0 < 2 * n_gate:
            y = jnp.concatenate([l2n(y[:, h0:h0 + dk]) for h0 in range(0, MXU_DIM, dk)], axis=1)
            if c0 < n_gate:
                y = y * (dk ** -0.5)
        qkv_ref[rows, cols] = y

    def head_rms(xf, gain):
        sq = xf * xf
        ms = jnp.concatenate([_dot_exact_rhs(sq[:, c0:c0 + MXU_DIM], gm_ref[...])
                              for c0 in range(0, xf.shape[1], MXU_DIM)], axis=1)
        return xf * lax.rsqrt(ms + EPS) * gain

    off_gate = n_conv
    off_ab = off_gate + n_gate
    off_q = off_ab + LANES
    off_k = off_q + w_sb
    off_v = off_k + w_sb
    conv_starts = list(range(0, n_conv, MXU_DIM))
    for rows, hb in zip(groups, hbs):
        gate_ref[rows, :] = _dot(hb, w_ref[:, off_gate:off_gate + n_gate])
    for rows, hb in zip(groups, hbs):
        sq = _dot(hb, w_ref[:, off_q:off_q + w_sb])
        q_ref[rows, :] = (head_rms(sq, qg_ref[...]) * (SB_DH ** -0.5)).astype(BF16)
    for rows, hb in zip(groups, hbs):
        sk = _dot(hb, w_ref[:, off_k:off_k + w_sb])
        k_ref[rows, :] = head_rms(sk, kg_ref[...]).astype(BF16)
    for rows, hb in zip(groups, hbs):
        v_ref[rows, :] = _dot(hb, w_ref[:, off_v:off_v + w_sb]).astype(BF16)
        ab_ref[rows, :] = _dot(hb, w_ref[:, off_ab:off_ab + LANES])
    for c0 in conv_starts:
        for rows in groups:
            conv_chunk(c0, rows)
    ext_ref[0:CONV_PAD, :] = ext_ref[tm:tm + CONV_PAD, :]


def _hyb_proj(x, gain, sc, sh, w, conv_w, gmat, qg, kg, seq, gdn_dk):
    n, d = x.shape
    tm = ROW_TILE_PROJ
    per_b = seq // tm
    w_sb = qg.shape[1]
    n_conv = conv_w.shape[1]
    n_gate = w.shape[1] - n_conv - LANES - 3 * w_sb
    bmap = lambda i: (i // per_b, 0, 0)
    row = lambda i: (i, 0)
    const = lambda i: (0, 0)
    return pl.pallas_call(
        functools.partial(_hyb_proj_kernel, per_batch=per_b, dk=gdn_dk),
        grid=(n // tm,),
        in_specs=[pl.BlockSpec((tm, d), row),
                  pl.BlockSpec((1, d), const),
                  pl.BlockSpec((1, 1, d), bmap),
                  pl.BlockSpec((1, 1, d), bmap),
                  _resident(w.shape, const),
                  pl.BlockSpec(conv_w.shape, const),
                  _resident(gmat.shape, const),
                  pl.BlockSpec((1, w_sb), const),
                  pl.BlockSpec((1, w_sb), const)],
        out_specs=[pl.BlockSpec((tm, n_conv), row),
                   pl.BlockSpec((tm, n_gate), row),
                   pl.BlockSpec((tm, LANES), row),
                   pl.BlockSpec((tm, w_sb), row),
                   pl.BlockSpec((tm, w_sb), row),
                   pl.BlockSpec((tm, w_sb), row)],
        out_shape=[jax.ShapeDtypeStruct((n, n_conv), F32),
                   jax.ShapeDtypeStruct((n, n_gate), F32),
                   jax.ShapeDtypeStruct((n, LANES), F32),
                   jax.ShapeDtypeStruct((n, w_sb), BF16),
                   jax.ShapeDtypeStruct((n, w_sb), BF16),
                   jax.ShapeDtypeStruct((n, w_sb), BF16)],
        scratch_shapes=[pltpu.VMEM((CONV_PAD + tm, n_conv), F32)],
        compiler_params=_params(1),
        name="hybrid_in_proj",
    )(x, gain, sc, sh, w, conv_w, gmat, qg, kg)


def _dot3s(a, b):
    ah, al = a
    bh, bl = b
    n = bh.shape[1]
    both = _dot(ah, jnp.concatenate([bh, bl], axis=1))
    return (both[:, :n] + both[:, n:]) + _dot(al, bh)


def _lower_child_rows(x, size):
    return jnp.concatenate([x[r + size:r + 2 * size] for r in range(0, x.shape[0], 2 * size)], axis=0)


def _unit_lower_inverse_all(lowers, eye, row, col):
    bits = 3
    same = (row >> bits) == (col >> bits)
    diags = [jnp.where(same, l, 0.0) for l in lowers]
    d_s = [_split2(d) for d in diags]
    x0_s = [_split2(eye - d) for d in diags]
    p2_s = [_split2(_dot3s(d, d)) for d in d_s]
    xs = [(eye - d) + _dot3s(x, p) for d, x, p in zip(diags, x0_s, p2_s)]
    p4_s = [_split2(_dot3s(p, p)) for p in p2_s]
    xs = [x + _dot3s(_split2(x), p) for x, p in zip(xs, p4_s)]
    n = lowers[0].shape[0]
    size = 1 << bits
    while size < n:
        parent = (row >> (bits + 1)) == (col >> (bits + 1))
        child = (row >> bits) == (col >> bits)
        off_s = [_split2(jnp.where(parent, jnp.where(child, 0.0, l), 0.0)) for l in lowers]
        x_s = [_split2(x) for x in xs]
        ys = [_dot3s(_split2(_lower_child_rows(x, size)), o) for x, o in zip(xs, off_s)]
        zs = [_dot3s(_split2(y), x) for y, x in zip(ys, x_s)]
        new_xs = []
        for x, z in zip(xs, zs):
            pieces = []
            for k, r in enumerate(range(0, n, 2 * size)):
                pieces.append(x[r:r + size])
                pieces.append(x[r + size:r + 2 * size] - z[k * size:(k + 1) * size])
            new_xs.append(jnp.concatenate(pieces, axis=0))
        xs = new_xs
        bits += 1
        size *= 2
    return xs


def _gdn_kernel(q_ref, k_ref, v_ref, gate_ref, ab_ref, alog_ref, dtb_ref, gn_ref, o_ref, state_ref):
    t = pl.program_id(1)
    tb, width = q_ref.shape
    c = GDN_BLOCK
    n_heads = state_ref.shape[0]
    dk = width // n_heads
    n_chunks = tb // c

    @pl.when(t == 0)
    def _():
        state_ref[...] = jnp.zeros_like(state_ref)

    lane = lax.broadcasted_iota(jnp.int32, (tb, LANES), 1)
    ab = ab_ref[...]
    g_lanes = -jnp.exp(alog_ref[...]) * _softplus(ab + dtb_ref[...])
    beta_lanes = jax.nn.sigmoid(ab)

    def lane_column(x, idx):
        return jnp.broadcast_to(jnp.sum(jnp.where(lane == idx, x, 0.0), axis=1, keepdims=True), (tb, LANES))

    row = lax.broadcasted_iota(jnp.int32, (c, c), 0)
    col = lax.broadcasted_iota(jnp.int32, (c, c), 1)
    incl = row >= col
    strict = row > col
    eye = jnp.where(row == col, 1.0, 0.0).astype(F32)
    tri = jnp.where(incl, 1.0, 0.0).astype(BF16)

    items = []
    for h in range(n_heads):
        g_h = lane_column(g_lanes, h)
        beta_h = lane_column(beta_lanes, h + n_heads)
        cols = slice(h * dk, (h + 1) * dk)
        for n in range(n_chunks):
            rows = slice(n * c, (n + 1) * c)
            items.append(dict(q=q_ref[rows, cols], k=k_ref[rows, cols], v=v_ref[rows, cols],
                              beta=beta_h[rows], g=g_h[rows]))

    def prepare(group):
        for it in group:
            it["gcol"] = _dot_exact_lhs(tri, it["g"])
        for it in group:
            gcol = it["gcol"]
            it["dec"] = jnp.exp(jnp.where(incl, gcol - gcol.T, 0.0))
            it["eg"] = jnp.exp(gcol)
            it["g_last"] = gcol[c - 1:c, :]
            it["kb"] = it["k"] * it["beta"]
            it["kk"] = it["k"].astype(BF16)
        for it in group:
            it["lower"] = jnp.where(strict, _dot_nt(it["kb"].astype(BF16), it["kk"]) * it["dec"], 0.0)
        tinvs = _unit_lower_inverse_all([it["lower"] for it in group], eye, row, col)
        for it, tinv in zip(group, tinvs):
            rhs = jnp.concatenate([it["v"] * it["beta"], it["kb"] * it["eg"]], axis=1)
            uw = _dot3(tinv, rhs)
            it["u"] = uw[:, :dk]
            a = jnp.where(incl, _dot_nt(it["q"].astype(BF16), it["kk"]) * it["dec"], 0.0)
            q_dec = it["q"] * it["eg"]
            k_dec = it["k"] * jnp.exp(it["g_last"] - it["gcol"])
            it["wq"] = jnp.concatenate([uw[:, dk:], q_dec], axis=0).astype(BF16)
            it["ak"] = jnp.concatenate([a, k_dec.T], axis=0).astype(BF16)
            it["decay"] = jnp.exp(it["g_last"])

    early = n_chunks // 2
    prepare([it for i, it in enumerate(items) if i % n_chunks < early])
    prepare([it for i, it in enumerate(items) if i % n_chunks >= early])

    gn = gn_ref[...]
    states = [state_ref[h] for h in range(n_heads)]
    for n in range(n_chunks):
        rows = slice(n * c, (n + 1) * c)
        for h in range(n_heads):
            it = items[h * n_chunks + n]
            cols = slice(h * dk, (h + 1) * dk)
            ws_qs = _dot(it["wq"], states[h].astype(BF16))
            v_new = it["u"] - ws_qs[:c]
            av_kv = _dot(it["ak"], v_new.astype(BF16))
            o = ws_qs[c:] + av_kv[:c]
            states[h] = states[h] * it["decay"] + av_kv[c:]
            ms = jnp.mean(o * o, axis=-1, keepdims=True)
            o = o * lax.rsqrt(ms + EPS) * gn
            o_ref[rows, cols] = (o * _silu(gate_ref[rows, cols])).astype(o_ref.dtype)
    for h in range(n_heads):
        state_ref[h] = states[h]


def _gdn(qkv, gate, ab, a_log, dt_bias, gdn_norm, batch, seq):
    n, width = gate.shape
    n_heads = a_log.shape[0]
    dk = width // n_heads
    tb = GDN_STEP
    nt = seq // tb
    rows = lambda b, t: b * nt + t
    pad = lambda x: jnp.zeros((1, LANES), F32).at[0, :n_heads].set(x)
    col_spec = lambda k: pl.BlockSpec((tb, width), lambda b, t: (rows(b, t), k))
    const = lambda b, t: (0, 0)
    return pl.pallas_call(
        _gdn_kernel,
        grid=(batch, nt),
        in_specs=[col_spec(0), col_spec(1), col_spec(2), col_spec(0),
                  pl.BlockSpec((tb, LANES), lambda b, t: (rows(b, t), 0)),
                  pl.BlockSpec((1, LANES), const), pl.BlockSpec((1, LANES), const),
                  pl.BlockSpec((1, dk), const)],
        out_specs=pl.BlockSpec((tb, width), lambda b, t: (rows(b, t), 0)),
        out_shape=jax.ShapeDtypeStruct((n, width), BF16),
        scratch_shapes=[pltpu.VMEM((n_heads, dk, dk), F32)],
        compiler_params=_params(2),
        name="gated_delta_rule",
    )(qkv, qkv, qkv, gate, ab, pad(a_log), pad(dt_bias), gdn_norm.reshape(1, dk))


def _sb_kernel(q_ref, k_ref, v_ref, m_ref, o_ref, q2_ref, acc_ref, carry_ref):
    i = pl.program_id(2)
    tq = q_ref.shape[0]
    lane = lax.broadcasted_iota(jnp.int32, (tq, LANES), 1)
    q = q_ref[...]
    zero = jnp.zeros_like(q)
    q2_ref[0:tq, :] = jnp.where(lane < SB_DH, q, zero)
    q2_ref[tq:2 * tq, :] = jnp.where(lane >= SB_DH, q, zero)
    suffix = m_ref[...]

    def tile(j, diagonal):
        start = pl.multiple_of(j * tq, tq)
        k = k_ref[pl.ds(start, tq), :]
        v = v_ref[pl.ds(start, tq), :]
        strip = tq // SB_STRIPS
        heads = [slice(n * strip, (n + 1) * strip) for n in range(2 * SB_STRIPS)]
        ys = [_dot_nt(q2_ref[rows, :], k) * LOG2_E for rows in heads]
        costs = [jnp.maximum(y, 0.0) + jnp.log2(1.0 + jnp.exp2(-jnp.abs(y))) for y in ys]
        if diagonal:
            s = lax.broadcasted_iota(jnp.int32, (strip, tq), 1)
            masks = [s < lax.broadcasted_iota(jnp.int32, (strip, tq), 0) + (n % SB_STRIPS) * strip
                     for n in range(2 * SB_STRIPS)]
            costs = [jnp.where(mask, cost, 0.0) for mask, cost in zip(masks, costs)]
        else:
            masks = [None] * len(heads)
        laters = [_dot_exact_rhs(cost, suffix) for cost in costs]
        for rows, y, cost, later, mask in zip(heads, ys, costs, laters, masks):
            if diagonal:
                soft = jnp.maximum(y, 0.0) + jnp.log2(1.0 + jnp.exp2(-jnp.abs(y)))
                a = jnp.where(mask, jnp.exp2(y - soft - later), 0.0)
                acc_ref[rows, :] = _dot(a.astype(BF16), v)
                carry_ref[rows, :] = jnp.sum(cost, axis=1, keepdims=True)
            else:
                a = jnp.exp2(y - cost - later)
                carry = carry_ref[rows, :]
                acc_ref[rows, :] += _dot(a.astype(BF16), v) * jnp.exp2(-carry)
                carry_ref[rows, :] = carry + jnp.sum(cost, axis=1, keepdims=True)

    tile(i, True)

    def cond(state):
        step, least = state
        return jnp.logical_and(step < i, least < SB_UNDERFLOW_LOG2)

    def body(state):
        step, _ = state
        tile(i - 1 - step, False)
        return step + 1, jnp.min(carry_ref[...])

    lax.while_loop(cond, body, (jnp.int32(0), jnp.min(carry_ref[...])))
    acc = acc_ref[...]
    o_ref[...] = jnp.where(lane < SB_DH, acc[0:tq], acc[tq:2 * tq]).astype(o_ref.dtype)


def _sb_attention(qn, kn, vn, suffix, batch, seq):
    n, width = qn.shape
    pairs = width // LANES
    tq = SB_TILE
    nq = seq // tq
    return pl.pallas_call(
        _sb_kernel,
        grid=(batch, pairs, nq),
        in_specs=[pl.BlockSpec((tq, LANES), lambda b, p, i: (b * nq + i, p)),
                  pl.BlockSpec((seq, LANES), lambda b, p, i: (b, p)),
                  pl.BlockSpec((seq, LANES), lambda b, p, i: (b, p)),
                  pl.BlockSpec((tq, tq), lambda b, p, i: (0, 0))],
        out_specs=pl.BlockSpec((tq, LANES), lambda b, p, i: (b * nq + i, p)),
        out_shape=jax.ShapeDtypeStruct((n, width), BF16),
        scratch_shapes=[pltpu.VMEM((2 * tq, LANES), BF16),
                        pltpu.VMEM((2 * tq, LANES), F32),
                        pltpu.VMEM((2 * tq, 1), F32)],
        compiler_params=_params(3),
        name="stick_breaking_attention",
    )(qn, kn, vn, suffix)


def _ret_proj_kernel(x_ref, g_ref, sc_ref, sh_ref, w_ref, cos_ref, sin_ref, qs_ref, ks_ref,
                     q_ref, k_ref, v_ref, gate_ref):
    tm = x_ref.shape[0]
    n_heads, chunk, _ = qs_ref.shape
    w_qk = q_ref.shape[1]
    dk = w_qk // n_heads
    half = dk // 2
    groups = [slice(r0, r0 + chunk) for r0 in range(0, tm, chunk)]
    hbs = [_norm_mod(x_ref[rows, :], g_ref[...], sc_ref[0], sh_ref[0]).astype(BF16) for rows in groups]

    def rot(x, rows, scale):
        cos = cos_ref[rows, :]
        sin = sin_ref[rows, :]
        x1, x2 = x[:, :half], x[:, half:]
        return jnp.concatenate([(x1 * cos - x2 * sin) * scale, (x1 * sin + x2 * cos) * scale], axis=-1)

    for h in range(n_heads):
        cols = slice(h * dk, (h + 1) * dk)
        for rows, hb in zip(groups, hbs):
            q_ref[rows, cols] = rot(_dot(hb, w_ref[:, cols]), rows, qs_ref[h]).astype(BF16)
    for h in range(n_heads):
        cols = slice(h * dk, (h + 1) * dk)
        for rows, hb in zip(groups, hbs):
            pre = _dot(hb, w_ref[:, w_qk + h * dk:w_qk + (h + 1) * dk])
            k_ref[rows, cols] = rot(pre, rows, ks_ref[h]).astype(BF16)
    step = 512
    off = 2 * w_qk
    for ref in (v_ref, gate_ref):
        for o in range(0, ref.shape[1], step):
            for rows, hb in zip(groups, hbs):
                ref[rows, o:o + step] = _dot(hb, w_ref[:, off + o:off + o + step]).astype(ref.dtype)
        off += ref.shape[1]


def _ret_proj(x, gain, sc, sh, w, cos, sin, q_scale, k_scale, n_qk, n_v, seq):
    n, d = x.shape
    tm = ROW_TILE_PROJ
    per_b = seq // tm
    bmap = lambda i: (i // per_b, 0, 0)
    row = lambda i: (i, 0)
    const = lambda i: (0, 0)
    pos = lambda i: (i % per_b, 0)
    return pl.pallas_call(
        _ret_proj_kernel,
        grid=(n // tm,),
        in_specs=[pl.BlockSpec((tm, d), row),
                  pl.BlockSpec((1, d), const),
                  pl.BlockSpec((1, 1, d), bmap),
                  pl.BlockSpec((1, 1, d), bmap),
                  _resident(w.shape, const),
                  pl.BlockSpec((tm, cos.shape[1]), pos),
                  pl.BlockSpec((tm, cos.shape[1]), pos),
                  pl.BlockSpec(q_scale.shape, lambda i: (0, 0, 0)),
                  pl.BlockSpec(k_scale.shape, lambda i: (0, 0, 0))],
        out_specs=[pl.BlockSpec((tm, n_qk), row),
                   pl.BlockSpec((tm, n_qk), row),
                   pl.BlockSpec((tm, n_v), row),
                   pl.BlockSpec((tm, n_v), row)],
        out_shape=[jax.ShapeDtypeStruct((n, n_qk), BF16),
                   jax.ShapeDtypeStruct((n, n_qk), BF16),
                   jax.ShapeDtypeStruct((n, n_v), BF16),
                   jax.ShapeDtypeStruct((n, n_v), F32)],
        compiler_params=_params(1),
        name="retention_in_proj",
    )(x, gain, sc, sh, w, cos, sin, q_scale, k_scale)


def _ret_kernel(q_ref, k_ref, v_ref, gate_ref, decay_ref, o_ref, state_ref):
    t = pl.program_id(1)
    c = q_ref.shape[0]
    n_heads = state_ref.shape[0]
    dk = q_ref.shape[1] // n_heads
    dv = v_ref.shape[1] // n_heads

    @pl.when(t == 0)
    def _():
        state_ref[...] = jnp.zeros_like(state_ref)

    row = lax.broadcasted_iota(jnp.int32, (c, c), 0)
    col = lax.broadcasted_iota(jnp.int32, (c, c), 1)
    causal = row >= col

    for h in range(n_heads):
        q = q_ref[:, h * dk:(h + 1) * dk]
        k = k_ref[:, h * dk:(h + 1) * dk]
        v = v_ref[:, h * dv:(h + 1) * dv]
        intra = jnp.where(causal, _dot_nt(q, k), 0.0)
        state = state_ref[h]
        o = _dot(intra.astype(BF16), v) + _dot(q, state.astype(BF16))
        k_t = k.astype(F32).T.astype(BF16)
        state_ref[h] = (state + _dot(k_t, v)) * decay_ref[h]
        ms = jnp.mean(o * o, axis=-1, keepdims=True)
        o = o * lax.rsqrt(ms + EPS)
        o_ref[:, h * dv:(h + 1) * dv] = (o * _silu(gate_ref[:, h * dv:(h + 1) * dv])).astype(o_ref.dtype)


def _retention(q, k, v, gate, chunk_decay, batch, seq):
    n, w_qk = q.shape
    n_heads = chunk_decay.shape[0]
    w_v = v.shape[1]
    c = RET_CHUNK
    nt = seq // c
    rows = lambda b, t: (b * nt + t, 0)
    return pl.pallas_call(
        _ret_kernel,
        grid=(batch, nt),
        in_specs=[pl.BlockSpec((c, w_qk), rows),
                  pl.BlockSpec((c, w_qk), rows),
                  pl.BlockSpec((c, w_v), rows),
                  pl.BlockSpec((c, w_v), rows),
                  pl.BlockSpec(chunk_decay.shape, lambda b, t: (0, 0, 0))],
        out_specs=pl.BlockSpec((c, w_v), rows),
        out_shape=jax.ShapeDtypeStruct((n, w_v), BF16),
        scratch_shapes=[pltpu.VMEM((n_heads, w_qk // n_heads, w_v // n_heads), F32)],
        compiler_params=_params(2),
        name="retention_chunkwise",
    )(q, k, v, gate, chunk_decay)


def _post_kernel(*refs, n_mix):
    x_ref = refs[0]
    o_refs = refs[1:1 + n_mix]
    wm_refs = refs[1 + n_mix:1 + 2 * n_mix]
    gtm_ref, g_ref, sc_ref, sh_ref, gtf_ref, win_ref, wout_ref, out_ref, acc_ref = refs[1 + 2 * n_mix:]
    y = _dot(o_refs[0][...], wm_refs[0][...])
    for o_ref, w_ref in zip(o_refs[1:], wm_refs[1:]):
        y = y + _dot(o_ref[...], w_ref[...])
    x1 = x_ref[...] + gtm_ref[0] * y
    hb = _norm_mod(x1, g_ref[...], sc_ref[0], sh_ref[0]).astype(BF16)
    hidden = wout_ref.shape[0]
    for idx, off in enumerate(range(0, hidden, FFN_CHUNK)):
        g = _dot(hb, win_ref[:, off:off + FFN_CHUNK])
        u = _dot(hb, win_ref[:, hidden + off:hidden + off + FFN_CHUNK])
        part = _dot((_silu(g) * u).astype(BF16), wout_ref[off:off + FFN_CHUNK, :])
        if idx == 0:
            acc_ref[...] = part
        else:
            acc_ref[...] += part
    out_ref[...] = x1 + gtf_ref[0] * acc_ref[...]


def _post(x, mix_outs, mix_ws, gt_m, gain, sc, sh, gt_f, w_in, w_out, seq):
    n, d = x.shape
    tm = ROW_TILE_FFN
    per_b = seq // tm
    bmap = lambda i: (i // per_b, 0, 0)
    row = lambda i: (i, 0)
    const = lambda i: (0, 0)
    vec = pl.BlockSpec((1, 1, d), bmap)
    n_mix = len(mix_outs)
    return pl.pallas_call(
        functools.partial(_post_kernel, n_mix=n_mix),
        grid=(n // tm,),
        in_specs=([pl.BlockSpec((tm, d), row)]
                  + [pl.BlockSpec((tm, o.shape[1]), row) for o in mix_outs]
                  + [_resident(w.shape, const) for w in mix_ws]
                  + [vec, pl.BlockSpec((1, d), const), vec, vec, vec,
                     _resident(w_in.shape, const), _resident(w_out.shape, const)]),
        out_specs=pl.BlockSpec((tm, d), row),
        out_shape=jax.ShapeDtypeStruct((n, d), F32),
        scratch_shapes=[pltpu.VMEM((tm, d), F32)],
        compiler_params=_params(1),
        name="out_proj_swiglu",
    )(x, *mix_outs, *mix_ws, gt_m, gain, sc, sh, gt_f, w_in, w_out)


def kernel(x, c, ada_w, ada_b, norm_mix, norm_ffn, hyb_w_in, hyb_conv, gdn_a_log, gdn_dt_bias, gdn_norm, sb_q_norm, sb_k_norm, hyb_w_out, ret_w_in, ret_w_out, ffn_w_in, ffn_w_out):
    batch, seq, d = x.shape
    depth = ada_w.shape[0]
    gdn_heads = gdn_a_log.shape[1]
    gdn_dv = gdn_norm.shape[1]
    gdn_w = gdn_heads * gdn_dv
    sb_dh = sb_q_norm.shape[1]
    assert sb_dh == SB_DH
    sb_w = hyb_w_out.shape[1] - gdn_w
    sb_heads = sb_w // sb_dh
    ret_heads = 4
    ret_qk = ret_w_in.shape[2] // 6
    ret_v = 2 * ret_qk
    ret_dk = ret_qk // ret_heads

    xf = x.reshape(batch * seq, d)
    mod = _modulation(c, ada_w, ada_b)

    head_mean = jnp.kron(jnp.eye(MXU_DIM // sb_dh, dtype=F32), jnp.full((sb_dh, sb_dh), 1.0 / sb_dh, F32)).astype(BF16)
    idx = jnp.arange(SB_TILE)
    suffix = (idx[:, None] > idx[None, :]).astype(BF16)
    pos = jnp.arange(seq, dtype=F32)
    inv = 1.0 / (ROPE_BASE ** (jnp.arange(0, ret_dk, 2, dtype=F32) / ret_dk))
    ang = pos[:, None] * inv[None, :]
    cos, sin = jnp.cos(ang), jnp.sin(ang)
    log_gamma = jnp.log1p(-jnp.exp2(-5.0 - jnp.arange(ret_heads, dtype=F32)))
    steps = jnp.arange(1, RET_CHUNK + 1, dtype=F32)
    ret_q_scale = jnp.broadcast_to(jnp.exp(log_gamma[:, None] * steps[None, :])[:, :, None],
                                   (ret_heads, RET_CHUNK, ret_dk // 2))
    ret_k_scale = jnp.broadcast_to((jnp.exp(-log_gamma[:, None] * steps[None, :]) * ret_dk ** -0.5)[:, :, None],
                                   (ret_heads, RET_CHUNK, ret_dk // 2))
    ret_chunk_decay = jnp.broadcast_to(jnp.exp(log_gamma * RET_CHUNK)[:, None, None], (ret_heads, 1, ret_v // ret_heads))

    for l in range(depth):
        sh_m, sc_m, gt_m, sh_f, sc_f, gt_f = [m.reshape(batch, 1, d) for m in jnp.split(mod[l], 6, axis=-1)]
        i = l // 2
        gain_m = norm_mix[l].reshape(1, d)
        if l % 2 == 0:
            w_in = hyb_w_in[i]
            n_main = 4 * gdn_w
            n_ab = 2 * gdn_heads
            ab_cols = jnp.zeros((d, LANES), F32).at[:, :n_ab].set(w_in[:, n_main:n_main + n_ab])
            w_re = jnp.concatenate([w_in[:, :n_main], ab_cols, w_in[:, n_main + n_ab:]], axis=1).astype(BF16)
            gdn_qkv, gdn_gate, ab, qn, kn, vn = _hyb_proj(
                xf, gain_m, sc_m, sh_m, w_re, hyb_conv[i], head_mean,
                jnp.tile(sb_q_norm[i], sb_heads).reshape(1, sb_w),
                jnp.tile(sb_k_norm[i], sb_heads).reshape(1, sb_w), seq, gdn_dv)
            o_a = _gdn(gdn_qkv, gdn_gate, ab, gdn_a_log[i], gdn_dt_bias[i], gdn_norm[i], batch, seq)
            o_b = _sb_attention(qn, kn, vn, suffix, batch, seq)
            w_out = hyb_w_out[i].astype(BF16)
            mix_outs = [o_a, o_b]
            mix_ws = [w_out[:gdn_w], w_out[gdn_w:]]
        else:
            rq, rk, rv, gate = _ret_proj(xf, gain_m, sc_m, sh_m, ret_w_in[i].astype(BF16), cos, sin,
                                         ret_q_scale, ret_k_scale, ret_qk, ret_v, seq)
            o_r = _retention(rq, rk, rv, gate, ret_chunk_decay, batch, seq)
            mix_outs = [o_r]
            mix_ws = [ret_w_out[i].astype(BF16)]
        xf = _post(xf, mix_outs, mix_ws, gt_m, norm_ffn[l].reshape(1, d), sc_f, sh_f, gt_f,
                   ffn_w_in[l].astype(BF16), ffn_w_out[l].astype(BF16), seq)
    return xf.reshape(batch, seq, d)
```

```python
import functools

import jax
import jax.numpy as jnp
from jax import lax
from jax.experimental import pallas as pl
from jax.experimental.pallas import tpu as pltpu

F32 = jnp.float32
BF16 = jnp.bfloat16
EPS = 1e-6
ROPE_BASE = 10000.0

LANES = 128
MXU_DIM = 256
VMEM_LIMIT = 56 * 1024 * 1024

GDN_CONV = 4
GDN_BLOCK = 128
GDN_STEP = 512
SB_TILE = 256
SB_DH = 64
SB_STRIPS = 2
LOG2_E = 1.4426950408889634
SB_UNDERFLOW_LOG2 = 180.0
RET_CHUNK = 256
ROW_TILE_PROJ = 512
PROJ_GROUP = 256
ROW_TILE_FFN = 512
FFN_CHUNK = 256
CONV_PAD = 8


def _dot(a, b):
    return jnp.dot(a, b, preferred_element_type=F32)


def _dot_nt(a, b):
    return lax.dot_general(a, b, (((1,), (1,)), ((), ())), preferred_element_type=F32)


def _split2(a):
    hi = a.astype(BF16)
    lo = (a - hi.astype(F32)).astype(BF16)
    return hi, lo


def _dot3(a, b):
    ah, al = _split2(a)
    bh, bl = _split2(b)
    n = b.shape[1]
    if 2 * n <= MXU_DIM:
        both = _dot(ah, jnp.concatenate([bh, bl], axis=1))
        return (both[:, :n] + both[:, n:]) + _dot(al, bh)
    return _dot(ah, bh) + (_dot(ah, bl) + _dot(al, bh))


def _dot_exact_rhs(a, m):
    ah, al = _split2(a)
    return _dot(ah, m) + _dot(al, m)


def _dot_exact_lhs(m, a):
    ah, al = _split2(a)
    n = a.shape[1]
    if 2 * n <= MXU_DIM:
        both = _dot(m, jnp.concatenate([ah, al], axis=1))
        return both[:, :n] + both[:, n:]
    return _dot(m, ah) + _dot(m, al)


def _silu(x):
    return x * jax.nn.sigmoid(x)


def _softplus(x):
    return jnp.maximum(x, 0.0) + jnp.log1p(jnp.exp(-jnp.abs(x)))


def _params(n_axes):
    return pltpu.CompilerParams(dimension_semantics=("arbitrary",) * n_axes,
                                vmem_limit_bytes=VMEM_LIMIT)


def _resident(shape, index_map):
    return pl.BlockSpec(shape, index_map, pipeline_mode=pl.Buffered(1))


def _mod_kernel(c_ref, w_ref, b_ref, o_ref):
    c = c_ref[...]
    o_ref[0] = _dot(_silu(c).astype(BF16), w_ref[0].astype(BF16)) + b_ref[0]


def _modulation(c, ada_w, ada_b):
    depth, d, n = ada_w.shape
    b = c.shape[0]
    rows = 8
    c_pad = jnp.zeros((rows, d), F32).at[:b].set(c)
    tn = 1024
    out = pl.pallas_call(
        _mod_kernel,
        grid=(depth, n // tn),
        in_specs=[pl.BlockSpec((rows, d), lambda l, j: (0, 0)),
                  pl.BlockSpec((1, d, tn), lambda l, j: (l, 0, j)),
                  pl.BlockSpec((1, 1, tn), lambda l, j: (l, 0, j))],
        out_specs=pl.BlockSpec((1, rows, tn), lambda l, j: (l, 0, j)),
        out_shape=jax.ShapeDtypeStruct((depth, rows, n), F32),
        compiler_params=_params(2),
        name="adaln_modulation",
    )(c_pad, ada_w, ada_b.reshape(depth, 1, n))
    return out[:, :b]


def _norm_mod(x, gain, sc, sh):
    ms = jnp.mean(x * x, axis=-1, keepdims=True)
    y = x * lax.rsqrt(ms + EPS)
    y = y * gain
    return y * (1.0 + sc) + sh


def _hyb_proj_kernel(x_ref, g_ref, sc_ref, sh_ref, w_ref, cw_ref, gm_ref, qg_ref, kg_ref,
                     qkv_ref, gate_ref, ab_ref, q_ref, k_ref, v_ref, ext_ref, *, per_batch, dk):
    i = pl.program_id(0)
    tm = x_ref.shape[0]
    n_conv = qkv_ref.shape[1]
    n_gate = gate_ref.shape[1]
    w_sb = q_ref.shape[1]
    groups = [slice(r0, r0 + PROJ_GROUP) for r0 in range(0, tm, PROJ_GROUP)]
    hbs = [_norm_mod(x_ref[rows, :], g_ref[...], sc_ref[0], sh_ref[0]).astype(BF16) for rows in groups]

    @pl.when(i % per_batch == 0)
    def _():
        ext_ref[0:CONV_PAD, :] = jnp.zeros((CONV_PAD, n_conv), F32)

    def l2n(y):
        return y * lax.rsqrt(jnp.sum(y * y, axis=-1, keepdims=True) + EPS)

    for c0 in range(0, n_conv, MXU_DIM):
        for rows, hb in zip(groups, hbs):
            ext_ref[CONV_PAD + rows.start:CONV_PAD + rows.stop, c0:c0 + MXU_DIM] = _dot(hb, w_ref[:, c0:c0 + MXU_DIM])

    def conv_chunk(c0, rows):
        cols = slice(c0, c0 + MXU_DIM)
        w = cw_ref[:, cols]
        y = ext_ref[CONV_PAD + rows.start:CONV_PAD + rows.stop, cols] * w[GDN_CONV - 1:GDN_CONV, :]
        for j in range(GDN_CONV - 1):
            shift = GDN_CONV - 1 - j
            y = y + ext_ref[CONV_PAD + rows.start - shift:CONV_PAD + rows.stop - shift, cols] * w[j:j + 1, :]
        y = _silu(y)
        if c0 < 2 * n_gate:
            y = jnp.concatenate([l2n(y[:, h0:h0 + dk]) for h0 in range(0, MXU_DIM, dk)], axis=1)
            if c0 < n_gate:
                y = y * (dk ** -0.5)
        qkv_ref[rows, cols] = y

    def head_rms(xf, gain):
        sq = xf * xf
        ms = jnp.concatenate([_dot_exact_rhs(sq[:, c0:c0 + MXU_DIM], gm_ref[...])
                              for c0 in range(0, xf.shape[1], MXU_DIM)], axis=1)
        return xf * lax.rsqrt(ms + EPS) * gain

    off_gate = n_conv
    off_ab = off_gate + n_gate
    off_q = off_ab + LANES
    off_k = off_q + w_sb
    off_v = off_k + w_sb
    conv_starts = list(range(0, n_conv, MXU_DIM))
    for rows, hb in zip(groups, hbs):
        gate_ref[rows, :] = _dot(hb, w_ref[:, off_gate:off_gate + n_gate])
    for rows, hb in zip(groups, hbs):
        sq = _dot(hb, w_ref[:, off_q:off_q + w_sb])
        q_ref[rows, :] = (head_rms(sq, qg_ref[...]) * (SB_DH ** -0.5)).astype(BF16)
    for rows, hb in zip(groups, hbs):
        sk = _dot(hb, w_ref[:, off_k:off_k + w_sb])
        k_ref[rows, :] = head_rms(sk, kg_ref[...]).astype(BF16)
    for rows, hb in zip(groups, hbs):
        v_ref[rows, :] = _dot(hb, w_ref[:, off_v:off_v + w_sb]).astype(BF16)
        ab_ref[rows, :] = _dot(hb, w_ref[:, off_ab:off_ab + LANES])
    for c0 in conv_starts:
        for rows in groups:
            conv_chunk(c0, rows)
    ext_ref[0:CONV_PAD, :] = ext_ref[tm:tm + CONV_PAD, :]


def _hyb_proj(x, gain, sc, sh, w, conv_w, gmat, qg, kg, seq, gdn_dk):
    n, d = x.shape
    tm = ROW_TILE_PROJ
    per_b = seq // tm
    w_sb = qg.shape[1]
    n_conv = conv_w.shape[1]
    n_gate = w.shape[1] - n_conv - LANES - 3 * w_sb
    bmap = lambda i: (i // per_b, 0, 0)
    row = lambda i: (i, 0)
    const = lambda i: (0, 0)
    return pl.pallas_call(
        functools.partial(_hyb_proj_kernel, per_batch=per_b, dk=gdn_dk),
        grid=(n // tm,),
        in_specs=[pl.BlockSpec((tm, d), row),
                  pl.BlockSpec((1, d), const),
                  pl.BlockSpec((1, 1, d), bmap),
                  pl.BlockSpec((1, 1, d), bmap),
                  _resident(w.shape, const),
                  pl.BlockSpec(conv_w.shape, const),
                  _resident(gmat.shape, const),
                  pl.BlockSpec((1, w_sb), const),
                  pl.BlockSpec((1, w_sb), const)],
        out_specs=[pl.BlockSpec((tm, n_conv), row),
                   pl.BlockSpec((tm, n_gate), row),
                   pl.BlockSpec((tm, LANES), row),
                   pl.BlockSpec((tm, w_sb), row),
                   pl.BlockSpec((tm, w_sb), row),
                   pl.BlockSpec((tm, w_sb), row)],
        out_shape=[jax.ShapeDtypeStruct((n, n_conv), F32),
                   jax.ShapeDtypeStruct((n, n_gate), F32),
                   jax.ShapeDtypeStruct((n, LANES), F32),
                   jax.ShapeDtypeStruct((n, w_sb), BF16),
                   jax.ShapeDtypeStruct((n, w_sb), BF16),
                   jax.ShapeDtypeStruct((n, w_sb), BF16)],
        scratch_shapes=[pltpu.VMEM((CONV_PAD + tm, n_conv), F32)],
        compiler_params=_params(1),
        name="hybrid_in_proj",
    )(x, gain, sc, sh, w, conv_w, gmat, qg, kg)


def _dot3s(a, b):
    ah, al = a
    bh, bl = b
    n = bh.shape[1]
    both = _dot(ah, jnp.concatenate([bh, bl], axis=1))
    return (both[:, :n] + both[:, n:]) + _dot(al, bh)


def _lower_child_rows(x, size):
    return jnp.concatenate([x[r + size:r + 2 * size] for r in range(0, x.shape[0], 2 * size)], axis=0)


def _unit_lower_inverse_all(lowers, eye, row, col):
    bits = 3
    same = (row >> bits) == (col >> bits)
    diags = [jnp.where(same, l, 0.0) for l in lowers]
    d_s = [_split2(d) for d in diags]
    x0_s = [_split2(eye - d) for d in diags]
    p2_s = [_split2(_dot3s(d, d)) for d in d_s]
    xs = [(eye - d) + _dot3s(x, p) for d, x, p in zip(diags, x0_s, p2_s)]
    p4_s = [_split2(_dot3s(p, p)) for p in p2_s]
    xs = [x + _dot3s(_split2(x), p) for x, p in zip(xs, p4_s)]
    n = lowers[0].shape[0]
    size = 1 << bits
    while size < n:
        parent = (row >> (bits + 1)) == (col >> (bits + 1))
        child = (row >> bits) == (col >> bits)
        off_s = [_split2(jnp.where(parent, jnp.where(child, 0.0, l), 0.0)) for l in lowers]
        x_s = [_split2(x) for x in xs]
        ys = [_dot3s(_split2(_lower_child_rows(x, size)), o) for x, o in zip(xs, off_s)]
        zs = [_dot3s(_split2(y), x) for y, x in zip(ys, x_s)]
        new_xs = []
        for x, z in zip(xs, zs):
            pieces = []
            for k, r in enumerate(range(0, n, 2 * size)):
                pieces.append(x[r:r + size])
                pieces.append(x[r + size:r + 2 * size] - z[k * size:(k + 1) * size])
            new_xs.append(jnp.concatenate(pieces, axis=0))
        xs = new_xs
        bits += 1
        size *= 2
    return xs


def _gdn_kernel(q_ref, k_ref, v_ref, gate_ref, ab_ref, alog_ref, dtb_ref, gn_ref, o_ref, state_ref):
    t = pl.program_id(1)
    tb, width = q_ref.shape
    c = GDN_BLOCK
    n_heads = state_ref.shape[0]
    dk = width // n_heads
    n_chunks = tb // c

    @pl.when(t == 0)
    def _():
        state_ref[...] = jnp.zeros_like(state_ref)

    lane = lax.broadcasted_iota(jnp.int32, (tb, LANES), 1)
    ab = ab_ref[...]
    g_lanes = -jnp.exp(alog_ref[...]) * _softplus(ab + dtb_ref[...])
    beta_lanes = jax.nn.sigmoid(ab)

    def lane_column(x, idx):
        return jnp.broadcast_to(jnp.sum(jnp.where(lane == idx, x, 0.0), axis=1, keepdims=True), (tb, LANES))

    row = lax.broadcasted_iota(jnp.int32, (c, c), 0)
    col = lax.broadcasted_iota(jnp.int32, (c, c), 1)
    incl = row >= col
    strict = row > col
    eye = jnp.where(row == col, 1.0, 0.0).astype(F32)
    tri = jnp.where(incl, 1.0, 0.0).astype(BF16)

    items = []
    for h in range(n_heads):
        g_h = lane_column(g_lanes, h)
        beta_h = lane_column(beta_lanes, h + n_heads)
        cols = slice(h * dk, (h + 1) * dk)
        for n in range(n_chunks):
            rows = slice(n * c, (n + 1) * c)
            items.append(dict(q=q_ref[rows, cols], k=k_ref[rows, cols], v=v_ref[rows, cols],
                              beta=beta_h[rows], g=g_h[rows]))

    def prepare(group):
        for it in group:
            it["gcol"] = _dot_exact_lhs(tri, it["g"])
        for it in group:
            gcol = it["gcol"]
            it["dec"] = jnp.exp(jnp.where(incl, gcol - gcol.T, 0.0))
            it["eg"] = jnp.exp(gcol)
            it["g_last"] = gcol[c - 1:c, :]
            it["kb"] = it["k"] * it["beta"]
            it["kk"] = it["k"].astype(BF16)
        for it in group:
            it["lower"] = jnp.where(strict, _dot_nt(it["kb"].astype(BF16), it["kk"]) * it["dec"], 0.0)
        tinvs = _unit_lower_inverse_all([it["lower"] for it in group], eye, row, col)
        for it, tinv in zip(group, tinvs):
            rhs = jnp.concatenate([it["v"] * it["beta"], it["kb"] * it["eg"]], axis=1)
            uw = _dot3(tinv, rhs)
            it["u"] = uw[:, :dk]
            a = jnp.where(incl, _dot_nt(it["q"].astype(BF16), it["kk"]) * it["dec"], 0.0)
            q_dec = it["q"] * it["eg"]
            k_dec = it["k"] * jnp.exp(it["g_last"] - it["gcol"])
            it["wq"] = jnp.concatenate([uw[:, dk:], q_dec], axis=0).astype(BF16)
            it["ak"] = jnp.concatenate([a, k_dec.T], axis=0).astype(BF16)
            it["decay"] = jnp.exp(it["g_last"])

    early = n_chunks // 2
    prepare([it for i, it in enumerate(items) if i % n_chunks < early])
    prepare([it for i, it in enumerate(items) if i % n_chunks >= early])

    gn = gn_ref[...]
    states = [state_ref[h] for h in range(n_heads)]
    for n in range(n_chunks):
        rows = slice(n * c, (n + 1) * c)
        for h in range(n_heads):
            it = items[h * n_chunks + n]
            cols = slice(h * dk, (h + 1) * dk)
            ws_qs = _dot(it["wq"], states[h].astype(BF16))
            v_new = it["u"] - ws_qs[:c]
            av_kv = _dot(it["ak"], v_new.astype(BF16))
            o = ws_qs[c:] + av_kv[:c]
            states[h] = states[h] * it["decay"] + av_kv[c:]
            ms = jnp.mean(o * o, axis=-1, keepdims=True)
            o = o * lax.rsqrt(ms + EPS) * gn
            o_ref[rows, cols] = (o * _silu(gate_ref[rows, cols])).astype(o_ref.dtype)
    for h in range(n_heads):
        state_ref[h] = states[h]


def _gdn(qkv, gate, ab, a_log, dt_bias, gdn_norm, batch, seq):
    n, width = gate.shape
    n_heads = a_log.shape[0]
    dk = width // n_heads
    tb = GDN_STEP
    nt = seq // tb
    rows = lambda b, t: b * nt + t
    pad = lambda x: jnp.zeros((1, LANES), F32).at[0, :n_heads].set(x)
    col_spec = lambda k: pl.BlockSpec((tb, width), lambda b, t: (rows(b, t), k))
    const = lambda b, t: (0, 0)
    return pl.pallas_call(
        _gdn_kernel,
        grid=(batch, nt),
        in_specs=[col_spec(0), col_spec(1), col_spec(2), col_spec(0),
                  pl.BlockSpec((tb, LANES), lambda b, t: (rows(b, t), 0)),
                  pl.BlockSpec((1, LANES), const), pl.BlockSpec((1, LANES), const),
                  pl.BlockSpec((1, dk), const)],
        out_specs=pl.BlockSpec((tb, width), lambda b, t: (rows(b, t), 0)),
        out_shape=jax.ShapeDtypeStruct((n, width), BF16),
        scratch_shapes=[pltpu.VMEM((n_heads, dk, dk), F32)],
        compiler_params=_params(2),
        name="gated_delta_rule",
    )(qkv, qkv, qkv, gate, ab, pad(a_log), pad(dt_bias), gdn_norm.reshape(1, dk))


def _sb_kernel(q_ref, k_ref, v_ref, m_ref, o_ref, q2_ref, acc_ref, carry_ref):
    i = pl.program_id(2)
    tq = q_ref.shape[0]
    lane = lax.broadcasted_iota(jnp.int32, (tq, LANES), 1)
    q = q_ref[...]
    zero = jnp.zeros_like(q)
    q2_ref[0:tq, :] = jnp.where(lane < SB_DH, q, zero)
    q2_ref[tq:2 * tq, :] = jnp.where(lane >= SB_DH, q, zero)
    suffix = m_ref[...]
    strip = tq // SB_STRIPS
    strips = [slice(n * strip, (n + 1) * strip) for n in range(2 * SB_STRIPS)]

    def weigh(tiles):
        work = []
        for j, diagonal in tiles:
            start = pl.multiple_of(j * tq, tq)
            k = k_ref[pl.ds(start, tq), :]
            v = v_ref[pl.ds(start, tq), :]
            for n, rows in enumerate(strips):
                work.append(dict(j=j, n=n, rows=rows, k=k, v=v, diagonal=diagonal))
        for w in work:
            w["y"] = _dot_nt(q2_ref[w["rows"], :], w["k"]) * LOG2_E
        for w in work:
            y = w["y"]
            w["soft"] = jnp.maximum(y, 0.0) + jnp.log2(1.0 + jnp.exp2(-jnp.abs(y)))
            if w["diagonal"]:
                s = lax.broadcasted_iota(jnp.int32, (strip, tq), 1)
                r = lax.broadcasted_iota(jnp.int32, (strip, tq), 0) + (w["n"] % SB_STRIPS) * strip
                w["mask"] = s < r
                w["cost"] = jnp.where(w["mask"], w["soft"], 0.0)
            else:
                w["cost"] = w["soft"]
        for w in work:
            w["later"] = _dot_exact_rhs(w["cost"], suffix)
        for w in work:
            a = jnp.exp2(w["y"] - w["soft"] - w["later"])
            if w["diagonal"]:
                a = jnp.where(w["mask"], a, 0.0)
            w["part"] = _dot(a.astype(BF16), w["v"])
            w["total"] = jnp.sum(w["cost"], axis=1, keepdims=True)
        per_tile = len(strips)
        return [work[t * per_tile:(t + 1) * per_tile] for t in range(len(tiles))]

    @pl.when(i == 0)
    def _():
        (diag,) = weigh([(i, True)])
        for w in diag:
            acc_ref[w["rows"], :] = w["part"]
            carry_ref[w["rows"], :] = w["total"]

    @pl.when(i > 0)
    def _():
        diag, prev = weigh([(i, True), (i - 1, False)])
        for d, p in zip(diag, prev):
            acc_ref[d["rows"], :] = d["part"] + p["part"] * jnp.exp2(-d["total"])
            carry_ref[d["rows"], :] = d["total"] + p["total"]

    def cond(state):
        step, least = state
        return jnp.logical_and(step < i - 1, least < SB_UNDERFLOW_LOG2)

    def body(state):
        step, _ = state
        (tile,) = weigh([(i - 2 - step, False)])
        for w in tile:
            carry = carry_ref[w["rows"], :]
            acc_ref[w["rows"], :] += w["part"] * jnp.exp2(-carry)
            carry_ref[w["rows"], :] = carry + w["total"]
        return step + 1, jnp.min(carry_ref[...])

    lax.while_loop(cond, body, (jnp.int32(0), jnp.min(carry_ref[...])))
    acc = acc_ref[...]
    o_ref[...] = jnp.where(lane < SB_DH, acc[0:tq], acc[tq:2 * tq]).astype(o_ref.dtype)


def _sb_attention(qn, kn, vn, suffix, batch, seq):
    n, width = qn.shape
    pairs = width // LANES
    tq = SB_TILE
    nq = seq // tq
    return pl.pallas_call(
        _sb_kernel,
        grid=(batch, pairs, nq),
        in_specs=[pl.BlockSpec((tq, LANES), lambda b, p, i: (b * nq + i, p)),
                  pl.BlockSpec((seq, LANES), lambda b, p, i: (b, p)),
                  pl.BlockSpec((seq, LANES), lambda b, p, i: (b, p)),
                  pl.BlockSpec((tq, tq), lambda b, p, i: (0, 0))],
        out_specs=pl.BlockSpec((tq, LANES), lambda b, p, i: (b * nq + i, p)),
        out_shape=jax.ShapeDtypeStruct((n, width), BF16),
        scratch_shapes=[pltpu.VMEM((2 * tq, LANES), BF16),
                        pltpu.VMEM((2 * tq, LANES), F32),
                        pltpu.VMEM((2 * tq, 1), F32)],
        compiler_params=_params(3),
        name="stick_breaking_attention",
    )(qn, kn, vn, suffix)


def _ret_proj_kernel(x_ref, g_ref, sc_ref, sh_ref, w_ref, cos_ref, sin_ref, qs_ref, ks_ref,
                     q_ref, k_ref, v_ref, gate_ref):
    tm = x_ref.shape[0]
    n_heads, chunk, _ = qs_ref.shape
    w_qk = q_ref.shape[1]
    dk = w_qk // n_heads
    half = dk // 2
    groups = [slice(r0, r0 + chunk) for r0 in range(0, tm, chunk)]
    hbs = [_norm_mod(x_ref[rows, :], g_ref[...], sc_ref[0], sh_ref[0]).astype(BF16) for rows in groups]

    def rot(x, rows, scale):
        cos = cos_ref[rows, :]
        sin = sin_ref[rows, :]
        x1, x2 = x[:, :half], x[:, half:]
        return jnp.concatenate([(x1 * cos - x2 * sin) * scale, (x1 * sin + x2 * cos) * scale], axis=-1)

    for h in range(n_heads):
        cols = slice(h * dk, (h + 1) * dk)
        for rows, hb in zip(groups, hbs):
            q_ref[rows, cols] = rot(_dot(hb, w_ref[:, cols]), rows, qs_ref[h]).astype(BF16)
    for h in range(n_heads):
        cols = slice(h * dk, (h + 1) * dk)
        for rows, hb in zip(groups, hbs):
            pre = _dot(hb, w_ref[:, w_qk + h * dk:w_qk + (h + 1) * dk])
            k_ref[rows, cols] = rot(pre, rows, ks_ref[h]).astype(BF16)
    step = 512
    off = 2 * w_qk
    for ref in (v_ref, gate_ref):
        for o in range(0, ref.shape[1], step):
            for rows, hb in zip(groups, hbs):
                ref[rows, o:o + step] = _dot(hb, w_ref[:, off + o:off + o + step]).astype(ref.dtype)
        off += ref.shape[1]


def _ret_proj(x, gain, sc, sh, w, cos, sin, q_scale, k_scale, n_qk, n_v, seq):
    n, d = x.shape
    tm = ROW_TILE_PROJ
    per_b = seq // tm
    bmap = lambda i: (i // per_b, 0, 0)
    row = lambda i: (i, 0)
    const = lambda i: (0, 0)
    pos = lambda i: (i % per_b, 0)
    return pl.pallas_call(
        _ret_proj_kernel,
        grid=(n // tm,),
        in_specs=[pl.BlockSpec((tm, d), row),
                  pl.BlockSpec((1, d), const),
                  pl.BlockSpec((1, 1, d), bmap),
                  pl.BlockSpec((1, 1, d), bmap),
                  _resident(w.shape, const),
                  pl.BlockSpec((tm, cos.shape[1]), pos),
                  pl.BlockSpec((tm, cos.shape[1]), pos),
                  pl.BlockSpec(q_scale.shape, lambda i: (0, 0, 0)),
                  pl.BlockSpec(k_scale.shape, lambda i: (0, 0, 0))],
        out_specs=[pl.BlockSpec((tm, n_qk), row),
                   pl.BlockSpec((tm, n_qk), row),
                   pl.BlockSpec((tm, n_v), row),
                   pl.BlockSpec((tm, n_v), row)],
        out_shape=[jax.ShapeDtypeStruct((n, n_qk), BF16),
                   jax.ShapeDtypeStruct((n, n_qk), BF16),
                   jax.ShapeDtypeStruct((n, n_v), BF16),
                   jax.ShapeDtypeStruct((n, n_v), F32)],
        compiler_params=_params(1),
        name="retention_in_proj",
    )(x, gain, sc, sh, w, cos, sin, q_scale, k_scale)


def _ret_kernel(q_ref, k_ref, v_ref, gate_ref, decay_ref, o_ref, state_ref):
    t = pl.program_id(1)
    c = q_ref.shape[0]
    n_heads = state_ref.shape[0]
    dk = q_ref.shape[1] // n_heads
    dv = v_ref.shape[1] // n_heads

    @pl.when(t == 0)
    def _():
        state_ref[...] = jnp.zeros_like(state_ref)

    row = lax.broadcasted_iota(jnp.int32, (c, c), 0)
    col = lax.broadcasted_iota(jnp.int32, (c, c), 1)
    causal = row >= col

    for h in range(n_heads):
        q = q_ref[:, h * dk:(h + 1) * dk]
        k = k_ref[:, h * dk:(h + 1) * dk]
        v = v_ref[:, h * dv:(h + 1) * dv]
        intra = jnp.where(causal, _dot_nt(q, k), 0.0)
        state = state_ref[h]
        o = _dot(intra.astype(BF16), v) + _dot(q, state.astype(BF16))
        k_t = k.astype(F32).T.astype(BF16)
        state_ref[h] = (state + _dot(k_t, v)) * decay_ref[h]
        ms = jnp.mean(o * o, axis=-1, keepdims=True)
        o = o * lax.rsqrt(ms + EPS)
        o_ref[:, h * dv:(h + 1) * dv] = (o * _silu(gate_ref[:, h * dv:(h + 1) * dv])).astype(o_ref.dtype)


def _retention(q, k, v, gate, chunk_decay, batch, seq):
    n, w_qk = q.shape
    n_heads = chunk_decay.shape[0]
    w_v = v.shape[1]
    c = RET_CHUNK
    nt = seq // c
    rows = lambda b, t: (b * nt + t, 0)
    return pl.pallas_call(
        _ret_kernel,
        grid=(batch, nt),
        in_specs=[pl.BlockSpec((c, w_qk), rows),
                  pl.BlockSpec((c, w_qk), rows),
                  pl.BlockSpec((c, w_v), rows),
                  pl.BlockSpec((c, w_v), rows),
                  pl.BlockSpec(chunk_decay.shape, lambda b, t: (0, 0, 0))],
        out_specs=pl.BlockSpec((c, w_v), rows),
        out_shape=jax.ShapeDtypeStruct((n, w_v), BF16),
        scratch_shapes=[pltpu.VMEM((n_heads, w_qk // n_heads, w_v // n_heads), F32)],
        compiler_params=_params(2),
        name="retention_chunkwise",
    )(q, k, v, gate, chunk_decay)


def _post_kernel(*refs, n_mix):
    x_ref = refs[0]
    o_refs = refs[1:1 + n_mix]
    wm_refs = refs[1 + n_mix:1 + 2 * n_mix]
    gtm_ref, g_ref, sc_ref, sh_ref, gtf_ref, win_ref, wout_ref, out_ref, acc_ref = refs[1 + 2 * n_mix:]
    y = _dot(o_refs[0][...], wm_refs[0][...])
    for o_ref, w_ref in zip(o_refs[1:], wm_refs[1:]):
        y = y + _dot(o_ref[...], w_ref[...])
    x1 = x_ref[...] + gtm_ref[0] * y
    hb = _norm_mod(x1, g_ref[...], sc_ref[0], sh_ref[0]).astype(BF16)
    hidden = wout_ref.shape[0]
    for idx, off in enumerate(range(0, hidden, FFN_CHUNK)):
        g = _dot(hb, win_ref[:, off:off + FFN_CHUNK])
        u = _dot(hb, win_ref[:, hidden + off:hidden + off + FFN_CHUNK])
        part = _dot((_silu(g) * u).astype(BF16), wout_ref[off:off + FFN_CHUNK, :])
        if idx == 0:
            acc_ref[...] = part
        else:
            acc_ref[...] += part
    out_ref[...] = x1 + gtf_ref[0] * acc_ref[...]


def _post(x, mix_outs, mix_ws, gt_m, gain, sc, sh, gt_f, w_in, w_out, seq):
    n, d = x.shape
    tm = ROW_TILE_FFN
    per_b = seq // tm
    bmap = lambda i: (i // per_b, 0, 0)
    row = lambda i: (i, 0)
    const = lambda i: (0, 0)
    vec = pl.BlockSpec((1, 1, d), bmap)
    n_mix = len(mix_outs)
    return pl.pallas_call(
        functools.partial(_post_kernel, n_mix=n_mix),
        grid=(n // tm,),
        in_specs=([pl.BlockSpec((tm, d), row)]
                  + [pl.BlockSpec((tm, o.shape[1]), row) for o in mix_outs]
                  + [_resident(w.shape, const) for w in mix_ws]
                  + [vec, pl.BlockSpec((1, d), const), vec, vec, vec,
                     _resident(w_in.shape, const), _resident(w_out.shape, const)]),
        out_specs=pl.BlockSpec((tm, d), row),
        out_shape=jax.ShapeDtypeStruct((n, d), F32),
        scratch_shapes=[pltpu.VMEM((tm, d), F32)],
        compiler_params=_params(1),
        name="out_proj_swiglu",
    )(x, *mix_outs, *mix_ws, gt_m, gain, sc, sh, gt_f, w_in, w_out)


def kernel(x, c, ada_w, ada_b, norm_mix, norm_ffn, hyb_w_in, hyb_conv, gdn_a_log, gdn_dt_bias, gdn_norm, sb_q_norm, sb_k_norm, hyb_w_out, ret_w_in, ret_w_out, ffn_w_in, ffn_w_out):
    batch, seq, d = x.shape
    depth = ada_w.shape[0]
    gdn_heads = gdn_a_log.shape[1]
    gdn_dv = gdn_norm.shape[1]
    gdn_w = gdn_heads * gdn_dv
    sb_dh = sb_q_norm.shape[1]
    assert sb_dh == SB_DH
    sb_w = hyb_w_out.shape[1] - gdn_w
    sb_heads = sb_w // sb_dh
    ret_heads = 4
    ret_qk = ret_w_in.shape[2] // 6
    ret_v = 2 * ret_qk
    ret_dk = ret_qk // ret_heads

    xf = x.reshape(batch * seq, d)
    mod = _modulation(c, ada_w, ada_b)

    head_mean = jnp.kron(jnp.eye(MXU_DIM // sb_dh, dtype=F32), jnp.full((sb_dh, sb_dh), 1.0 / sb_dh, F32)).astype(BF16)
    idx = jnp.arange(SB_TILE)
    suffix = (idx[:, None] > idx[None, :]).astype(BF16)
    pos = jnp.arange(seq, dtype=F32)
    inv = 1.0 / (ROPE_BASE ** (jnp.arange(0, ret_dk, 2, dtype=F32) / ret_dk))
    ang = pos[:, None] * inv[None, :]
    cos, sin = jnp.cos(ang), jnp.sin(ang)
    log_gamma = jnp.log1p(-jnp.exp2(-5.0 - jnp.arange(ret_heads, dtype=F32)))
    steps = jnp.arange(1, RET_CHUNK + 1, dtype=F32)
    ret_q_scale = jnp.broadcast_to(jnp.exp(log_gamma[:, None] * steps[None, :])[:, :, None],
                                   (ret_heads, RET_CHUNK, ret_dk // 2))
    ret_k_scale = jnp.broadcast_to((jnp.exp(-log_gamma[:, None] * steps[None, :]) * ret_dk ** -0.5)[:, :, None],
                                   (ret_heads, RET_CHUNK, ret_dk // 2))
    ret_chunk_decay = jnp.broadcast_to(jnp.exp(log_gamma * RET_CHUNK)[:, None, None], (ret_heads, 1, ret_v // ret_heads))

    for l in range(depth):
        sh_m, sc_m, gt_m, sh_f, sc_f, gt_f = [m.reshape(batch, 1, d) for m in jnp.split(mod[l], 6, axis=-1)]
        i = l // 2
        gain_m = norm_mix[l].reshape(1, d)
        if l % 2 == 0:
            w_in = hyb_w_in[i]
            n_main = 4 * gdn_w
            n_ab = 2 * gdn_heads
            ab_cols = jnp.zeros((d, LANES), F32).at[:, :n_ab].set(w_in[:, n_main:n_main + n_ab])
            w_re = jnp.concatenate([w_in[:, :n_main], ab_cols, w_in[:, n_main + n_ab:]], axis=1).astype(BF16)
            gdn_qkv, gdn_gate, ab, qn, kn, vn = _hyb_proj(
                xf, gain_m, sc_m, sh_m, w_re, hyb_conv[i], head_mean,
                jnp.tile(sb_q_norm[i], sb_heads).reshape(1, sb_w),
                jnp.tile(sb_k_norm[i], sb_heads).reshape(1, sb_w), seq, gdn_dv)
            o_a = _gdn(gdn_qkv, gdn_gate, ab, gdn_a_log[i], gdn_dt_bias[i], gdn_norm[i], batch, seq)
            o_b = _sb_attention(qn, kn, vn, suffix, batch, seq)
            w_out = hyb_w_out[i].astype(BF16)
            mix_outs = [o_a, o_b]
            mix_ws = [w_out[:gdn_w], w_out[gdn_w:]]
        else:
            rq, rk, rv, gate = _ret_proj(xf, gain_m, sc_m, sh_m, ret_w_in[i].astype(BF16), cos, sin,
                                         ret_q_scale, ret_k_scale, ret_qk, ret_v, seq)
            o_r = _retention(rq, rk, rv, gate, ret_chunk_decay, batch, seq)
            mix_outs = [o_r]
            mix_ws = [ret_w_out[i].astype(BF16)]
        xf = _post(xf, mix_outs, mix_ws, gt_m, norm_ffn[l].reshape(1, d), sc_f, sh_f, gt_f,
                   ffn_w_in[l].astype(BF16), ffn_w_out[l].astype(BF16), seq)
    return xf.reshape(batch, seq, d)
```

```python
import functools

import jax
import jax.numpy as jnp
from jax import lax
from jax.experimental import pallas as pl
from jax.experimental.pallas import tpu as pltpu

F32 = jnp.float32
BF16 = jnp.bfloat16
EPS = 1e-6
ROPE_BASE = 10000.0

LANES = 128
MXU_DIM = 256
VMEM_LIMIT = 56 * 1024 * 1024

GDN_CONV = 4
GDN_BLOCK = 128
GDN_STEP = 512
SB_TILE = 256
SB_DH = 64
SB_STRIPS = 2
SB_TILES_PER_STEP = 4
LOG2_E = 1.4426950408889634
SB_UNDERFLOW_LOG2 = 180.0
RET_CHUNK = 256
RET_CHUNKS_PER_STEP = 2
ROW_TILE_PROJ = 512
PROJ_GROUP = 256
ROW_TILE_FFN = 512
FFN_CHUNK = 256
CONV_PAD = 8


def _dot(a, b):
    return jnp.dot(a, b, preferred_element_type=F32)


def _dot_nt(a, b):
    return lax.dot_general(a, b, (((1,), (1,)), ((), ())), preferred_element_type=F32)


def _split2(a):
    hi = a.astype(BF16)
    lo = (a - hi.astype(F32)).astype(BF16)
    return hi, lo


def _dot3(a, b):
    ah, al = _split2(a)
    bh, bl = _split2(b)
    n = b.shape[1]
    if 2 * n <= MXU_DIM:
        both = _dot(ah, jnp.concatenate([bh, bl], axis=1))
        return (both[:, :n] + both[:, n:]) + _dot(al, bh)
    return _dot(ah, bh) + (_dot(ah, bl) + _dot(al, bh))


def _dot_exact_rhs(a, m):
    ah, al = _split2(a)
    return _dot(ah, m) + _dot(al, m)


def _dot_exact_lhs(m, a):
    ah, al = _split2(a)
    n = a.shape[1]
    if 2 * n <= MXU_DIM:
        both = _dot(m, jnp.concatenate([ah, al], axis=1))
        return both[:, :n] + both[:, n:]
    return _dot(m, ah) + _dot(m, al)


def _silu(x):
    return x * jax.nn.sigmoid(x)


def _softplus(x):
    return jnp.maximum(x, 0.0) + jnp.log1p(jnp.exp(-jnp.abs(x)))


def _params(n_axes):
    return pltpu.CompilerParams(dimension_semantics=("arbitrary",) * n_axes,
                                vmem_limit_bytes=VMEM_LIMIT)


def _resident(shape, index_map):
    return pl.BlockSpec(shape, index_map, pipeline_mode=pl.Buffered(1))


def _mod_kernel(c_ref, w_ref, b_ref, o_ref):
    c = c_ref[...]
    o_ref[0] = _dot(_silu(c).astype(BF16), w_ref[0].astype(BF16)) + b_ref[0]


def _modulation(c, ada_w, ada_b):
    depth, d, n = ada_w.shape
    b = c.shape[0]
    rows = 8
    c_pad = jnp.zeros((rows, d), F32).at[:b].set(c)
    tn = 1024
    out = pl.pallas_call(
        _mod_kernel,
        grid=(depth, n // tn),
        in_specs=[pl.BlockSpec((rows, d), lambda l, j: (0, 0)),
                  pl.BlockSpec((1, d, tn), lambda l, j: (l, 0, j)),
                  pl.BlockSpec((1, 1, tn), lambda l, j: (l, 0, j))],
        out_specs=pl.BlockSpec((1, rows, tn), lambda l, j: (l, 0, j)),
        out_shape=jax.ShapeDtypeStruct((depth, rows, n), F32),
        compiler_params=_params(2),
        name="adaln_modulation",
    )(c_pad, ada_w, ada_b.reshape(depth, 1, n))
    return out[:, :b]


def _norm_mod(x, gain, sc, sh):
    ms = jnp.mean(x * x, axis=-1, keepdims=True)
    y = x * lax.rsqrt(ms + EPS)
    y = y * gain
    return y * (1.0 + sc) + sh


def _hyb_proj_kernel(x_ref, g_ref, sc_ref, sh_ref, w_ref, cw_ref, gm_ref, qg_ref, kg_ref,
                     qkv_ref, gate_ref, ab_ref, q_ref, k_ref, v_ref, ext_ref, *, per_batch, dk):
    i = pl.program_id(0)
    tm = x_ref.shape[0]
    n_conv = qkv_ref.shape[1]
    n_gate = gate_ref.shape[1]
    w_sb = q_ref.shape[1]
    groups = [slice(r0, r0 + PROJ_GROUP) for r0 in range(0, tm, PROJ_GROUP)]
    hbs = [_norm_mod(x_ref[rows, :], g_ref[...], sc_ref[0], sh_ref[0]).astype(BF16) for rows in groups]

    @pl.when(i % per_batch == 0)
    def _():
        ext_ref[0:CONV_PAD, :] = jnp.zeros((CONV_PAD, n_conv), F32)

    def l2n(y):
        return y * lax.rsqrt(jnp.sum(y * y, axis=-1, keepdims=True) + EPS)

    for c0 in range(0, n_conv, MXU_DIM):
        for rows, hb in zip(groups, hbs):
            ext_ref[CONV_PAD + rows.start:CONV_PAD + rows.stop, c0:c0 + MXU_DIM] = _dot(hb, w_ref[:, c0:c0 + MXU_DIM])

    def conv_chunk(c0, rows):
        cols = slice(c0, c0 + MXU_DIM)
        w = cw_ref[:, cols]
        y = ext_ref[CONV_PAD + rows.start:CONV_PAD + rows.stop, cols] * w[GDN_CONV - 1:GDN_CONV, :]
        for j in range(GDN_CONV - 1):
            shift = GDN_CONV - 1 - j
            y = y + ext_ref[CONV_PAD + rows.start - shift:CONV_PAD + rows.stop - shift, cols] * w[j:j + 1, :]
        y = _silu(y)
        if c0 < 2 * n_gate:
            y = jnp.concatenate([l2n(y[:, h0:h0 + dk]) for h0 in range(0, MXU_DIM, dk)], axis=1)
            if c0 < n_gate:
                y = y * (dk ** -0.5)
        qkv_ref[rows, cols] = y

    def head_rms(xf, gain):
        sq = xf * xf
        ms = jnp.concatenate([_dot_exact_rhs(sq[:, c0:c0 + MXU_DIM], gm_ref[...])
                              for c0 in range(0, xf.shape[1], MXU_DIM)], axis=1)
        return xf * lax.rsqrt(ms + EPS) * gain

    off_gate = n_conv
    off_ab = off_gate + n_gate
    off_q = off_ab + LANES
    off_k = off_q + w_sb
    off_v = off_k + w_sb
    conv_starts = list(range(0, n_conv, MXU_DIM))
    for rows, hb in zip(groups, hbs):
        gate_ref[rows, :] = _dot(hb, w_ref[:, off_gate:off_gate + n_gate])
    for rows, hb in zip(groups, hbs):
        sq = _dot(hb, w_ref[:, off_q:off_q + w_sb])
        q_ref[rows, :] = (head_rms(sq, qg_ref[...]) * (SB_DH ** -0.5)).astype(BF16)
    for rows, hb in zip(groups, hbs):
        sk = _dot(hb, w_ref[:, off_k:off_k + w_sb])
        k_ref[rows, :] = head_rms(sk, kg_ref[...]).astype(BF16)
    for rows, hb in zip(groups, hbs):
        v_ref[rows, :] = _dot(hb, w_ref[:, off_v:off_v + w_sb]).astype(BF16)
        ab_ref[rows, :] = _dot(hb, w_ref[:, off_ab:off_ab + LANES])
    for c0 in conv_starts:
        for rows in groups:
            conv_chunk(c0, rows)
    ext_ref[0:CONV_PAD, :] = ext_ref[tm:tm + CONV_PAD, :]


def _hyb_proj(x, gain, sc, sh, w, conv_w, gmat, qg, kg, seq, gdn_dk):
    n, d = x.shape
    tm = ROW_TILE_PROJ
    per_b = seq // tm
    w_sb = qg.shape[1]
    n_conv = conv_w.shape[1]
    n_gate = w.shape[1] - n_conv - LANES - 3 * w_sb
    bmap = lambda i: (i // per_b, 0, 0)
    row = lambda i: (i, 0)
    const = lambda i: (0, 0)
    return pl.pallas_call(
        functools.partial(_hyb_proj_kernel, per_batch=per_b, dk=gdn_dk),
        grid=(n // tm,),
        in_specs=[pl.BlockSpec((tm, d), row),
                  pl.BlockSpec((1, d), const),
                  pl.BlockSpec((1, 1, d), bmap),
                  pl.BlockSpec((1, 1, d), bmap),
                  _resident(w.shape, const),
                  pl.BlockSpec(conv_w.shape, const),
                  _resident(gmat.shape, const),
                  pl.BlockSpec((1, w_sb), const),
                  pl.BlockSpec((1, w_sb), const)],
        out_specs=[pl.BlockSpec((tm, n_conv), row),
                   pl.BlockSpec((tm, n_gate), row),
                   pl.BlockSpec((tm, LANES), row),
                   pl.BlockSpec((tm, w_sb), row),
                   pl.BlockSpec((tm, w_sb), row),
                   pl.BlockSpec((tm, w_sb), row)],
        out_shape=[jax.ShapeDtypeStruct((n, n_conv), F32),
                   jax.ShapeDtypeStruct((n, n_gate), F32),
                   jax.ShapeDtypeStruct((n, LANES), F32),
                   jax.ShapeDtypeStruct((n, w_sb), BF16),
                   jax.ShapeDtypeStruct((n, w_sb), BF16),
                   jax.ShapeDtypeStruct((n, w_sb), BF16)],
        scratch_shapes=[pltpu.VMEM((CONV_PAD + tm, n_conv), F32)],
        compiler_params=_params(1),
        name="hybrid_in_proj",
    )(x, gain, sc, sh, w, conv_w, gmat, qg, kg)


def _dot3s(a, b):
    ah, al = a
    bh, bl = b
    n = bh.shape[1]
    both = _dot(ah, jnp.concatenate([bh, bl], axis=1))
    return (both[:, :n] + both[:, n:]) + _dot(al, bh)


def _lower_child_rows(x, size):
    return jnp.concatenate([x[r + size:r + 2 * size] for r in range(0, x.shape[0], 2 * size)], axis=0)


def _unit_lower_inverse_all(lowers, eye, row, col):
    bits = 3
    same = (row >> bits) == (col >> bits)
    diags = [jnp.where(same, l, 0.0) for l in lowers]
    d_s = [_split2(d) for d in diags]
    x0_s = [_split2(eye - d) for d in diags]
    p2_s = [_split2(_dot3s(d, d)) for d in d_s]
    xs = [(eye - d) + _dot3s(x, p) for d, x, p in zip(diags, x0_s, p2_s)]
    p4_s = [_split2(_dot3s(p, p)) for p in p2_s]
    xs = [x + _dot3s(_split2(x), p) for x, p in zip(xs, p4_s)]
    n = lowers[0].shape[0]
    size = 1 << bits
    while size < n:
        parent = (row >> (bits + 1)) == (col >> (bits + 1))
        child = (row >> bits) == (col >> bits)
        off_s = [_split2(jnp.where(parent, jnp.where(child, 0.0, l), 0.0)) for l in lowers]
        x_s = [_split2(x) for x in xs]
        ys = [_dot3s(_split2(_lower_child_rows(x, size)), o) for x, o in zip(xs, off_s)]
        zs = [_dot3s(_split2(y), x) for y, x in zip(ys, x_s)]
        new_xs = []
        for x, z in zip(xs, zs):
            pieces = []
            for k, r in enumerate(range(0, n, 2 * size)):
                pieces.append(x[r:r + size])
                pieces.append(x[r + size:r + 2 * size] - z[k * size:(k + 1) * size])
            new_xs.append(jnp.concatenate(pieces, axis=0))
        xs = new_xs
        bits += 1
        size *= 2
    return xs


def _gdn_kernel(q_ref, k_ref, v_ref, gate_ref, ab_ref, alog_ref, dtb_ref, gn_ref, o_ref, state_ref):
    t = pl.program_id(1)
    tb, width = q_ref.shape
    c = GDN_BLOCK
    n_heads = state_ref.shape[0]
    dk = width // n_heads
    n_chunks = tb // c

    @pl.when(t == 0)
    def _():
        state_ref[...] = jnp.zeros_like(state_ref)

    lane = lax.broadcasted_iota(jnp.int32, (tb, LANES), 1)
    ab = ab_ref[...]
    g_lanes = -jnp.exp(alog_ref[...]) * _softplus(ab + dtb_ref[...])
    beta_lanes = jax.nn.sigmoid(ab)

    def lane_column(x, idx):
        return jnp.broadcast_to(jnp.sum(jnp.where(lane == idx, x, 0.0), axis=1, keepdims=True), (tb, LANES))

    row = lax.broadcasted_iota(jnp.int32, (c, c), 0)
    col = lax.broadcasted_iota(jnp.int32, (c, c), 1)
    incl = row >= col
    strict = row > col
    eye = jnp.where(row == col, 1.0, 0.0).astype(F32)
    tri = jnp.where(incl, 1.0, 0.0).astype(BF16)

    items = []
    for h in range(n_heads):
        g_h = lane_column(g_lanes, h)
        beta_h = lane_column(beta_lanes, h + n_heads)
        cols = slice(h * dk, (h + 1) * dk)
        for n in range(n_chunks):
            rows = slice(n * c, (n + 1) * c)
            items.append(dict(q=q_ref[rows, cols], k=k_ref[rows, cols], v=v_ref[rows, cols],
                              beta=beta_h[rows], g=g_h[rows]))

    def prepare(group):
        for it in group:
            it["gcol"] = _dot_exact_lhs(tri, it["g"])
        for it in group:
            gcol = it["gcol"]
            it["dec"] = jnp.exp(jnp.where(incl, gcol - gcol.T, 0.0))
            it["eg"] = jnp.exp(gcol)
            it["g_last"] = gcol[c - 1:c, :]
            it["kb"] = it["k"] * it["beta"]
            it["kk"] = it["k"].astype(BF16)
        for it in group:
            it["lower"] = jnp.where(strict, _dot_nt(it["kb"].astype(BF16), it["kk"]) * it["dec"], 0.0)
        tinvs = _unit_lower_inverse_all([it["lower"] for it in group], eye, row, col)
        for it, tinv in zip(group, tinvs):
            rhs = jnp.concatenate([it["v"] * it["beta"], it["kb"] * it["eg"]], axis=1)
            uw = _dot3(tinv, rhs)
            it["u"] = uw[:, :dk]
            a = jnp.where(incl, _dot_nt(it["q"].astype(BF16), it["kk"]) * it["dec"], 0.0)
            q_dec = it["q"] * it["eg"]
            k_dec = it["k"] * jnp.exp(it["g_last"] - it["gcol"])
            it["wq"] = jnp.concatenate([uw[:, dk:], q_dec], axis=0).astype(BF16)
            it["ak"] = jnp.concatenate([a, k_dec.T], axis=0).astype(BF16)
            it["decay"] = jnp.exp(it["g_last"])

    early = n_chunks // 2
    prepare([it for i, it in enumerate(items) if i % n_chunks < early])
    prepare([it for i, it in enumerate(items) if i % n_chunks >= early])

    gn = gn_ref[...]
    states = [state_ref[h] for h in range(n_heads)]
    for n in range(n_chunks):
        rows = slice(n * c, (n + 1) * c)
        for h in range(n_heads):
            it = items[h * n_chunks + n]
            cols = slice(h * dk, (h + 1) * dk)
            ws_qs = _dot(it["wq"], states[h].astype(BF16))
            v_new = it["u"] - ws_qs[:c]
            av_kv = _dot(it["ak"], v_new.astype(BF16))
            o = ws_qs[c:] + av_kv[:c]
            states[h] = states[h] * it["decay"] + av_kv[c:]
            ms = jnp.mean(o * o, axis=-1, keepdims=True)
            o = o * lax.rsqrt(ms + EPS) * gn
            o_ref[rows, cols] = (o * _silu(gate_ref[rows, cols])).astype(o_ref.dtype)
    for h in range(n_heads):
        state_ref[h] = states[h]


def _gdn(qkv, gate, ab, a_log, dt_bias, gdn_norm, batch, seq):
    n, width = gate.shape
    n_heads = a_log.shape[0]
    dk = width // n_heads
    tb = GDN_STEP
    nt = seq // tb
    rows = lambda b, t: b * nt + t
    pad = lambda x: jnp.zeros((1, LANES), F32).at[0, :n_heads].set(x)
    col_spec = lambda k: pl.BlockSpec((tb, width), lambda b, t: (rows(b, t), k))
    const = lambda b, t: (0, 0)
    return pl.pallas_call(
        _gdn_kernel,
        grid=(batch, nt),
        in_specs=[col_spec(0), col_spec(1), col_spec(2), col_spec(0),
                  pl.BlockSpec((tb, LANES), lambda b, t: (rows(b, t), 0)),
                  pl.BlockSpec((1, LANES), const), pl.BlockSpec((1, LANES), const),
                  pl.BlockSpec((1, dk), const)],
        out_specs=pl.BlockSpec((tb, width), lambda b, t: (rows(b, t), 0)),
        out_shape=jax.ShapeDtypeStruct((n, width), BF16),
        scratch_shapes=[pltpu.VMEM((n_heads, dk, dk), F32)],
        compiler_params=_params(2),
        name="gated_delta_rule",
    )(qkv, qkv, qkv, gate, ab, pad(a_log), pad(dt_bias), gdn_norm.reshape(1, dk))


def _sb_kernel(q_ref, k_ref, v_ref, m_ref, o_ref, q2_ref, acc_ref, carry_ref):
    i = pl.program_id(2)
    tq = SB_TILE
    n_tiles = q_ref.shape[0] // tq
    lane = lax.broadcasted_iota(jnp.int32, (tq, LANES), 1)
    suffix = m_ref[...]
    strip = tq // SB_STRIPS

    for t in range(n_tiles):
        q = q_ref[t * tq:(t + 1) * tq, :]
        zero = jnp.zeros_like(q)
        q2_ref[2 * t * tq:(2 * t + 1) * tq, :] = jnp.where(lane < SB_DH, q, zero)
        q2_ref[(2 * t + 1) * tq:(2 * t + 2) * tq, :] = jnp.where(lane >= SB_DH, q, zero)

    def strips_of(t):
        return [slice(2 * t * tq + n * strip, 2 * t * tq + (n + 1) * strip) for n in range(2 * SB_STRIPS)]

    def weigh(pairs):
        work = []
        for t, j, diagonal in pairs:
            start = pl.multiple_of(j * tq, tq)
            k = k_ref[pl.ds(start, tq), :]
            v = v_ref[pl.ds(start, tq), :]
            for n, rows in enumerate(strips_of(t)):
                work.append(dict(n=n, rows=rows, k=k, v=v, diagonal=diagonal))
        for w in work:
            w["y"] = _dot_nt(q2_ref[w["rows"], :], w["k"]) * LOG2_E
        for w in work:
            y = w["y"]
            w["soft"] = jnp.maximum(y, 0.0) + jnp.log2(1.0 + jnp.exp2(-jnp.abs(y)))
            if w["diagonal"]:
                s = lax.broadcasted_iota(jnp.int32, (strip, tq), 1)
                r = lax.broadcasted_iota(jnp.int32, (strip, tq), 0) + (w["n"] % SB_STRIPS) * strip
                w["mask"] = s < r
                w["cost"] = jnp.where(w["mask"], w["soft"], 0.0)
            else:
                w["cost"] = w["soft"]
        for w in work:
            w["later"] = _dot_exact_rhs(w["cost"], suffix)
        for w in work:
            a = jnp.exp2(w["y"] - w["soft"] - w["later"])
            if w["diagonal"]:
                a = jnp.where(w["mask"], a, 0.0)
            w["part"] = _dot(a.astype(BF16), w["v"])
            w["total"] = jnp.sum(w["cost"], axis=1, keepdims=True)
        per_pair = 2 * SB_STRIPS
        return [work[p * per_pair:(p + 1) * per_pair] for p in range(len(pairs))]

    first = i * n_tiles
    pairs = [(t, first + t, True) for t in range(n_tiles)]
    pairs += [(t, jnp.maximum(first + t - 1, 0), False) for t in range(n_tiles)]
    results = weigh(pairs)
    for t in range(n_tiles):
        has_prev = (first + t > 0).astype(F32)
        for d, p in zip(results[t], results[n_tiles + t]):
            acc_ref[d["rows"], :] = d["part"] + p["part"] * (jnp.exp2(-d["total"]) * has_prev)
            carry_ref[d["rows"], :] = d["total"] + p["total"] * has_prev

    @pl.when(jnp.min(carry_ref[...]) < SB_UNDERFLOW_LOG2)
    def _():
        for t in range(n_tiles):
            rows_t = slice(2 * t * tq, 2 * (t + 1) * tq)
            remaining = first + t - 1

            def cond(state):
                step, least = state
                return jnp.logical_and(step < remaining, least < SB_UNDERFLOW_LOG2)

            def body(state, t=t, rows_t=rows_t, remaining=remaining):
                step, _ = state
                (tile,) = weigh([(t, remaining - 1 - step, False)])
                for w in tile:
                    carry = carry_ref[w["rows"], :]
                    acc_ref[w["rows"], :] += w["part"] * jnp.exp2(-carry)
                    carry_ref[w["rows"], :] = carry + w["total"]
                return step + 1, jnp.min(carry_ref[rows_t, :])

            lax.while_loop(cond, body, (jnp.int32(0), jnp.min(carry_ref[rows_t, :])))

    for t in range(n_tiles):
        acc0 = acc_ref[2 * t * tq:(2 * t + 1) * tq, :]
        acc1 = acc_ref[(2 * t + 1) * tq:(2 * t + 2) * tq, :]
        o_ref[t * tq:(t + 1) * tq, :] = jnp.where(lane < SB_DH, acc0, acc1).astype(o_ref.dtype)


def _sb_attention(qn, kn, vn, suffix, batch, seq):
    n, width = qn.shape
    pairs = width // LANES
    tq = SB_TILE
    rows = SB_TILES_PER_STEP * tq
    nq = seq // rows
    return pl.pallas_call(
        _sb_kernel,
        grid=(batch, pairs, nq),
        in_specs=[pl.BlockSpec((rows, LANES), lambda b, p, i: (b * nq + i, p)),
                  pl.BlockSpec((seq, LANES), lambda b, p, i: (b, p)),
                  pl.BlockSpec((seq, LANES), lambda b, p, i: (b, p)),
                  pl.BlockSpec((tq, tq), lambda b, p, i: (0, 0))],
        out_specs=pl.BlockSpec((rows, LANES), lambda b, p, i: (b * nq + i, p)),
        out_shape=jax.ShapeDtypeStruct((n, width), BF16),
        scratch_shapes=[pltpu.VMEM((2 * rows, LANES), BF16),
                        pltpu.VMEM((2 * rows, LANES), F32),
                        pltpu.VMEM((2 * rows, 1), F32)],
        compiler_params=_params(3),
        name="stick_breaking_attention",
    )(qn, kn, vn, suffix)


def _ret_proj_kernel(x_ref, g_ref, sc_ref, sh_ref, w_ref, cos_ref, sin_ref, qs_ref, ks_ref,
                     q_ref, k_ref, v_ref, gate_ref):
    tm = x_ref.shape[0]
    n_heads, chunk, _ = qs_ref.shape
    w_qk = q_ref.shape[1]
    dk = w_qk // n_heads
    half = dk // 2
    groups = [slice(r0, r0 + chunk) for r0 in range(0, tm, chunk)]
    hbs = [_norm_mod(x_ref[rows, :], g_ref[...], sc_ref[0], sh_ref[0]).astype(BF16) for rows in groups]

    def rot(x, rows, scale):
        cos = cos_ref[rows, :]
        sin = sin_ref[rows, :]
        x1, x2 = x[:, :half], x[:, half:]
        return jnp.concatenate([(x1 * cos - x2 * sin) * scale, (x1 * sin + x2 * cos) * scale], axis=-1)

    for h in range(n_heads):
        cols = slice(h * dk, (h + 1) * dk)
        for rows, hb in zip(groups, hbs):
            q_ref[rows, cols] = rot(_dot(hb, w_ref[:, cols]), rows, qs_ref[h]).astype(BF16)
    for h in range(n_heads):
        cols = slice(h * dk, (h + 1) * dk)
        for rows, hb in zip(groups, hbs):
            pre = _dot(hb, w_ref[:, w_qk + h * dk:w_qk + (h + 1) * dk])
            k_ref[rows, cols] = rot(pre, rows, ks_ref[h]).astype(BF16)
    step = 512
    off = 2 * w_qk
    for ref in (v_ref, gate_ref):
        for o in range(0, ref.shape[1], step):
            for rows, hb in zip(groups, hbs):
                ref[rows, o:o + step] = _dot(hb, w_ref[:, off + o:off + o + step]).astype(ref.dtype)
        off += ref.shape[1]


def _ret_proj(x, gain, sc, sh, w, cos, sin, q_scale, k_scale, n_qk, n_v, seq):
    n, d = x.shape
    tm = ROW_TILE_PROJ
    per_b = seq // tm
    bmap = lambda i: (i // per_b, 0, 0)
    row = lambda i: (i, 0)
    const = lambda i: (0, 0)
    pos = lambda i: (i % per_b, 0)
    return pl.pallas_call(
        _ret_proj_kernel,
        grid=(n // tm,),
        in_specs=[pl.BlockSpec((tm, d), row),
                  pl.BlockSpec((1, d), const),
                  pl.BlockSpec((1, 1, d), bmap),
                  pl.BlockSpec((1, 1, d), bmap),
                  _resident(w.shape, const),
                  pl.BlockSpec((tm, cos.shape[1]), pos),
                  pl.BlockSpec((tm, cos.shape[1]), pos),
                  pl.BlockSpec(q_scale.shape, lambda i: (0, 0, 0)),
                  pl.BlockSpec(k_scale.shape, lambda i: (0, 0, 0))],
        out_specs=[pl.BlockSpec((tm, n_qk), row),
                   pl.BlockSpec((tm, n_qk), row),
                   pl.BlockSpec((tm, n_v), row),
                   pl.BlockSpec((tm, n_v), row)],
        out_shape=[jax.ShapeDtypeStruct((n, n_qk), BF16),
                   jax.ShapeDtypeStruct((n, n_qk), BF16),
                   jax.ShapeDtypeStruct((n, n_v), BF16),
                   jax.ShapeDtypeStruct((n, n_v), F32)],
        compiler_params=_params(1),
        name="retention_in_proj",
    )(x, gain, sc, sh, w, cos, sin, q_scale, k_scale)


def _ret_kernel(q_ref, k_ref, v_ref, gate_ref, decay_ref, o_ref, state_ref):
    t = pl.program_id(1)
    c = RET_CHUNK
    n_heads = state_ref.shape[0]
    dk = q_ref.shape[1] // n_heads
    dv = v_ref.shape[1] // n_heads

    @pl.when(t == 0)
    def _():
        state_ref[...] = jnp.zeros_like(state_ref)

    row = lax.broadcasted_iota(jnp.int32, (c, c), 0)
    col = lax.broadcasted_iota(jnp.int32, (c, c), 1)
    causal = row >= col

    for r0 in range(0, q_ref.shape[0], c):
        rows = slice(r0, r0 + c)
        for h in range(n_heads):
            q = q_ref[rows, h * dk:(h + 1) * dk]
            k = k_ref[rows, h * dk:(h + 1) * dk]
            v = v_ref[rows, h * dv:(h + 1) * dv]
            intra = jnp.where(causal, _dot_nt(q, k), 0.0)
            state = state_ref[h]
            o = _dot(intra.astype(BF16), v) + _dot(q, state.astype(BF16))
            k_t = k.astype(F32).T.astype(BF16)
            state_ref[h] = (state + _dot(k_t, v)) * decay_ref[h]
            ms = jnp.mean(o * o, axis=-1, keepdims=True)
            o = o * lax.rsqrt(ms + EPS)
            gate = gate_ref[rows, h * dv:(h + 1) * dv]
            o_ref[rows, h * dv:(h + 1) * dv] = (o * _silu(gate)).astype(o_ref.dtype)


def _retention(q, k, v, gate, chunk_decay, batch, seq):
    n, w_qk = q.shape
    n_heads = chunk_decay.shape[0]
    w_v = v.shape[1]
    c = RET_CHUNKS_PER_STEP * RET_CHUNK
    nt = seq // c
    rows = lambda b, t: (b * nt + t, 0)
    return pl.pallas_call(
        _ret_kernel,
        grid=(batch, nt),
        in_specs=[pl.BlockSpec((c, w_qk), rows),
                  pl.BlockSpec((c, w_qk), rows),
                  pl.BlockSpec((c, w_v), rows),
                  pl.BlockSpec((c, w_v), rows),
                  pl.BlockSpec(chunk_decay.shape, lambda b, t: (0, 0, 0))],
        out_specs=pl.BlockSpec((c, w_v), rows),
        out_shape=jax.ShapeDtypeStruct((n, w_v), BF16),
        scratch_shapes=[pltpu.VMEM((n_heads, w_qk // n_heads, w_v // n_heads), F32)],
        compiler_params=_params(2),
        name="retention_chunkwise",
    )(q, k, v, gate, chunk_decay)


def _post_kernel(*refs, n_mix):
    x_ref = refs[0]
    o_refs = refs[1:1 + n_mix]
    wm_refs = refs[1 + n_mix:1 + 2 * n_mix]
    gtm_ref, g_ref, sc_ref, sh_ref, gtf_ref, win_ref, wout_ref, out_ref, acc_ref = refs[1 + 2 * n_mix:]
    y = _dot(o_refs[0][...], wm_refs[0][...])
    for o_ref, w_ref in zip(o_refs[1:], wm_refs[1:]):
        y = y + _dot(o_ref[...], w_ref[...])
    x1 = x_ref[...] + gtm_ref[0] * y
    hb = _norm_mod(x1, g_ref[...], sc_ref[0], sh_ref[0]).astype(BF16)
    hidden = wout_ref.shape[0]
    for idx, off in enumerate(range(0, hidden, FFN_CHUNK)):
        g = _dot(hb, win_ref[:, off:off + FFN_CHUNK])
        u = _dot(hb, win_ref[:, hidden + off:hidden + off + FFN_CHUNK])
        part = _dot((_silu(g) * u).astype(BF16), wout_ref[off:off + FFN_CHUNK, :])
        if idx == 0:
            acc_ref[...] = part
        else:
            acc_ref[...] += part
    out_ref[...] = x1 + gtf_ref[0] * acc_ref[...]


def _post(x, mix_outs, mix_ws, gt_m, gain, sc, sh, gt_f, w_in, w_out, seq):
    n, d = x.shape
    tm = ROW_TILE_FFN
    per_b = seq // tm
    bmap = lambda i: (i // per_b, 0, 0)
    row = lambda i: (i, 0)
    const = lambda i: (0, 0)
    vec = pl.BlockSpec((1, 1, d), bmap)
    n_mix = len(mix_outs)
    return pl.pallas_call(
        functools.partial(_post_kernel, n_mix=n_mix),
        grid=(n // tm,),
        in_specs=([pl.BlockSpec((tm, d), row)]
                  + [pl.BlockSpec((tm, o.shape[1]), row) for o in mix_outs]
                  + [_resident(w.shape, const) for w in mix_ws]
                  + [vec, pl.BlockSpec((1, d), const), vec, vec, vec,
                     _resident(w_in.shape, const), _resident(w_out.shape, const)]),
        out_specs=pl.BlockSpec((tm, d), row),
        out_shape=jax.ShapeDtypeStruct((n, d), F32),
        scratch_shapes=[pltpu.VMEM((tm, d), F32)],
        compiler_params=_params(1),
        name="out_proj_swiglu",
    )(x, *mix_outs, *mix_ws, gt_m, gain, sc, sh, gt_f, w_in, w_out)


def kernel(x, c, ada_w, ada_b, norm_mix, norm_ffn, hyb_w_in, hyb_conv, gdn_a_log, gdn_dt_bias, gdn_norm, sb_q_norm, sb_k_norm, hyb_w_out, ret_w_in, ret_w_out, ffn_w_in, ffn_w_out):
    batch, seq, d = x.shape
    depth = ada_w.shape[0]
    gdn_heads = gdn_a_log.shape[1]
    gdn_dv = gdn_norm.shape[1]
    gdn_w = gdn_heads * gdn_dv
    sb_dh = sb_q_norm.shape[1]
    assert sb_dh == SB_DH
    sb_w = hyb_w_out.shape[1] - gdn_w
    sb_heads = sb_w // sb_dh
    ret_heads = 4
    ret_qk = ret_w_in.shape[2] // 6
    ret_v = 2 * ret_qk
    ret_dk = ret_qk // ret_heads

    xf = x.reshape(batch * seq, d)
    mod = _modulation(c, ada_w, ada_b)

    head_mean = jnp.kron(jnp.eye(MXU_DIM // sb_dh, dtype=F32), jnp.full((sb_dh, sb_dh), 1.0 / sb_dh, F32)).astype(BF16)
    idx = jnp.arange(SB_TILE)
    suffix = (idx[:, None] > idx[None, :]).astype(BF16)
    pos = jnp.arange(seq, dtype=F32)
    inv = 1.0 / (ROPE_BASE ** (jnp.arange(0, ret_dk, 2, dtype=F32) / ret_dk))
    ang = pos[:, None] * inv[None, :]
    cos, sin = jnp.cos(ang), jnp.sin(ang)
    log_gamma = jnp.log1p(-jnp.exp2(-5.0 - jnp.arange(ret_heads, dtype=F32)))
    steps = jnp.arange(1, RET_CHUNK + 1, dtype=F32)
    ret_q_scale = jnp.broadcast_to(jnp.exp(log_gamma[:, None] * steps[None, :])[:, :, None],
                                   (ret_heads, RET_CHUNK, ret_dk // 2))
    ret_k_scale = jnp.broadcast_to((jnp.exp(-log_gamma[:, None] * steps[None, :]) * ret_dk ** -0.5)[:, :, None],
                                   (ret_heads, RET_CHUNK, ret_dk // 2))
    ret_chunk_decay = jnp.broadcast_to(jnp.exp(log_gamma * RET_CHUNK)[:, None, None], (ret_heads, 1, ret_v // ret_heads))

    for l in range(depth):
        sh_m, sc_m, gt_m, sh_f, sc_f, gt_f = [m.reshape(batch, 1, d) for m in jnp.split(mod[l], 6, axis=-1)]
        i = l // 2
        gain_m = norm_mix[l].reshape(1, d)
        if l % 2 == 0:
            w_in = hyb_w_in[i]
            n_main = 4 * gdn_w
            n_ab = 2 * gdn_heads
            ab_cols = jnp.zeros((d, LANES), F32).at[:, :n_ab].set(w_in[:, n_main:n_main + n_ab])
            w_re = jnp.concatenate([w_in[:, :n_main], ab_cols, w_in[:, n_main + n_ab:]], axis=1).astype(BF16)
            gdn_qkv, gdn_gate, ab, qn, kn, vn = _hyb_proj(
                xf, gain_m, sc_m, sh_m, w_re, hyb_conv[i], head_mean,
                jnp.tile(sb_q_norm[i], sb_heads).reshape(1, sb_w),
                jnp.tile(sb_k_norm[i], sb_heads).reshape(1, sb_w), seq, gdn_dv)
            o_a = _gdn(gdn_qkv, gdn_gate, ab, gdn_a_log[i], gdn_dt_bias[i], gdn_norm[i], batch, seq)
            o_b = _sb_attention(qn, kn, vn, suffix, batch, seq)
            w_out = hyb_w_out[i].astype(BF16)
            mix_outs = [o_a, o_b]
            mix_ws = [w_out[:gdn_w], w_out[gdn_w:]]
        else:
            rq, rk, rv, gate = _ret_proj(xf, gain_m, sc_m, sh_m, ret_w_in[i].astype(BF16), cos, sin,
                                         ret_q_scale, ret_k_scale, ret_qk, ret_v, seq)
            o_r = _retention(rq, rk, rv, gate, ret_chunk_decay, batch, seq)
            mix_outs = [o_r]
            mix_ws = [ret_w_out[i].astype(BF16)]
        xf = _post(xf, mix_outs, mix_ws, gt_m, norm_ffn[l].reshape(1, d), sc_f, sh_f, gt_f,
                   ffn_w_in[l].astype(BF16), ffn_w_out[l].astype(BF16), seq)
    return xf.reshape(batch, seq, d)
```

```python
import functools

import jax
import jax.numpy as jnp
from jax import lax
from jax.experimental import pallas as pl
from jax.experimental.pallas import tpu as pltpu

F32 = jnp.float32
BF16 = jnp.bfloat16
EPS = 1e-6
ROPE_BASE = 10000.0

LANES = 128
MXU_DIM = 256
VMEM_LIMIT = 56 * 1024 * 1024

GDN_CONV = 4
GDN_BLOCK = 128
GDN_STEP = 512
SB_TILE = 256
SB_DH = 64
SB_STRIPS = 2
SB_TILES_PER_STEP = 4
LOG2_E = 1.4426950408889634
SB_UNDERFLOW_LOG2 = 180.0
RET_CHUNK = 256
RET_CHUNKS_PER_STEP = 2
ROW_TILE_PROJ = 512
PROJ_GROUP = 256
ROW_TILE_FFN = 512
FFN_CHUNK = 256
CONV_PAD = 8


def _dot(a, b):
    return jnp.dot(a, b, preferred_element_type=F32)


def _dot_nt(a, b):
    return lax.dot_general(a, b, (((1,), (1,)), ((), ())), preferred_element_type=F32)


def _split2(a):
    hi = a.astype(BF16)
    lo = (a - hi.astype(F32)).astype(BF16)
    return hi, lo


def _dot3(a, b):
    ah, al = _split2(a)
    bh, bl = _split2(b)
    n = b.shape[1]
    if 2 * n <= MXU_DIM:
        both = _dot(ah, jnp.concatenate([bh, bl], axis=1))
        return (both[:, :n] + both[:, n:]) + _dot(al, bh)
    return _dot(ah, bh) + (_dot(ah, bl) + _dot(al, bh))


def _dot_exact_rhs(a, m):
    ah, al = _split2(a)
    return _dot(ah, m) + _dot(al, m)


def _dot_exact_lhs(m, a):
    ah, al = _split2(a)
    n = a.shape[1]
    if 2 * n <= MXU_DIM:
        both = _dot(m, jnp.concatenate([ah, al], axis=1))
        return both[:, :n] + both[:, n:]
    return _dot(m, ah) + _dot(m, al)


def _silu(x):
    return x * jax.nn.sigmoid(x)


def _softplus(x):
    return jnp.maximum(x, 0.0) + jnp.log1p(jnp.exp(-jnp.abs(x)))


def _params(n_axes):
    return pltpu.CompilerParams(dimension_semantics=("arbitrary",) * n_axes,
                                vmem_limit_bytes=VMEM_LIMIT)


def _resident(shape, index_map):
    return pl.BlockSpec(shape, index_map, pipeline_mode=pl.Buffered(1))


def _mod_kernel(c_ref, w_ref, b_ref, o_ref):
    c = c_ref[...]
    o_ref[0] = _dot(_silu(c).astype(BF16), w_ref[0].astype(BF16)) + b_ref[0]


def _modulation(c, ada_w, ada_b):
    depth, d, n = ada_w.shape
    b = c.shape[0]
    rows = 8
    c_pad = jnp.zeros((rows, d), F32).at[:b].set(c)
    tn = 1024
    out = pl.pallas_call(
        _mod_kernel,
        grid=(depth, n // tn),
        in_specs=[pl.BlockSpec((rows, d), lambda l, j: (0, 0)),
                  pl.BlockSpec((1, d, tn), lambda l, j: (l, 0, j)),
                  pl.BlockSpec((1, 1, tn), lambda l, j: (l, 0, j))],
        out_specs=pl.BlockSpec((1, rows, tn), lambda l, j: (l, 0, j)),
        out_shape=jax.ShapeDtypeStruct((depth, rows, n), F32),
        compiler_params=_params(2),
        name="adaln_modulation",
    )(c_pad, ada_w, ada_b.reshape(depth, 1, n))
    return out[:, :b]


def _norm_mod(x, gain, sc, sh):
    ms = jnp.mean(x * x, axis=-1, keepdims=True)
    y = x * lax.rsqrt(ms + EPS)
    y = y * gain
    return y * (1.0 + sc) + sh


def _hyb_proj_kernel(x_ref, g_ref, sc_ref, sh_ref, w_ref, cw_ref, gm_ref, qg_ref, kg_ref,
                     gdn_ref, sb_ref, ext_ref, *, per_batch, dk):
    i = pl.program_id(0)
    tm = x_ref.shape[0]
    n_conv = cw_ref.shape[1]
    n_gate = gdn_ref.shape[1] - n_conv - LANES
    w_sb = qg_ref.shape[1]
    groups = [slice(r0, r0 + PROJ_GROUP) for r0 in range(0, tm, PROJ_GROUP)]
    hbs = [_norm_mod(x_ref[rows, :], g_ref[...], sc_ref[0], sh_ref[0]).astype(BF16) for rows in groups]

    @pl.when(i % per_batch == 0)
    def _():
        ext_ref[0:CONV_PAD, :] = jnp.zeros((CONV_PAD, n_conv), F32)

    def l2n(y):
        return y * lax.rsqrt(jnp.sum(y * y, axis=-1, keepdims=True) + EPS)

    for c0 in range(0, n_conv, MXU_DIM):
        for rows, hb in zip(groups, hbs):
            ext_ref[CONV_PAD + rows.start:CONV_PAD + rows.stop, c0:c0 + MXU_DIM] = _dot(hb, w_ref[:, c0:c0 + MXU_DIM])

    def conv_chunk(c0, rows):
        cols = slice(c0, c0 + MXU_DIM)
        w = cw_ref[:, cols]
        y = ext_ref[CONV_PAD + rows.start:CONV_PAD + rows.stop, cols] * w[GDN_CONV - 1:GDN_CONV, :]
        for j in range(GDN_CONV - 1):
            shift = GDN_CONV - 1 - j
            y = y + ext_ref[CONV_PAD + rows.start - shift:CONV_PAD + rows.stop - shift, cols] * w[j:j + 1, :]
        y = _silu(y)
        if c0 < 2 * n_gate:
            y = jnp.concatenate([l2n(y[:, h0:h0 + dk]) for h0 in range(0, MXU_DIM, dk)], axis=1)
            if c0 < n_gate:
                y = y * (dk ** -0.5)
        gdn_ref[rows, cols] = y

    def head_rms(xf, gain):
        sq = xf * xf
        ms = jnp.concatenate([_dot_exact_rhs(sq[:, c0:c0 + MXU_DIM], gm_ref[...])
                              for c0 in range(0, xf.shape[1], MXU_DIM)], axis=1)
        return xf * lax.rsqrt(ms + EPS) * gain

    off_gate = n_conv
    off_ab = off_gate + n_gate
    off_q = off_ab + LANES
    off_k = off_q + w_sb
    off_v = off_k + w_sb
    conv_starts = list(range(0, n_conv, MXU_DIM))
    for rows, hb in zip(groups, hbs):
        gdn_ref[rows, off_gate:off_ab] = _dot(hb, w_ref[:, off_gate:off_gate + n_gate])
    for rows, hb in zip(groups, hbs):
        sq = _dot(hb, w_ref[:, off_q:off_q + w_sb])
        sb_ref[rows, 0:w_sb] = (head_rms(sq, qg_ref[...]) * (SB_DH ** -0.5)).astype(BF16)
    for rows, hb in zip(groups, hbs):
        sk = _dot(hb, w_ref[:, off_k:off_k + w_sb])
        sb_ref[rows, w_sb:2 * w_sb] = head_rms(sk, kg_ref[...]).astype(BF16)
    for rows, hb in zip(groups, hbs):
        sb_ref[rows, 2 * w_sb:3 * w_sb] = _dot(hb, w_ref[:, off_v:off_v + w_sb]).astype(BF16)
        gdn_ref[rows, off_ab:off_q] = _dot(hb, w_ref[:, off_ab:off_ab + LANES])
    for c0 in conv_starts:
        for rows in groups:
            conv_chunk(c0, rows)
    ext_ref[0:CONV_PAD, :] = ext_ref[tm:tm + CONV_PAD, :]


def _hyb_proj(x, gain, sc, sh, w, conv_w, gmat, qg, kg, seq, gdn_dk):
    n, d = x.shape
    tm = ROW_TILE_PROJ
    per_b = seq // tm
    w_sb = qg.shape[1]
    n_conv = conv_w.shape[1]
    n_gate = w.shape[1] - n_conv - LANES - 3 * w_sb
    bmap = lambda i: (i // per_b, 0, 0)
    row = lambda i: (i, 0)
    const = lambda i: (0, 0)
    return pl.pallas_call(
        functools.partial(_hyb_proj_kernel, per_batch=per_b, dk=gdn_dk),
        grid=(n // tm,),
        in_specs=[pl.BlockSpec((tm, d), row),
                  pl.BlockSpec((1, d), const),
                  pl.BlockSpec((1, 1, d), bmap),
                  pl.BlockSpec((1, 1, d), bmap),
                  _resident(w.shape, const),
                  pl.BlockSpec(conv_w.shape, const),
                  _resident(gmat.shape, const),
                  pl.BlockSpec((1, w_sb), const),
                  pl.BlockSpec((1, w_sb), const)],
        out_specs=[pl.BlockSpec((tm, n_conv + n_gate + LANES), row),
                   pl.BlockSpec((tm, 3 * w_sb), row)],
        out_shape=[jax.ShapeDtypeStruct((n, n_conv + n_gate + LANES), F32),
                   jax.ShapeDtypeStruct((n, 3 * w_sb), BF16)],
        scratch_shapes=[pltpu.VMEM((CONV_PAD + tm, n_conv), F32)],
        compiler_params=_params(1),
        name="hybrid_in_proj",
    )(x, gain, sc, sh, w, conv_w, gmat, qg, kg)


def _dot3s(a, b):
    ah, al = a
    bh, bl = b
    n = bh.shape[1]
    both = _dot(ah, jnp.concatenate([bh, bl], axis=1))
    return (both[:, :n] + both[:, n:]) + _dot(al, bh)


def _lower_child_rows(x, size):
    return jnp.concatenate([x[r + size:r + 2 * size] for r in range(0, x.shape[0], 2 * size)], axis=0)


def _unit_lower_inverse_all(lowers, eye, row, col):
    bits = 3
    same = (row >> bits) == (col >> bits)
    diags = [jnp.where(same, l, 0.0) for l in lowers]
    d_s = [_split2(d) for d in diags]
    x0_s = [_split2(eye - d) for d in diags]
    p2_s = [_split2(_dot3s(d, d)) for d in d_s]
    xs = [(eye - d) + _dot3s(x, p) for d, x, p in zip(diags, x0_s, p2_s)]
    p4_s = [_split2(_dot3s(p, p)) for p in p2_s]
    xs = [x + _dot3s(_split2(x), p) for x, p in zip(xs, p4_s)]
    n = lowers[0].shape[0]
    size = 1 << bits
    while size < n:
        parent = (row >> (bits + 1)) == (col >> (bits + 1))
        child = (row >> bits) == (col >> bits)
        off_s = [_split2(jnp.where(parent, jnp.where(child, 0.0, l), 0.0)) for l in lowers]
        x_s = [_split2(x) for x in xs]
        ys = [_dot3s(_split2(_lower_child_rows(x, size)), o) for x, o in zip(xs, off_s)]
        zs = [_dot3s(_split2(y), x) for y, x in zip(ys, x_s)]
        new_xs = []
        for x, z in zip(xs, zs):
            pieces = []
            for k, r in enumerate(range(0, n, 2 * size)):
                pieces.append(x[r:r + size])
                pieces.append(x[r + size:r + 2 * size] - z[k * size:(k + 1) * size])
            new_xs.append(jnp.concatenate(pieces, axis=0))
        xs = new_xs
        bits += 1
        size *= 2
    return xs


def _gdn_kernel(q_ref, k_ref, v_ref, gate_ref, ab_ref, alog_ref, dtb_ref, gn_ref, o_ref, state_ref):
    t = pl.program_id(1)
    tb, width = q_ref.shape
    c = GDN_BLOCK
    n_heads = state_ref.shape[0]
    dk = width // n_heads
    n_chunks = tb // c

    @pl.when(t == 0)
    def _():
        state_ref[...] = jnp.zeros_like(state_ref)

    lane = lax.broadcasted_iota(jnp.int32, (tb, LANES), 1)
    ab = ab_ref[...]
    g_lanes = -jnp.exp(alog_ref[...]) * _softplus(ab + dtb_ref[...])
    beta_lanes = jax.nn.sigmoid(ab)

    def lane_column(x, idx):
        return jnp.broadcast_to(jnp.sum(jnp.where(lane == idx, x, 0.0), axis=1, keepdims=True), (tb, LANES))

    row = lax.broadcasted_iota(jnp.int32, (c, c), 0)
    col = lax.broadcasted_iota(jnp.int32, (c, c), 1)
    incl = row >= col
    strict = row > col
    eye = jnp.where(row == col, 1.0, 0.0).astype(F32)
    tri = jnp.where(incl, 1.0, 0.0).astype(BF16)

    items = []
    for h in range(n_heads):
        g_h = lane_column(g_lanes, h)
        beta_h = lane_column(beta_lanes, h + n_heads)
        cols = slice(h * dk, (h + 1) * dk)
        for n in range(n_chunks):
            rows = slice(n * c, (n + 1) * c)
            items.append(dict(q=q_ref[rows, cols], k=k_ref[rows, cols], v=v_ref[rows, cols],
                              beta=beta_h[rows], g=g_h[rows]))

    def prepare(group):
        for it in group:
            it["gcol"] = _dot_exact_lhs(tri, it["g"])
        for it in group:
            gcol = it["gcol"]
            it["dec"] = jnp.exp(jnp.where(incl, gcol - gcol.T, 0.0))
            it["eg"] = jnp.exp(gcol)
            it["g_last"] = gcol[c - 1:c, :]
            it["kb"] = it["k"] * it["beta"]
            it["kk"] = it["k"].astype(BF16)
        for it in group:
            it["lower"] = jnp.where(strict, _dot_nt(it["kb"].astype(BF16), it["kk"]) * it["dec"], 0.0)
        tinvs = _unit_lower_inverse_all([it["lower"] for it in group], eye, row, col)
        for it, tinv in zip(group, tinvs):
            rhs = jnp.concatenate([it["v"] * it["beta"], it["kb"] * it["eg"]], axis=1)
            uw = _dot3(tinv, rhs)
            it["u"] = uw[:, :dk]
            a = jnp.where(incl, _dot_nt(it["q"].astype(BF16), it["kk"]) * it["dec"], 0.0)
            q_dec = it["q"] * it["eg"]
            k_dec = it["k"] * jnp.exp(it["g_last"] - it["gcol"])
            it["wq"] = jnp.concatenate([uw[:, dk:], q_dec], axis=0).astype(BF16)
            it["ak"] = jnp.concatenate([a, k_dec.T], axis=0).astype(BF16)
            it["decay"] = jnp.exp(it["g_last"])

    early = n_chunks // 2
    prepare([it for i, it in enumerate(items) if i % n_chunks < early])
    prepare([it for i, it in enumerate(items) if i % n_chunks >= early])

    gn = gn_ref[...]
    states = [state_ref[h] for h in range(n_heads)]
    for n in range(n_chunks):
        rows = slice(n * c, (n + 1) * c)
        for h in range(n_heads):
            it = items[h * n_chunks + n]
            cols = slice(h * dk, (h + 1) * dk)
            ws_qs = _dot(it["wq"], states[h].astype(BF16))
            v_new = it["u"] - ws_qs[:c]
            av_kv = _dot(it["ak"], v_new.astype(BF16))
            o = ws_qs[c:] + av_kv[:c]
            states[h] = states[h] * it["decay"] + av_kv[c:]
            ms = jnp.mean(o * o, axis=-1, keepdims=True)
            o = o * lax.rsqrt(ms + EPS) * gn
            o_ref[rows, cols] = (o * _silu(gate_ref[rows, cols])).astype(o_ref.dtype)
    for h in range(n_heads):
        state_ref[h] = states[h]


def _gdn(proj, a_log, dt_bias, gdn_norm, batch, seq):
    n = proj.shape[0]
    width = (proj.shape[1] - LANES) // 4
    n_heads = a_log.shape[0]
    dk = width // n_heads
    tb = GDN_STEP
    nt = seq // tb
    rows = lambda b, t: b * nt + t
    pad = lambda x: jnp.zeros((1, LANES), F32).at[0, :n_heads].set(x)
    col_spec = lambda k: pl.BlockSpec((tb, width), lambda b, t: (rows(b, t), k))
    const = lambda b, t: (0, 0)
    return pl.pallas_call(
        _gdn_kernel,
        grid=(batch, nt),
        in_specs=[col_spec(0), col_spec(1), col_spec(2), col_spec(3),
                  pl.BlockSpec((tb, LANES), lambda b, t: (rows(b, t), 4 * width // LANES)),
                  pl.BlockSpec((1, LANES), const), pl.BlockSpec((1, LANES), const),
                  pl.BlockSpec((1, dk), const)],
        out_specs=pl.BlockSpec((tb, width), lambda b, t: (rows(b, t), 0)),
        out_shape=jax.ShapeDtypeStruct((n, width), BF16),
        scratch_shapes=[pltpu.VMEM((n_heads, dk, dk), F32)],
        compiler_params=_params(2),
        name="gated_delta_rule",
    )(proj, proj, proj, proj, proj, pad(a_log), pad(dt_bias), gdn_norm.reshape(1, dk))


def _sb_kernel(q_ref, k_ref, v_ref, m_ref, o_ref, q2_ref, acc_ref, carry_ref):
    i = pl.program_id(2)
    tq = SB_TILE
    n_tiles = q_ref.shape[0] // tq
    lane = lax.broadcasted_iota(jnp.int32, (tq, LANES), 1)
    suffix = m_ref[...]
    strip = tq // SB_STRIPS

    for t in range(n_tiles):
        q = q_ref[t * tq:(t + 1) * tq, :]
        zero = jnp.zeros_like(q)
        q2_ref[2 * t * tq:(2 * t + 1) * tq, :] = jnp.where(lane < SB_DH, q, zero)
        q2_ref[(2 * t + 1) * tq:(2 * t + 2) * tq, :] = jnp.where(lane >= SB_DH, q, zero)

    def strips_of(t):
        return [slice(2 * t * tq + n * strip, 2 * t * tq + (n + 1) * strip) for n in range(2 * SB_STRIPS)]

    def weigh(pairs):
        work = []
        for t, j, diagonal in pairs:
            start = pl.multiple_of(j * tq, tq)
            k = k_ref[pl.ds(start, tq), :]
            v = v_ref[pl.ds(start, tq), :]
            for n, rows in enumerate(strips_of(t)):
                work.append(dict(n=n, rows=rows, k=k, v=v, diagonal=diagonal))
        for w in work:
            w["y"] = _dot_nt(q2_ref[w["rows"], :], w["k"]) * LOG2_E
        for w in work:
            y = w["y"]
            w["soft"] = jnp.maximum(y, 0.0) + jnp.log2(1.0 + jnp.exp2(-jnp.abs(y)))
            if w["diagonal"]:
                s = lax.broadcasted_iota(jnp.int32, (strip, tq), 1)
                r = lax.broadcasted_iota(jnp.int32, (strip, tq), 0) + (w["n"] % SB_STRIPS) * strip
                w["mask"] = s < r
                w["cost"] = jnp.where(w["mask"], w["soft"], 0.0)
            else:
                w["cost"] = w["soft"]
        for w in work:
            w["later"] = _dot_exact_rhs(w["cost"], suffix)
        for w in work:
            a = jnp.exp2(w["y"] - w["soft"] - w["later"])
            if w["diagonal"]:
                a = jnp.where(w["mask"], a, 0.0)
            w["part"] = _dot(a.astype(BF16), w["v"])
            w["total"] = jnp.sum(w["cost"], axis=1, keepdims=True)
        per_pair = 2 * SB_STRIPS
        return [work[p * per_pair:(p + 1) * per_pair] for p in range(len(pairs))]

    first = i * n_tiles
    pairs = [(t, first + t, True) for t in range(n_tiles)]
    pairs += [(t, jnp.maximum(first + t - 1, 0), False) for t in range(n_tiles)]
    results = weigh(pairs)
    for t in range(n_tiles):
        has_prev = (first + t > 0).astype(F32)
        for d, p in zip(results[t], results[n_tiles + t]):
            acc_ref[d["rows"], :] = d["part"] + p["part"] * (jnp.exp2(-d["total"]) * has_prev)
            carry_ref[d["rows"], :] = d["total"] + p["total"] * has_prev

    @pl.when(jnp.min(carry_ref[...]) < SB_UNDERFLOW_LOG2)
    def _():
        for t in range(n_tiles):
            rows_t = slice(2 * t * tq, 2 * (t + 1) * tq)
            remaining = first + t - 1

            def cond(state):
                step, least = state
                return jnp.logical_and(step < remaining, least < SB_UNDERFLOW_LOG2)

            def body(state, t=t, rows_t=rows_t, remaining=remaining):
                step, _ = state
                (tile,) = weigh([(t, remaining - 1 - step, False)])
                for w in tile:
                    carry = carry_ref[w["rows"], :]
                    acc_ref[w["rows"], :] += w["part"] * jnp.exp2(-carry)
                    carry_ref[w["rows"], :] = carry + w["total"]
                return step + 1, jnp.min(carry_ref[rows_t, :])

            lax.while_loop(cond, body, (jnp.int32(0), jnp.min(carry_ref[rows_t, :])))

    for t in range(n_tiles):
        acc0 = acc_ref[2 * t * tq:(2 * t + 1) * tq, :]
        acc1 = acc_ref[(2 * t + 1) * tq:(2 * t + 2) * tq, :]
        o_ref[t * tq:(t + 1) * tq, :] = jnp.where(lane < SB_DH, acc0, acc1).astype(o_ref.dtype)


def _sb_attention(qkv, suffix, batch, seq):
    n = qkv.shape[0]
    width = qkv.shape[1] // 3
    pairs = width // LANES
    tq = SB_TILE
    rows = SB_TILES_PER_STEP * tq
    nq = seq // rows
    return pl.pallas_call(
        _sb_kernel,
        grid=(batch, pairs, nq),
        in_specs=[pl.BlockSpec((rows, LANES), lambda b, p, i: (b * nq + i, p)),
                  pl.BlockSpec((seq, LANES), lambda b, p, i: (b, pairs + p)),
                  pl.BlockSpec((seq, LANES), lambda b, p, i: (b, 2 * pairs + p)),
                  pl.BlockSpec((tq, tq), lambda b, p, i: (0, 0))],
        out_specs=pl.BlockSpec((rows, LANES), lambda b, p, i: (b * nq + i, p)),
        out_shape=jax.ShapeDtypeStruct((n, width), BF16),
        scratch_shapes=[pltpu.VMEM((2 * rows, LANES), BF16),
                        pltpu.VMEM((2 * rows, LANES), F32),
                        pltpu.VMEM((2 * rows, 1), F32)],
        compiler_params=_params(3),
        name="stick_breaking_attention",
    )(qkv, qkv, qkv, suffix)


def _ret_proj_kernel(x_ref, g_ref, sc_ref, sh_ref, w_ref, cos_ref, sin_ref, qs_ref, ks_ref,
                     q_ref, k_ref, v_ref, gate_ref):
    tm = x_ref.shape[0]
    n_heads, chunk, _ = qs_ref.shape
    w_qk = q_ref.shape[1]
    dk = w_qk // n_heads
    half = dk // 2
    groups = [slice(r0, r0 + chunk) for r0 in range(0, tm, chunk)]
    hbs = [_norm_mod(x_ref[rows, :], g_ref[...], sc_ref[0], sh_ref[0]).astype(BF16) for rows in groups]

    def rot(x, rows, scale):
        cos = cos_ref[rows, :]
        sin = sin_ref[rows, :]
        x1, x2 = x[:, :half], x[:, half:]
        return jnp.concatenate([(x1 * cos - x2 * sin) * scale, (x1 * sin + x2 * cos) * scale], axis=-1)

    for h in range(n_heads):
        cols = slice(h * dk, (h + 1) * dk)
        for rows, hb in zip(groups, hbs):
            q_ref[rows, cols] = rot(_dot(hb, w_ref[:, cols]), rows, qs_ref[h]).astype(BF16)
    for h in range(n_heads):
        cols = slice(h * dk, (h + 1) * dk)
        for rows, hb in zip(groups, hbs):
            pre = _dot(hb, w_ref[:, w_qk + h * dk:w_qk + (h + 1) * dk])
            k_ref[rows, cols] = rot(pre, rows, ks_ref[h]).astype(BF16)
    step = 512
    off = 2 * w_qk
    for ref in (v_ref, gate_ref):
        for o in range(0, ref.shape[1], step):
            for rows, hb in zip(groups, hbs):
                ref[rows, o:o + step] = _dot(hb, w_ref[:, off + o:off + o + step]).astype(ref.dtype)
        off += ref.shape[1]


def _ret_proj(x, gain, sc, sh, w, cos, sin, q_scale, k_scale, n_qk, n_v, seq):
    n, d = x.shape
    tm = ROW_TILE_PROJ
    per_b = seq // tm
    bmap = lambda i: (i // per_b, 0, 0)
    row = lambda i: (i, 0)
    const = lambda i: (0, 0)
    pos = lambda i: (i % per_b, 0)
    return pl.pallas_call(
        _ret_proj_kernel,
        grid=(n // tm,),
        in_specs=[pl.BlockSpec((tm, d), row),
                  pl.BlockSpec((1, d), const),
                  pl.BlockSpec((1, 1, d), bmap),
                  pl.BlockSpec((1, 1, d), bmap),
                  _resident(w.shape, const),
                  pl.BlockSpec((tm, cos.shape[1]), pos),
                  pl.BlockSpec((tm, cos.shape[1]), pos),
                  pl.BlockSpec(q_scale.shape, lambda i: (0, 0, 0)),
                  pl.BlockSpec(k_scale.shape, lambda i: (0, 0, 0))],
        out_specs=[pl.BlockSpec((tm, n_qk), row),
                   pl.BlockSpec((tm, n_qk), row),
                   pl.BlockSpec((tm, n_v), row),
                   pl.BlockSpec((tm, n_v), row)],
        out_shape=[jax.ShapeDtypeStruct((n, n_qk), BF16),
                   jax.ShapeDtypeStruct((n, n_qk), BF16),
                   jax.ShapeDtypeStruct((n, n_v), BF16),
                   jax.ShapeDtypeStruct((n, n_v), F32)],
        compiler_params=_params(1),
        name="retention_in_proj",
    )(x, gain, sc, sh, w, cos, sin, q_scale, k_scale)


def _ret_kernel(q_ref, k_ref, v_ref, gate_ref, decay_ref, o_ref, state_ref):
    t = pl.program_id(1)
    c = RET_CHUNK
    n_heads = state_ref.shape[0]
    dk = q_ref.shape[1] // n_heads
    dv = v_ref.shape[1] // n_heads

    @pl.when(t == 0)
    def _():
        state_ref[...] = jnp.zeros_like(state_ref)

    row = lax.broadcasted_iota(jnp.int32, (c, c), 0)
    col = lax.broadcasted_iota(jnp.int32, (c, c), 1)
    causal = row >= col

    for r0 in range(0, q_ref.shape[0], c):
        rows = slice(r0, r0 + c)
        for h in range(n_heads):
            q = q_ref[rows, h * dk:(h + 1) * dk]
            k = k_ref[rows, h * dk:(h + 1) * dk]
            v = v_ref[rows, h * dv:(h + 1) * dv]
            intra = jnp.where(causal, _dot_nt(q, k), 0.0)
            state = state_ref[h]
            o = _dot(intra.astype(BF16), v) + _dot(q, state.astype(BF16))
            k_t = k.astype(F32).T.astype(BF16)
            state_ref[h] = (state + _dot(k_t, v)) * decay_ref[h]
            ms = jnp.mean(o * o, axis=-1, keepdims=True)
            o = o * lax.rsqrt(ms + EPS)
            gate = gate_ref[rows, h * dv:(h + 1) * dv]
            o_ref[rows, h * dv:(h + 1) * dv] = (o * _silu(gate)).astype(o_ref.dtype)


def _retention(q, k, v, gate, chunk_decay, batch, seq):
    n, w_qk = q.shape
    n_heads = chunk_decay.shape[0]
    w_v = v.shape[1]
    c = RET_CHUNKS_PER_STEP * RET_CHUNK
    nt = seq // c
    rows = lambda b, t: (b * nt + t, 0)
    return pl.pallas_call(
        _ret_kernel,
        grid=(batch, nt),
        in_specs=[pl.BlockSpec((c, w_qk), rows),
                  pl.BlockSpec((c, w_qk), rows),
                  pl.BlockSpec((c, w_v), rows),
                  pl.BlockSpec((c, w_v), rows),
                  pl.BlockSpec(chunk_decay.shape, lambda b, t: (0, 0, 0))],
        out_specs=pl.BlockSpec((c, w_v), rows),
        out_shape=jax.ShapeDtypeStruct((n, w_v), BF16),
        scratch_shapes=[pltpu.VMEM((n_heads, w_qk // n_heads, w_v // n_heads), F32)],
        compiler_params=_params(2),
        name="retention_chunkwise",
    )(q, k, v, gate, chunk_decay)


def _post_kernel(*refs, n_mix):
    x_ref = refs[0]
    o_refs = refs[1:1 + n_mix]
    wm_refs = refs[1 + n_mix:1 + 2 * n_mix]
    gtm_ref, g_ref, sc_ref, sh_ref, gtf_ref, win_ref, wout_ref, out_ref, acc_ref = refs[1 + 2 * n_mix:]
    y = _dot(o_refs[0][...], wm_refs[0][...])
    for o_ref, w_ref in zip(o_refs[1:], wm_refs[1:]):
        y = y + _dot(o_ref[...], w_ref[...])
    x1 = x_ref[...] + gtm_ref[0] * y
    hb = _norm_mod(x1, g_ref[...], sc_ref[0], sh_ref[0]).astype(BF16)
    hidden = wout_ref.shape[0]
    for idx, off in enumerate(range(0, hidden, FFN_CHUNK)):
        g = _dot(hb, win_ref[:, off:off + FFN_CHUNK])
        u = _dot(hb, win_ref[:, hidden + off:hidden + off + FFN_CHUNK])
        part = _dot((_silu(g) * u).astype(BF16), wout_ref[off:off + FFN_CHUNK, :])
        if idx == 0:
            acc_ref[...] = part
        else:
            acc_ref[...] += part
    out_ref[...] = x1 + gtf_ref[0] * acc_ref[...]


def _post(x, mix_outs, mix_ws, gt_m, gain, sc, sh, gt_f, w_in, w_out, seq):
    n, d = x.shape
    tm = ROW_TILE_FFN
    per_b = seq // tm
    bmap = lambda i: (i // per_b, 0, 0)
    row = lambda i: (i, 0)
    const = lambda i: (0, 0)
    vec = pl.BlockSpec((1, 1, d), bmap)
    n_mix = len(mix_outs)
    return pl.pallas_call(
        functools.partial(_post_kernel, n_mix=n_mix),
        grid=(n // tm,),
        in_specs=([pl.BlockSpec((tm, d), row)]
                  + [pl.BlockSpec((tm, o.shape[1]), row) for o in mix_outs]
                  + [_resident(w.shape, const) for w in mix_ws]
                  + [vec, pl.BlockSpec((1, d), const), vec, vec, vec,
                     _resident(w_in.shape, const), _resident(w_out.shape, const)]),
        out_specs=pl.BlockSpec((tm, d), row),
        out_shape=jax.ShapeDtypeStruct((n, d), F32),
        scratch_shapes=[pltpu.VMEM((tm, d), F32)],
        compiler_params=_params(1),
        name="out_proj_swiglu",
    )(x, *mix_outs, *mix_ws, gt_m, gain, sc, sh, gt_f, w_in, w_out)


def kernel(x, c, ada_w, ada_b, norm_mix, norm_ffn, hyb_w_in, hyb_conv, gdn_a_log, gdn_dt_bias, gdn_norm, sb_q_norm, sb_k_norm, hyb_w_out, ret_w_in, ret_w_out, ffn_w_in, ffn_w_out):
    batch, seq, d = x.shape
    depth = ada_w.shape[0]
    gdn_heads = gdn_a_log.shape[1]
    gdn_dv = gdn_norm.shape[1]
    gdn_w = gdn_heads * gdn_dv
    sb_dh = sb_q_norm.shape[1]
    assert sb_dh == SB_DH
    sb_w = hyb_w_out.shape[1] - gdn_w
    sb_heads = sb_w // sb_dh
    ret_heads = 4
    ret_qk = ret_w_in.shape[2] // 6
    ret_v = 2 * ret_qk
    ret_dk = ret_qk // ret_heads

    xf = x.reshape(batch * seq, d)
    mod = _modulation(c, ada_w, ada_b)

    head_mean = jnp.kron(jnp.eye(MXU_DIM // sb_dh, dtype=F32), jnp.full((sb_dh, sb_dh), 1.0 / sb_dh, F32)).astype(BF16)
    idx = jnp.arange(SB_TILE)
    suffix = (idx[:, None] > idx[None, :]).astype(BF16)
    pos = jnp.arange(seq, dtype=F32)
    inv = 1.0 / (ROPE_BASE ** (jnp.arange(0, ret_dk, 2, dtype=F32) / ret_dk))
    ang = pos[:, None] * inv[None, :]
    cos, sin = jnp.cos(ang), jnp.sin(ang)
    log_gamma = jnp.log1p(-jnp.exp2(-5.0 - jnp.arange(ret_heads, dtype=F32)))
    steps = jnp.arange(1, RET_CHUNK + 1, dtype=F32)
    ret_q_scale = jnp.broadcast_to(jnp.exp(log_gamma[:, None] * steps[None, :])[:, :, None],
                                   (ret_heads, RET_CHUNK, ret_dk // 2))
    ret_k_scale = jnp.broadcast_to((jnp.exp(-log_gamma[:, None] * steps[None, :]) * ret_dk ** -0.5)[:, :, None],
                                   (ret_heads, RET_CHUNK, ret_dk // 2))
    ret_chunk_decay = jnp.broadcast_to(jnp.exp(log_gamma * RET_CHUNK)[:, None, None], (ret_heads, 1, ret_v // ret_heads))

    for l in range(depth):
        sh_m, sc_m, gt_m, sh_f, sc_f, gt_f = [m.reshape(batch, 1, d) for m in jnp.split(mod[l], 6, axis=-1)]
        i = l // 2
        gain_m = norm_mix[l].reshape(1, d)
        if l % 2 == 0:
            w_in = hyb_w_in[i]
            n_main = 4 * gdn_w
            n_ab = 2 * gdn_heads
            ab_cols = jnp.zeros((d, LANES), F32).at[:, :n_ab].set(w_in[:, n_main:n_main + n_ab])
            w_re = jnp.concatenate([w_in[:, :n_main], ab_cols, w_in[:, n_main + n_ab:]], axis=1).astype(BF16)
            gdn_proj, sb_qkv = _hyb_proj(
                xf, gain_m, sc_m, sh_m, w_re, hyb_conv[i], head_mean,
                jnp.tile(sb_q_norm[i], sb_heads).reshape(1, sb_w),
                jnp.tile(sb_k_norm[i], sb_heads).reshape(1, sb_w), seq, gdn_dv)
            o_a = _gdn(gdn_proj, gdn_a_log[i], gdn_dt_bias[i], gdn_norm[i], batch, seq)
            o_b = _sb_attention(sb_qkv, suffix, batch, seq)
            w_out = hyb_w_out[i].astype(BF16)
            mix_outs = [o_a, o_b]
            mix_ws = [w_out[:gdn_w], w_out[gdn_w:]]
        else:
            rq, rk, rv, gate = _ret_proj(xf, gain_m, sc_m, sh_m, ret_w_in[i].astype(BF16), cos, sin,
                                         ret_q_scale, ret_k_scale, ret_qk, ret_v, seq)
            o_r = _retention(rq, rk, rv, gate, ret_chunk_decay, batch, seq)
            mix_outs = [o_r]
            mix_ws = [ret_w_out[i].astype(BF16)]
        xf = _post(xf, mix_outs, mix_ws, gt_m, norm_ffn[l].reshape(1, d), sc_f, sh_f, gt_f,
                   ffn_w_in[l].astype(BF16), ffn_w_out[l].astype(BF16), seq)
    return xf.reshape(batch, seq, d)
```

```python
import functools

import jax
import jax.numpy as jnp
from jax import lax
from jax.experimental import pallas as pl
from jax.experimental.pallas import tpu as pltpu

F32 = jnp.float32
BF16 = jnp.bfloat16
EPS = 1e-6
ROPE_BASE = 10000.0

LANES = 128
MXU_DIM = 256
VMEM_LIMIT = 56 * 1024 * 1024

GDN_CONV = 4
GDN_BLOCK = 128
GDN_STEP = 512
SB_TILE = 256
SB_DH = 64
SB_STRIPS = 2
SB_TILES_PER_STEP = 4
LOG2_E = 1.4426950408889634
SB_UNDERFLOW_LOG2 = 180.0
RET_CHUNK = 256
RET_CHUNKS_PER_STEP = 2
ROW_TILE_PROJ = 512
PROJ_GROUP = 256
ROW_TILE_FFN = 512
FFN_CHUNK = 256
CONV_PAD = 8


def _dot(a, b):
    return jnp.dot(a, b, preferred_element_type=F32)


def _dot_nt(a, b):
    return lax.dot_general(a, b, (((1,), (1,)), ((), ())), preferred_element_type=F32)


def _split2(a):
    hi = a.astype(BF16)
    lo = (a - hi.astype(F32)).astype(BF16)
    return hi, lo


def _dot3(a, b):
    ah, al = _split2(a)
    bh, bl = _split2(b)
    n = b.shape[1]
    if 2 * n <= MXU_DIM:
        both = _dot(ah, jnp.concatenate([bh, bl], axis=1))
        return (both[:, :n] + both[:, n:]) + _dot(al, bh)
    return _dot(ah, bh) + (_dot(ah, bl) + _dot(al, bh))


def _dot_exact_rhs(a, m):
    ah, al = _split2(a)
    return _dot(ah, m) + _dot(al, m)


def _dot_exact_lhs(m, a):
    ah, al = _split2(a)
    n = a.shape[1]
    if 2 * n <= MXU_DIM:
        both = _dot(m, jnp.concatenate([ah, al], axis=1))
        return both[:, :n] + both[:, n:]
    return _dot(m, ah) + _dot(m, al)


def _silu(x):
    return x * jax.nn.sigmoid(x)


def _softplus(x):
    return jnp.maximum(x, 0.0) + jnp.log1p(jnp.exp(-jnp.abs(x)))


def _params(n_axes):
    return pltpu.CompilerParams(dimension_semantics=("arbitrary",) * n_axes,
                                vmem_limit_bytes=VMEM_LIMIT)


def _resident(shape, index_map):
    return pl.BlockSpec(shape, index_map, pipeline_mode=pl.Buffered(1))


def _mod_kernel(c_ref, w_ref, b_ref, o_ref):
    c = c_ref[...]
    o_ref[0] = _dot(_silu(c).astype(BF16), w_ref[0].astype(BF16)) + b_ref[0]


def _modulation(c, ada_w, ada_b):
    depth, d, n = ada_w.shape
    b = c.shape[0]
    rows = 8
    c_pad = jnp.zeros((rows, d), F32).at[:b].set(c)
    tn = 1024
    out = pl.pallas_call(
        _mod_kernel,
        grid=(depth, n // tn),
        in_specs=[pl.BlockSpec((rows, d), lambda l, j: (0, 0)),
                  pl.BlockSpec((1, d, tn), lambda l, j: (l, 0, j)),
                  pl.BlockSpec((1, 1, tn), lambda l, j: (l, 0, j))],
        out_specs=pl.BlockSpec((1, rows, tn), lambda l, j: (l, 0, j)),
        out_shape=jax.ShapeDtypeStruct((depth, rows, n), F32),
        compiler_params=_params(2),
        name="adaln_modulation",
    )(c_pad, ada_w, ada_b.reshape(depth, 1, n))
    return out[:, :b]


def _norm_mod(x, gain, sc, sh):
    ms = jnp.mean(x * x, axis=-1, keepdims=True)
    y = x * lax.rsqrt(ms + EPS)
    y = y * gain
    return y * (1.0 + sc) + sh


def _hyb_proj_kernel(x_ref, g_ref, sc_ref, sh_ref, w_ref, cw_ref, gm_ref, qg_ref, kg_ref,
                     gdn_ref, sb_ref, ext_ref, *, per_batch, dk):
    i = pl.program_id(0)
    tm = x_ref.shape[0]
    n_conv = cw_ref.shape[1]
    n_gate = gdn_ref.shape[1] - n_conv - LANES
    w_sb = qg_ref.shape[1]
    groups = [slice(r0, r0 + PROJ_GROUP) for r0 in range(0, tm, PROJ_GROUP)]
    hbs = [_norm_mod(x_ref[rows, :], g_ref[...], sc_ref[0], sh_ref[0]).astype(BF16) for rows in groups]

    @pl.when(i % per_batch == 0)
    def _():
        ext_ref[0:CONV_PAD, :] = jnp.zeros((CONV_PAD, n_conv), F32)

    def l2n(y):
        return y * lax.rsqrt(jnp.sum(y * y, axis=-1, keepdims=True) + EPS)

    for c0 in range(0, n_conv, MXU_DIM):
        for rows, hb in zip(groups, hbs):
            ext_ref[CONV_PAD + rows.start:CONV_PAD + rows.stop, c0:c0 + MXU_DIM] = _dot(hb, w_ref[:, c0:c0 + MXU_DIM])

    def conv_chunk(c0, rows):
        cols = slice(c0, c0 + MXU_DIM)
        w = cw_ref[:, cols]
        ext = ext_ref[rows.start:CONV_PAD + rows.stop, cols]
        y = ext[CONV_PAD:] * w[GDN_CONV - 1:GDN_CONV, :]
        for j in range(GDN_CONV - 1):
            shift = GDN_CONV - 1 - j
            y = y + pltpu.roll(ext, shift, 0)[CONV_PAD:] * w[j:j + 1, :]
        y = _silu(y)
        if c0 < 2 * n_gate:
            y = jnp.concatenate([l2n(y[:, h0:h0 + dk]) for h0 in range(0, MXU_DIM, dk)], axis=1)
            if c0 < n_gate:
                y = y * (dk ** -0.5)
        gdn_ref[rows, cols] = y

    def head_rms(xf, gain):
        sq = xf * xf
        ms = jnp.concatenate([_dot_exact_rhs(sq[:, c0:c0 + MXU_DIM], gm_ref[...])
                              for c0 in range(0, xf.shape[1], MXU_DIM)], axis=1)
        return xf * lax.rsqrt(ms + EPS) * gain

    off_gate = n_conv
    off_ab = off_gate + n_gate
    off_q = off_ab + LANES
    off_k = off_q + w_sb
    off_v = off_k + w_sb
    conv_starts = list(range(0, n_conv, MXU_DIM))
    for rows, hb in zip(groups, hbs):
        gdn_ref[rows, off_gate:off_ab] = _dot(hb, w_ref[:, off_gate:off_gate + n_gate])
    for rows, hb in zip(groups, hbs):
        sq = _dot(hb, w_ref[:, off_q:off_q + w_sb])
        sb_ref[rows, 0:w_sb] = (head_rms(sq, qg_ref[...]) * (SB_DH ** -0.5)).astype(BF16)
    for rows, hb in zip(groups, hbs):
        sk = _dot(hb, w_ref[:, off_k:off_k + w_sb])
        sb_ref[rows, w_sb:2 * w_sb] = head_rms(sk, kg_ref[...]).astype(BF16)
    for rows, hb in zip(groups, hbs):
        sb_ref[rows, 2 * w_sb:3 * w_sb] = _dot(hb, w_ref[:, off_v:off_v + w_sb]).astype(BF16)
        gdn_ref[rows, off_ab:off_q] = _dot(hb, w_ref[:, off_ab:off_ab + LANES])
    for c0 in conv_starts:
        for rows in groups:
            conv_chunk(c0, rows)
    ext_ref[0:CONV_PAD, :] = ext_ref[tm:tm + CONV_PAD, :]


def _hyb_proj(x, gain, sc, sh, w, conv_w, gmat, qg, kg, seq, gdn_dk):
    n, d = x.shape
    tm = ROW_TILE_PROJ
    per_b = seq // tm
    w_sb = qg.shape[1]
    n_conv = conv_w.shape[1]
    n_gate = w.shape[1] - n_conv - LANES - 3 * w_sb
    bmap = lambda i: (i // per_b, 0, 0)
    row = lambda i: (i, 0)
    const = lambda i: (0, 0)
    return pl.pallas_call(
        functools.partial(_hyb_proj_kernel, per_batch=per_b, dk=gdn_dk),
        grid=(n // tm,),
        in_specs=[pl.BlockSpec((tm, d), row),
                  pl.BlockSpec((1, d), const),
                  pl.BlockSpec((1, 1, d), bmap),
                  pl.BlockSpec((1, 1, d), bmap),
                  _resident(w.shape, const),
                  pl.BlockSpec(conv_w.shape, const),
                  _resident(gmat.shape, const),
                  pl.BlockSpec((1, w_sb), const),
                  pl.BlockSpec((1, w_sb), const)],
        out_specs=[pl.BlockSpec((tm, n_conv + n_gate + LANES), row),
                   pl.BlockSpec((tm, 3 * w_sb), row)],
        out_shape=[jax.ShapeDtypeStruct((n, n_conv + n_gate + LANES), F32),
                   jax.ShapeDtypeStruct((n, 3 * w_sb), BF16)],
        scratch_shapes=[pltpu.VMEM((CONV_PAD + tm, n_conv), F32)],
        compiler_params=_params(1),
        name="hybrid_in_proj",
    )(x, gain, sc, sh, w, conv_w, gmat, qg, kg)


def _dot3s(a, b):
    ah, al = a
    bh, bl = b
    n = bh.shape[1]
    both = _dot(ah, jnp.concatenate([bh, bl], axis=1))
    return (both[:, :n] + both[:, n:]) + _dot(al, bh)


def _lower_child_rows(x, size):
    return jnp.concatenate([x[r + size:r + 2 * size] for r in range(0, x.shape[0], 2 * size)], axis=0)


def _unit_lower_inverse_all(lowers, eye, row, col):
    bits = 3
    same = (row >> bits) == (col >> bits)
    diags = [jnp.where(same, l, 0.0) for l in lowers]
    d_s = [_split2(d) for d in diags]
    x0_s = [_split2(eye - d) for d in diags]
    p2_s = [_split2(_dot3s(d, d)) for d in d_s]
    xs = [(eye - d) + _dot3s(x, p) for d, x, p in zip(diags, x0_s, p2_s)]
    p4_s = [_split2(_dot3s(p, p)) for p in p2_s]
    xs = [x + _dot3s(_split2(x), p) for x, p in zip(xs, p4_s)]
    n = lowers[0].shape[0]
    size = 1 << bits
    while size < n:
        parent = (row >> (bits + 1)) == (col >> (bits + 1))
        child = (row >> bits) == (col >> bits)
        off_s = [_split2(jnp.where(parent, jnp.where(child, 0.0, l), 0.0)) for l in lowers]
        x_s = [_split2(x) for x in xs]
        ys = [_dot3s(_split2(_lower_child_rows(x, size)), o) for x, o in zip(xs, off_s)]
        zs = [_dot3s(_split2(y), x) for y, x in zip(ys, x_s)]
        new_xs = []
        for x, z in zip(xs, zs):
            pieces = []
            for k, r in enumerate(range(0, n, 2 * size)):
                pieces.append(x[r:r + size])
                pieces.append(x[r + size:r + 2 * size] - z[k * size:(k + 1) * size])
            new_xs.append(jnp.concatenate(pieces, axis=0))
        xs = new_xs
        bits += 1
        size *= 2
    return xs


def _gdn_kernel(q_ref, k_ref, v_ref, gate_ref, ab_ref, alog_ref, dtb_ref, gn_ref, o_ref, state_ref):
    t = pl.program_id(1)
    tb, width = q_ref.shape
    c = GDN_BLOCK
    n_heads = state_ref.shape[0]
    dk = width // n_heads
    n_chunks = tb // c

    @pl.when(t == 0)
    def _():
        state_ref[...] = jnp.zeros_like(state_ref)

    lane = lax.broadcasted_iota(jnp.int32, (tb, LANES), 1)
    ab = ab_ref[...]
    g_lanes = -jnp.exp(alog_ref[...]) * _softplus(ab + dtb_ref[...])
    beta_lanes = jax.nn.sigmoid(ab)

    def lane_column(x, idx):
        return jnp.broadcast_to(jnp.sum(jnp.where(lane == idx, x, 0.0), axis=1, keepdims=True), (tb, LANES))

    row = lax.broadcasted_iota(jnp.int32, (c, c), 0)
    col = lax.broadcasted_iota(jnp.int32, (c, c), 1)
    incl = row >= col
    strict = row > col
    eye = jnp.where(row == col, 1.0, 0.0).astype(F32)
    tri = jnp.where(incl, 1.0, 0.0).astype(BF16)

    items = []
    for h in range(n_heads):
        g_h = lane_column(g_lanes, h)
        beta_h = lane_column(beta_lanes, h + n_heads)
        cols = slice(h * dk, (h + 1) * dk)
        for n in range(n_chunks):
            rows = slice(n * c, (n + 1) * c)
            items.append(dict(q=q_ref[rows, cols], k=k_ref[rows, cols], v=v_ref[rows, cols],
                              beta=beta_h[rows], g=g_h[rows]))

    def prepare(group):
        for it in group:
            it["gcol"] = _dot_exact_lhs(tri, it["g"])
        for it in group:
            gcol = it["gcol"]
            it["dec"] = jnp.exp(jnp.where(incl, gcol - gcol.T, 0.0))
            it["eg"] = jnp.exp(gcol)
            it["g_last"] = gcol[c - 1:c, :]
            it["kb"] = it["k"] * it["beta"]
            it["kk"] = it["k"].astype(BF16)
        for it in group:
            it["lower"] = jnp.where(strict, _dot_nt(it["kb"].astype(BF16), it["kk"]) * it["dec"], 0.0)
        tinvs = _unit_lower_inverse_all([it["lower"] for it in group], eye, row, col)
        for it, tinv in zip(group, tinvs):
            rhs = jnp.concatenate([it["v"] * it["beta"], it["kb"] * it["eg"]], axis=1)
            uw = _dot3(tinv, rhs)
            it["u"] = uw[:, :dk]
            a = jnp.where(incl, _dot_nt(it["q"].astype(BF16), it["kk"]) * it["dec"], 0.0)
            q_dec = it["q"] * it["eg"]
            k_dec = it["k"] * jnp.exp(it["g_last"] - it["gcol"])
            it["wq"] = jnp.concatenate([uw[:, dk:], q_dec], axis=0).astype(BF16)
            it["ak"] = jnp.concatenate([a, k_dec.T], axis=0).astype(BF16)
            it["decay"] = jnp.exp(it["g_last"])

    early = n_chunks // 2
    prepare([it for i, it in enumerate(items) if i % n_chunks < early])
    prepare([it for i, it in enumerate(items) if i % n_chunks >= early])

    gn = gn_ref[...]
    states = [state_ref[h] for h in range(n_heads)]
    for n in range(n_chunks):
        rows = slice(n * c, (n + 1) * c)
        for h in range(n_heads):
            it = items[h * n_chunks + n]
            cols = slice(h * dk, (h + 1) * dk)
            ws_qs = _dot(it["wq"], states[h].astype(BF16))
            v_new = it["u"] - ws_qs[:c]
            av_kv = _dot(it["ak"], v_new.astype(BF16))
            o = ws_qs[c:] + av_kv[:c]
            states[h] = states[h] * it["decay"] + av_kv[c:]
            ms = jnp.mean(o * o, axis=-1, keepdims=True)
            o = o * lax.rsqrt(ms + EPS) * gn
            o_ref[rows, cols] = (o * _silu(gate_ref[rows, cols])).astype(o_ref.dtype)
    for h in range(n_heads):
        state_ref[h] = states[h]


def _gdn(proj, a_log, dt_bias, gdn_norm, batch, seq):
    n = proj.shape[0]
    width = (proj.shape[1] - LANES) // 4
    n_heads = a_log.shape[0]
    dk = width // n_heads
    tb = GDN_STEP
    nt = seq // tb
    rows = lambda b, t: b * nt + t
    pad = lambda x: jnp.zeros((1, LANES), F32).at[0, :n_heads].set(x)
    col_spec = lambda k: pl.BlockSpec((tb, width), lambda b, t: (rows(b, t), k))
    const = lambda b, t: (0, 0)
    return pl.pallas_call(
        _gdn_kernel,
        grid=(batch, nt),
        in_specs=[col_spec(0), col_spec(1), col_spec(2), col_spec(3),
                  pl.BlockSpec((tb, LANES), lambda b, t: (rows(b, t), 4 * width // LANES)),
                  pl.BlockSpec((1, LANES), const), pl.BlockSpec((1, LANES), const),
                  pl.BlockSpec((1, dk), const)],
        out_specs=pl.BlockSpec((tb, width), lambda b, t: (rows(b, t), 0)),
        out_shape=jax.ShapeDtypeStruct((n, width), BF16),
        scratch_shapes=[pltpu.VMEM((n_heads, dk, dk), F32)],
        compiler_params=_params(2),
        name="gated_delta_rule",
    )(proj, proj, proj, proj, proj, pad(a_log), pad(dt_bias), gdn_norm.reshape(1, dk))


def _sb_kernel(q_ref, k_ref, v_ref, m_ref, o_ref, q2_ref, acc_ref, carry_ref):
    i = pl.program_id(2)
    tq = SB_TILE
    n_tiles = q_ref.shape[0] // tq
    lane = lax.broadcasted_iota(jnp.int32, (tq, LANES), 1)
    suffix = m_ref[...]
    strip = tq // SB_STRIPS

    for t in range(n_tiles):
        q = q_ref[t * tq:(t + 1) * tq, :]
        zero = jnp.zeros_like(q)
        q2_ref[2 * t * tq:(2 * t + 1) * tq, :] = jnp.where(lane < SB_DH, q, zero)
        q2_ref[(2 * t + 1) * tq:(2 * t + 2) * tq, :] = jnp.where(lane >= SB_DH, q, zero)

    def strips_of(t):
        return [slice(2 * t * tq + n * strip, 2 * t * tq + (n + 1) * strip) for n in range(2 * SB_STRIPS)]

    def weigh(pairs):
        work = []
        for t, j, diagonal in pairs:
            start = pl.multiple_of(j * tq, tq)
            k = k_ref[pl.ds(start, tq), :]
            v = v_ref[pl.ds(start, tq), :]
            for n, rows in enumerate(strips_of(t)):
                work.append(dict(n=n, rows=rows, k=k, v=v, diagonal=diagonal))
        for w in work:
            w["y"] = _dot_nt(q2_ref[w["rows"], :], w["k"]) * LOG2_E
        for w in work:
            y = w["y"]
            w["soft"] = jnp.maximum(y, 0.0) + jnp.log2(1.0 + jnp.exp2(-jnp.abs(y)))
            if w["diagonal"]:
                s = lax.broadcasted_iota(jnp.int32, (strip, tq), 1)
                r = lax.broadcasted_iota(jnp.int32, (strip, tq), 0) + (w["n"] % SB_STRIPS) * strip
                w["mask"] = s < r
                w["cost"] = jnp.where(w["mask"], w["soft"], 0.0)
            else:
                w["cost"] = w["soft"]
        for w in work:
            w["later"] = _dot_exact_rhs(w["cost"], suffix)
        for w in work:
            a = jnp.exp2(w["y"] - w["soft"] - w["later"])
            if w["diagonal"]:
                a = jnp.where(w["mask"], a, 0.0)
            w["part"] = _dot(a.astype(BF16), w["v"])
            w["total"] = jnp.sum(w["cost"], axis=1, keepdims=True)
        per_pair = 2 * SB_STRIPS
        return [work[p * per_pair:(p + 1) * per_pair] for p in range(len(pairs))]

    first = i * n_tiles
    pairs = [(t, first + t, True) for t in range(n_tiles)]
    pairs += [(t, jnp.maximum(first + t - 1, 0), False) for t in range(n_tiles)]
    results = weigh(pairs)
    for t in range(n_tiles):
        has_prev = (first + t > 0).astype(F32)
        for d, p in zip(results[t], results[n_tiles + t]):
            acc_ref[d["rows"], :] = d["part"] + p["part"] * (jnp.exp2(-d["total"]) * has_prev)
            carry_ref[d["rows"], :] = d["total"] + p["total"] * has_prev

    @pl.when(jnp.min(carry_ref[...]) < SB_UNDERFLOW_LOG2)
    def _():
        for t in range(n_tiles):
            rows_t = slice(2 * t * tq, 2 * (t + 1) * tq)
            remaining = first + t - 1

            def cond(state):
                step, least = state
                return jnp.logical_and(step < remaining, least < SB_UNDERFLOW_LOG2)

            def body(state, t=t, rows_t=rows_t, remaining=remaining):
                step, _ = state
                (tile,) = weigh([(t, remaining - 1 - step, False)])
                for w in tile:
                    carry = carry_ref[w["rows"], :]
                    acc_ref[w["rows"], :] += w["part"] * jnp.exp2(-carry)
                    carry_ref[w["rows"], :] = carry + w["total"]
                return step + 1, jnp.min(carry_ref[rows_t, :])

            lax.while_loop(cond, body, (jnp.int32(0), jnp.min(carry_ref[rows_t, :])))

    for t in range(n_tiles):
        acc0 = acc_ref[2 * t * tq:(2 * t + 1) * tq, :]
        acc1 = acc_ref[(2 * t + 1) * tq:(2 * t + 2) * tq, :]
        o_ref[t * tq:(t + 1) * tq, :] = jnp.where(lane < SB_DH, acc0, acc1).astype(o_ref.dtype)


def _sb_attention(qkv, suffix, batch, seq):
    n = qkv.shape[0]
    width = qkv.shape[1] // 3
    pairs = width // LANES
    tq = SB_TILE
    rows = SB_TILES_PER_STEP * tq
    nq = seq // rows
    return pl.pallas_call(
        _sb_kernel,
        grid=(batch, pairs, nq),
        in_specs=[pl.BlockSpec((rows, LANES), lambda b, p, i: (b * nq + i, p)),
                  pl.BlockSpec((seq, LANES), lambda b, p, i: (b, pairs + p)),
                  pl.BlockSpec((seq, LANES), lambda b, p, i: (b, 2 * pairs + p)),
                  pl.BlockSpec((tq, tq), lambda b, p, i: (0, 0))],
        out_specs=pl.BlockSpec((rows, LANES), lambda b, p, i: (b * nq + i, p)),
        out_shape=jax.ShapeDtypeStruct((n, width), BF16),
        scratch_shapes=[pltpu.VMEM((2 * rows, LANES), BF16),
                        pltpu.VMEM((2 * rows, LANES), F32),
                        pltpu.VMEM((2 * rows, 1), F32)],
        compiler_params=_params(3),
        name="stick_breaking_attention",
    )(qkv, qkv, qkv, suffix)


def _ret_proj_kernel(x_ref, g_ref, sc_ref, sh_ref, w_ref, cos_ref, sin_ref, qs_ref, ks_ref,
                     q_ref, k_ref, v_ref, gate_ref):
    tm = x_ref.shape[0]
    n_heads, chunk, _ = qs_ref.shape
    w_qk = q_ref.shape[1]
    dk = w_qk // n_heads
    half = dk // 2
    groups = [slice(r0, r0 + chunk) for r0 in range(0, tm, chunk)]
    hbs = [_norm_mod(x_ref[rows, :], g_ref[...], sc_ref[0], sh_ref[0]).astype(BF16) for rows in groups]

    def rot(x, rows, scale):
        cos = cos_ref[rows, :]
        sin = sin_ref[rows, :]
        x1, x2 = x[:, :half], x[:, half:]
        return jnp.concatenate([(x1 * cos - x2 * sin) * scale, (x1 * sin + x2 * cos) * scale], axis=-1)

    for h in range(n_heads):
        cols = slice(h * dk, (h + 1) * dk)
        for rows, hb in zip(groups, hbs):
            q_ref[rows, cols] = rot(_dot(hb, w_ref[:, cols]), rows, qs_ref[h]).astype(BF16)
    for h in range(n_heads):
        cols = slice(h * dk, (h + 1) * dk)
        for rows, hb in zip(groups, hbs):
            pre = _dot(hb, w_ref[:, w_qk + h * dk:w_qk + (h + 1) * dk])
            k_ref[rows, cols] = rot(pre, rows, ks_ref[h]).astype(BF16)
    step = 512
    off = 2 * w_qk
    for ref in (v_ref, gate_ref):
        for o in range(0, ref.shape[1], step):
            for rows, hb in zip(groups, hbs):
                ref[rows, o:o + step] = _dot(hb, w_ref[:, off + o:off + o + step]).astype(ref.dtype)
        off += ref.shape[1]


def _ret_proj(x, gain, sc, sh, w, cos, sin, q_scale, k_scale, n_qk, n_v, seq):
    n, d = x.shape
    tm = ROW_TILE_PROJ
    per_b = seq // tm
    bmap = lambda i: (i // per_b, 0, 0)
    row = lambda i: (i, 0)
    const = lambda i: (0, 0)
    pos = lambda i: (i % per_b, 0)
    return pl.pallas_call(
        _ret_proj_kernel,
        grid=(n // tm,),
        in_specs=[pl.BlockSpec((tm, d), row),
                  pl.BlockSpec((1, d), const),
                  pl.BlockSpec((1, 1, d), bmap),
                  pl.BlockSpec((1, 1, d), bmap),
                  _resident(w.shape, const),
                  pl.BlockSpec((tm, cos.shape[1]), pos),
                  pl.BlockSpec((tm, cos.shape[1]), pos),
                  pl.BlockSpec(q_scale.shape, lambda i: (0, 0, 0)),
                  pl.BlockSpec(k_scale.shape, lambda i: (0, 0, 0))],
        out_specs=[pl.BlockSpec((tm, n_qk), row),
                   pl.BlockSpec((tm, n_qk), row),
                   pl.BlockSpec((tm, n_v), row),
                   pl.BlockSpec((tm, n_v), row)],
        out_shape=[jax.ShapeDtypeStruct((n, n_qk), BF16),
                   jax.ShapeDtypeStruct((n, n_qk), BF16),
                   jax.ShapeDtypeStruct((n, n_v), BF16),
                   jax.ShapeDtypeStruct((n, n_v), F32)],
        compiler_params=_params(1),
        name="retention_in_proj",
    )(x, gain, sc, sh, w, cos, sin, q_scale, k_scale)


def _ret_kernel(q_ref, k_ref, v_ref, gate_ref, decay_ref, o_ref, state_ref):
    t = pl.program_id(1)
    c = RET_CHUNK
    n_heads = state_ref.shape[0]
    dk = q_ref.shape[1] // n_heads
    dv = v_ref.shape[1] // n_heads

    @pl.when(t == 0)
    def _():
        state_ref[...] = jnp.zeros_like(state_ref)

    row = lax.broadcasted_iota(jnp.int32, (c, c), 0)
    col = lax.broadcasted_iota(jnp.int32, (c, c), 1)
    causal = row >= col

    for r0 in range(0, q_ref.shape[0], c):
        rows = slice(r0, r0 + c)
        for h in range(n_heads):
            q = q_ref[rows, h * dk:(h + 1) * dk]
            k = k_ref[rows, h * dk:(h + 1) * dk]
            v = v_ref[rows, h * dv:(h + 1) * dv]
            intra = jnp.where(causal, _dot_nt(q, k), 0.0)
            state = state_ref[h]
            o = _dot(intra.astype(BF16), v) + _dot(q, state.astype(BF16))
            k_t = k.astype(F32).T.astype(BF16)
            state_ref[h] = (state + _dot(k_t, v)) * decay_ref[h]
            ms = jnp.mean(o * o, axis=-1, keepdims=True)
            o = o * lax.rsqrt(ms + EPS)
            gate = gate_ref[rows, h * dv:(h + 1) * dv]
            o_ref[rows, h * dv:(h + 1) * dv] = (o * _silu(gate)).astype(o_ref.dtype)


def _retention(q, k, v, gate, chunk_decay, batch, seq):
    n, w_qk = q.shape
    n_heads = chunk_decay.shape[0]
    w_v = v.shape[1]
    c = RET_CHUNKS_PER_STEP * RET_CHUNK
    nt = seq // c
    rows = lambda b, t: (b * nt + t, 0)
    return pl.pallas_call(
        _ret_kernel,
        grid=(batch, nt),
        in_specs=[pl.BlockSpec((c, w_qk), rows),
                  pl.BlockSpec((c, w_qk), rows),
                  pl.BlockSpec((c, w_v), rows),
                  pl.BlockSpec((c, w_v), rows),
                  pl.BlockSpec(chunk_decay.shape, lambda b, t: (0, 0, 0))],
        out_specs=pl.BlockSpec((c, w_v), rows),
        out_shape=jax.ShapeDtypeStruct((n, w_v), BF16),
        scratch_shapes=[pltpu.VMEM((n_heads, w_qk // n_heads, w_v // n_heads), F32)],
        compiler_params=_params(2),
        name="retention_chunkwise",
    )(q, k, v, gate, chunk_decay)


def _post_kernel(*refs, n_mix):
    x_ref = refs[0]
    o_refs = refs[1:1 + n_mix]
    wm_refs = refs[1 + n_mix:1 + 2 * n_mix]
    gtm_ref, g_ref, sc_ref, sh_ref, gtf_ref, win_ref, wout_ref, out_ref, acc_ref = refs[1 + 2 * n_mix:]
    y = _dot(o_refs[0][...], wm_refs[0][...])
    for o_ref, w_ref in zip(o_refs[1:], wm_refs[1:]):
        y = y + _dot(o_ref[...], w_ref[...])
    x1 = x_ref[...] + gtm_ref[0] * y
    hb = _norm_mod(x1, g_ref[...], sc_ref[0], sh_ref[0]).astype(BF16)
    hidden = wout_ref.shape[0]
    for idx, off in enumerate(range(0, hidden, FFN_CHUNK)):
        g = _dot(hb, win_ref[:, off:off + FFN_CHUNK])
        u = _dot(hb, win_ref[:, hidden + off:hidden + off + FFN_CHUNK])
        part = _dot((_silu(g) * u).astype(BF16), wout_ref[off:off + FFN_CHUNK, :])
        if idx == 0:
            acc_ref[...] = part
        else:
            acc_ref[...] += part
    out_ref[...] = x1 + gtf_ref[0] * acc_ref[...]


def _post(x, mix_outs, mix_ws, gt_m, gain, sc, sh, gt_f, w_in, w_out, seq):
    n, d = x.shape
    tm = ROW_TILE_FFN
    per_b = seq // tm
    bmap = lambda i: (i // per_b, 0, 0)
    row = lambda i: (i, 0)
    const = lambda i: (0, 0)
    vec = pl.BlockSpec((1, 1, d), bmap)
    n_mix = len(mix_outs)
    return pl.pallas_call(
        functools.partial(_post_kernel, n_mix=n_mix),
        grid=(n // tm,),
        in_specs=([pl.BlockSpec((tm, d), row)]
                  + [pl.BlockSpec((tm, o.shape[1]), row) for o in mix_outs]
                  + [_resident(w.shape, const) for w in mix_ws]
                  + [vec, pl.BlockSpec((1, d), const), vec, vec, vec,
                     _resident(w_in.shape, const), _resident(w_out.shape, const)]),
        out_specs=pl.BlockSpec((tm, d), row),
        out_shape=jax.ShapeDtypeStruct((n, d), F32),
        scratch_shapes=[pltpu.VMEM((tm, d), F32)],
        compiler_params=_params(1),
        name="out_proj_swiglu",
    )(x, *mix_outs, *mix_ws, gt_m, gain, sc, sh, gt_f, w_in, w_out)


def kernel(x, c, ada_w, ada_b, norm_mix, norm_ffn, hyb_w_in, hyb_conv, gdn_a_log, gdn_dt_bias, gdn_norm, sb_q_norm, sb_k_norm, hyb_w_out, ret_w_in, ret_w_out, ffn_w_in, ffn_w_out):
    batch, seq, d = x.shape
    depth = ada_w.shape[0]
    gdn_heads = gdn_a_log.shape[1]
    gdn_dv = gdn_norm.shape[1]
    gdn_w = gdn_heads * gdn_dv
    sb_dh = sb_q_norm.shape[1]
    assert sb_dh == SB_DH
    sb_w = hyb_w_out.shape[1] - gdn_w
    sb_heads = sb_w // sb_dh
    ret_heads = 4
    ret_qk = ret_w_in.shape[2] // 6
    ret_v = 2 * ret_qk
    ret_dk = ret_qk // ret_heads

    xf = x.reshape(batch * seq, d)
    mod = _modulation(c, ada_w, ada_b)

    head_mean = jnp.kron(jnp.eye(MXU_DIM // sb_dh, dtype=F32), jnp.full((sb_dh, sb_dh), 1.0 / sb_dh, F32)).astype(BF16)
    idx = jnp.arange(SB_TILE)
    suffix = (idx[:, None] > idx[None, :]).astype(BF16)
    pos = jnp.arange(seq, dtype=F32)
    inv = 1.0 / (ROPE_BASE ** (jnp.arange(0, ret_dk, 2, dtype=F32) / ret_dk))
    ang = pos[:, None] * inv[None, :]
    cos, sin = jnp.cos(ang), jnp.sin(ang)
    log_gamma = jnp.log1p(-jnp.exp2(-5.0 - jnp.arange(ret_heads, dtype=F32)))
    steps = jnp.arange(1, RET_CHUNK + 1, dtype=F32)
    ret_q_scale = jnp.broadcast_to(jnp.exp(log_gamma[:, None] * steps[None, :])[:, :, None],
                                   (ret_heads, RET_CHUNK, ret_dk // 2))
    ret_k_scale = jnp.broadcast_to((jnp.exp(-log_gamma[:, None] * steps[None, :]) * ret_dk ** -0.5)[:, :, None],
                                   (ret_heads, RET_CHUNK, ret_dk // 2))
    ret_chunk_decay = jnp.broadcast_to(jnp.exp(log_gamma * RET_CHUNK)[:, None, None], (ret_heads, 1, ret_v // ret_heads))

    for l in range(depth):
        sh_m, sc_m, gt_m, sh_f, sc_f, gt_f = [m.reshape(batch, 1, d) for m in jnp.split(mod[l], 6, axis=-1)]
        i = l // 2
        gain_m = norm_mix[l].reshape(1, d)
        if l % 2 == 0:
            w_in = hyb_w_in[i]
            n_main = 4 * gdn_w
            n_ab = 2 * gdn_heads
            ab_cols = jnp.zeros((d, LANES), F32).at[:, :n_ab].set(w_in[:, n_main:n_main + n_ab])
            w_re = jnp.concatenate([w_in[:, :n_main], ab_cols, w_in[:, n_main + n_ab:]], axis=1).astype(BF16)
            gdn_proj, sb_qkv = _hyb_proj(
                xf, gain_m, sc_m, sh_m, w_re, hyb_conv[i], head_mean,
                jnp.tile(sb_q_norm[i], sb_heads).reshape(1, sb_w),
                jnp.tile(sb_k_norm[i], sb_heads).reshape(1, sb_w), seq, gdn_dv)
            o_a = _gdn(gdn_proj, gdn_a_log[i], gdn_dt_bias[i], gdn_norm[i], batch, seq)
            o_b = _sb_attention(sb_qkv, suffix, batch, seq)
            w_out = hyb_w_out[i].astype(BF16)
            mix_outs = [o_a, o_b]
            mix_ws = [w_out[:gdn_w], w_out[gdn_w:]]
        else:
            rq, rk, rv, gate = _ret_proj(xf, gain_m, sc_m, sh_m, ret_w_in[i].astype(BF16), cos, sin,
                                         ret_q_scale, ret_k_scale, ret_qk, ret_v, seq)
            o_r = _retention(rq, rk, rv, gate, ret_chunk_decay, batch, seq)
            mix_outs = [o_r]
            mix_ws = [ret_w_out[i].astype(BF16)]
        xf = _post(xf, mix_outs, mix_ws, gt_m, norm_ffn[l].reshape(1, d), sc_f, sh_f, gt_f,
                   ffn_w_in[l].astype(BF16), ffn_w_out[l].astype(BF16), seq)
    return xf.reshape(batch, seq, d)
```

```python
import functools

import jax
import jax.numpy as jnp
from jax import lax
from jax.experimental import pallas as pl
from jax.experimental.pallas import tpu as pltpu

F32 = jnp.float32
BF16 = jnp.bfloat16
EPS = 1e-6
ROPE_BASE = 10000.0

LANES = 128
MXU_DIM = 256
VMEM_LIMIT = 56 * 1024 * 1024

GDN_CONV = 4
GDN_BLOCK = 128
GDN_STEP = 512
SB_TILE = 256
SB_DH = 64
SB_STRIPS = 2
SB_TILES_PER_STEP = 2
LOG2_E = 1.4426950408889634
SB_UNDERFLOW_LOG2 = 180.0
RET_CHUNK = 256
RET_CHUNKS_PER_STEP = 2
ROW_TILE_PROJ = 512
PROJ_GROUP = 256
ROW_TILE_FFN = 512
FFN_CHUNK = 256
CONV_PAD = 8


def _dot(a, b):
    return jnp.dot(a, b, preferred_element_type=F32)


def _dot_nt(a, b):
    return lax.dot_general(a, b, (((1,), (1,)), ((), ())), preferred_element_type=F32)


def _split2(a):
    hi = a.astype(BF16)
    lo = (a - hi.astype(F32)).astype(BF16)
    return hi, lo


def _dot3(a, b):
    ah, al = _split2(a)
    bh, bl = _split2(b)
    n = b.shape[1]
    if 2 * n <= MXU_DIM:
        both = _dot(ah, jnp.concatenate([bh, bl], axis=1))
        return (both[:, :n] + both[:, n:]) + _dot(al, bh)
    return _dot(ah, bh) + (_dot(ah, bl) + _dot(al, bh))


def _dot_exact_rhs(a, m):
    ah, al = _split2(a)
    return _dot(ah, m) + _dot(al, m)


def _dot_exact_lhs(m, a):
    ah, al = _split2(a)
    n = a.shape[1]
    if 2 * n <= MXU_DIM:
        both = _dot(m, jnp.concatenate([ah, al], axis=1))
        return both[:, :n] + both[:, n:]
    return _dot(m, ah) + _dot(m, al)


def _silu(x):
    return x * jax.nn.sigmoid(x)


def _softplus(x):
    return jnp.maximum(x, 0.0) + jnp.log1p(jnp.exp(-jnp.abs(x)))


def _params(n_axes):
    return pltpu.CompilerParams(dimension_semantics=("arbitrary",) * n_axes,
                                vmem_limit_bytes=VMEM_LIMIT)


def _resident(shape, index_map):
    return pl.BlockSpec(shape, index_map, pipeline_mode=pl.Buffered(1))


def _mod_kernel(c_ref, w_ref, b_ref, o_ref):
    c = c_ref[...]
    o_ref[0] = _dot(_silu(c).astype(BF16), w_ref[0].astype(BF16)) + b_ref[0]


def _modulation(c, ada_w, ada_b):
    depth, d, n = ada_w.shape
    b = c.shape[0]
    rows = 8
    c_pad = jnp.zeros((rows, d), F32).at[:b].set(c)
    tn = 1024
    out = pl.pallas_call(
        _mod_kernel,
        grid=(depth, n // tn),
        in_specs=[pl.BlockSpec((rows, d), lambda l, j: (0, 0)),
                  pl.BlockSpec((1, d, tn), lambda l, j: (l, 0, j)),
                  pl.BlockSpec((1, 1, tn), lambda l, j: (l, 0, j))],
        out_specs=pl.BlockSpec((1, rows, tn), lambda l, j: (l, 0, j)),
        out_shape=jax.ShapeDtypeStruct((depth, rows, n), F32),
        compiler_params=_params(2),
        name="adaln_modulation",
    )(c_pad, ada_w, ada_b.reshape(depth, 1, n))
    return out[:, :b]


def _norm_mod(x, gain, sc, sh):
    ms = jnp.mean(x * x, axis=-1, keepdims=True)
    y = x * lax.rsqrt(ms + EPS)
    y = y * gain
    return y * (1.0 + sc) + sh


def _hyb_proj_kernel(x_ref, g_ref, sc_ref, sh_ref, w_ref, cw_ref, gm_ref, qg_ref, kg_ref,
                     gdn_ref, sb_ref, ext_ref, *, per_batch, dk):
    i = pl.program_id(0)
    tm = x_ref.shape[0]
    n_conv = cw_ref.shape[1]
    n_gate = gdn_ref.shape[1] - n_conv - LANES
    w_sb = qg_ref.shape[1]
    groups = [slice(r0, r0 + PROJ_GROUP) for r0 in range(0, tm, PROJ_GROUP)]
    hbs = [_norm_mod(x_ref[rows, :], g_ref[...], sc_ref[0], sh_ref[0]).astype(BF16) for rows in groups]

    @pl.when(i % per_batch == 0)
    def _():
        ext_ref[0:CONV_PAD, :] = jnp.zeros((CONV_PAD, n_conv), F32)

    def l2n(y):
        return y * lax.rsqrt(jnp.sum(y * y, axis=-1, keepdims=True) + EPS)

    for c0 in range(0, n_conv, MXU_DIM):
        for rows, hb in zip(groups, hbs):
            ext_ref[CONV_PAD + rows.start:CONV_PAD + rows.stop, c0:c0 + MXU_DIM] = _dot(hb, w_ref[:, c0:c0 + MXU_DIM])

    def conv_chunk(c0, rows):
        cols = slice(c0, c0 + MXU_DIM)
        w = cw_ref[:, cols]
        y = ext_ref[CONV_PAD + rows.start:CONV_PAD + rows.stop, cols] * w[GDN_CONV - 1:GDN_CONV, :]
        for j in range(GDN_CONV - 1):
            shift = GDN_CONV - 1 - j
            y = y + ext_ref[CONV_PAD + rows.start - shift:CONV_PAD + rows.stop - shift, cols] * w[j:j + 1, :]
        y = _silu(y)
        if c0 < 2 * n_gate:
            y = jnp.concatenate([l2n(y[:, h0:h0 + dk]) for h0 in range(0, MXU_DIM, dk)], axis=1)
            if c0 < n_gate:
                y = y * (dk ** -0.5)
        gdn_ref[rows, cols] = y

    def head_rms(xf, gain):
        sq = xf * xf
        ms = jnp.concatenate([_dot_exact_rhs(sq[:, c0:c0 + MXU_DIM], gm_ref[...])
                              for c0 in range(0, xf.shape[1], MXU_DIM)], axis=1)
        return xf * lax.rsqrt(ms + EPS) * gain

    off_gate = n_conv
    off_ab = off_gate + n_gate
    off_q = off_ab + LANES
    off_k = off_q + w_sb
    off_v = off_k + w_sb
    conv_starts = list(range(0, n_conv, MXU_DIM))
    for rows, hb in zip(groups, hbs):
        gdn_ref[rows, off_gate:off_ab] = _dot(hb, w_ref[:, off_gate:off_gate + n_gate])
    for rows, hb in zip(groups, hbs):
        sq = _dot(hb, w_ref[:, off_q:off_q + w_sb])
        sb_ref[rows, 0:w_sb] = (head_rms(sq, qg_ref[...]) * (SB_DH ** -0.5)).astype(BF16)
    for rows, hb in zip(groups, hbs):
        sk = _dot(hb, w_ref[:, off_k:off_k + w_sb])
        sb_ref[rows, w_sb:2 * w_sb] = head_rms(sk, kg_ref[...]).astype(BF16)
    for rows, hb in zip(groups, hbs):
        sb_ref[rows, 2 * w_sb:3 * w_sb] = _dot(hb, w_ref[:, off_v:off_v + w_sb]).astype(BF16)
        gdn_ref[rows, off_ab:off_q] = _dot(hb, w_ref[:, off_ab:off_ab + LANES])
    for c0 in conv_starts:
        for rows in groups:
            conv_chunk(c0, rows)
    ext_ref[0:CONV_PAD, :] = ext_ref[tm:tm + CONV_PAD, :]


def _hyb_proj(x, gain, sc, sh, w, conv_w, gmat, qg, kg, seq, gdn_dk):
    n, d = x.shape
    tm = ROW_TILE_PROJ
    per_b = seq // tm
    w_sb = qg.shape[1]
    n_conv = conv_w.shape[1]
    n_gate = w.shape[1] - n_conv - LANES - 3 * w_sb
    bmap = lambda i: (i // per_b, 0, 0)
    row = lambda i: (i, 0)
    const = lambda i: (0, 0)
    return pl.pallas_call(
        functools.partial(_hyb_proj_kernel, per_batch=per_b, dk=gdn_dk),
        grid=(n // tm,),
        in_specs=[pl.BlockSpec((tm, d), row),
                  pl.BlockSpec((1, d), const),
                  pl.BlockSpec((1, 1, d), bmap),
                  pl.BlockSpec((1, 1, d), bmap),
                  _resident(w.shape, const),
                  pl.BlockSpec(conv_w.shape, const),
                  _resident(gmat.shape, const),
                  pl.BlockSpec((1, w_sb), const),
                  pl.BlockSpec((1, w_sb), const)],
        out_specs=[pl.BlockSpec((tm, n_conv + n_gate + LANES), row),
                   pl.BlockSpec((tm, 3 * w_sb), row)],
        out_shape=[jax.ShapeDtypeStruct((n, n_conv + n_gate + LANES), F32),
                   jax.ShapeDtypeStruct((n, 3 * w_sb), BF16)],
        scratch_shapes=[pltpu.VMEM((CONV_PAD + tm, n_conv), F32)],
        compiler_params=_params(1),
        name="hybrid_in_proj",
    )(x, gain, sc, sh, w, conv_w, gmat, qg, kg)


def _dot3s(a, b):
    ah, al = a
    bh, bl = b
    n = bh.shape[1]
    both = _dot(ah, jnp.concatenate([bh, bl], axis=1))
    return (both[:, :n] + both[:, n:]) + _dot(al, bh)


def _lower_child_rows(x, size):
    return jnp.concatenate([x[r + size:r + 2 * size] for r in range(0, x.shape[0], 2 * size)], axis=0)


def _unit_lower_inverse_all(lowers, eye, row, col):
    bits = 3
    same = (row >> bits) == (col >> bits)
    diags = [jnp.where(same, l, 0.0) for l in lowers]
    d_s = [_split2(d) for d in diags]
    x0_s = [_split2(eye - d) for d in diags]
    p2_s = [_split2(_dot3s(d, d)) for d in d_s]
    xs = [(eye - d) + _dot3s(x, p) for d, x, p in zip(diags, x0_s, p2_s)]
    p4_s = [_split2(_dot3s(p, p)) for p in p2_s]
    xs = [x + _dot3s(_split2(x), p) for x, p in zip(xs, p4_s)]
    n = lowers[0].shape[0]
    size = 1 << bits
    while size < n:
        parent = (row >> (bits + 1)) == (col >> (bits + 1))
        child = (row >> bits) == (col >> bits)
        off_s = [_split2(jnp.where(parent, jnp.where(child, 0.0, l), 0.0)) for l in lowers]
        x_s = [_split2(x) for x in xs]
        ys = [_dot3s(_split2(_lower_child_rows(x, size)), o) for x, o in zip(xs, off_s)]
        zs = [_dot3s(_split2(y), x) for y, x in zip(ys, x_s)]
        new_xs = []
        for x, z in zip(xs, zs):
            pieces = []
            for k, r in enumerate(range(0, n, 2 * size)):
                pieces.append(x[r:r + size])
                pieces.append(x[r + size:r + 2 * size] - z[k * size:(k + 1) * size])
            new_xs.append(jnp.concatenate(pieces, axis=0))
        xs = new_xs
        bits += 1
        size *= 2
    return xs


def _gdn_kernel(q_ref, k_ref, v_ref, gate_ref, ab_ref, alog_ref, dtb_ref, gn_ref, o_ref, state_ref):
    t = pl.program_id(1)
    tb, width = q_ref.shape
    c = GDN_BLOCK
    n_heads = state_ref.shape[0]
    dk = width // n_heads
    n_chunks = tb // c

    @pl.when(t == 0)
    def _():
        state_ref[...] = jnp.zeros_like(state_ref)

    lane = lax.broadcasted_iota(jnp.int32, (tb, LANES), 1)
    ab = ab_ref[...]
    g_lanes = -jnp.exp(alog_ref[...]) * _softplus(ab + dtb_ref[...])
    beta_lanes = jax.nn.sigmoid(ab)

    def lane_column(x, idx):
        return jnp.broadcast_to(jnp.sum(jnp.where(lane == idx, x, 0.0), axis=1, keepdims=True), (tb, LANES))

    row = lax.broadcasted_iota(jnp.int32, (c, c), 0)
    col = lax.broadcasted_iota(jnp.int32, (c, c), 1)
    incl = row >= col
    strict = row > col
    eye = jnp.where(row == col, 1.0, 0.0).astype(F32)
    tri = jnp.where(incl, 1.0, 0.0).astype(BF16)

    items = []
    for h in range(n_heads):
        g_h = lane_column(g_lanes, h)
        beta_h = lane_column(beta_lanes, h + n_heads)
        cols = slice(h * dk, (h + 1) * dk)
        for n in range(n_chunks):
            rows = slice(n * c, (n + 1) * c)
            items.append(dict(q=q_ref[rows, cols], k=k_ref[rows, cols], v=v_ref[rows, cols],
                              beta=beta_h[rows], g=g_h[rows]))

    def prepare(group):
        for it in group:
            it["gcol"] = _dot_exact_lhs(tri, it["g"])
        for it in group:
            gcol = it["gcol"]
            it["dec"] = jnp.exp(jnp.where(incl, gcol - gcol.T, 0.0))
            it["eg"] = jnp.exp(gcol)
            it["g_last"] = gcol[c - 1:c, :]
            it["kb"] = it["k"] * it["beta"]
            it["kk"] = it["k"].astype(BF16)
        for it in group:
            it["lower"] = jnp.where(strict, _dot_nt(it["kb"].astype(BF16), it["kk"]) * it["dec"], 0.0)
        tinvs = _unit_lower_inverse_all([it["lower"] for it in group], eye, row, col)
        for it, tinv in zip(group, tinvs):
            rhs = jnp.concatenate([it["v"] * it["beta"], it["kb"] * it["eg"]], axis=1)
            uw = _dot3(tinv, rhs)
            it["u"] = uw[:, :dk]
            a = jnp.where(incl, _dot_nt(it["q"].astype(BF16), it["kk"]) * it["dec"], 0.0)
            q_dec = it["q"] * it["eg"]
            k_dec = it["k"] * jnp.exp(it["g_last"] - it["gcol"])
            it["wq"] = jnp.concatenate([uw[:, dk:], q_dec], axis=0).astype(BF16)
            it["ak"] = jnp.concatenate([a, k_dec.T], axis=0).astype(BF16)
            it["decay"] = jnp.exp(it["g_last"])

    early = n_chunks // 2
    prepare([it for i, it in enumerate(items) if i % n_chunks < early])
    prepare([it for i, it in enumerate(items) if i % n_chunks >= early])

    gn = gn_ref[...]
    states = [state_ref[h] for h in range(n_heads)]
    for n in range(n_chunks):
        rows = slice(n * c, (n + 1) * c)
        for h in range(n_heads):
            it = items[h * n_chunks + n]
            cols = slice(h * dk, (h + 1) * dk)
            ws_qs = _dot(it["wq"], states[h].astype(BF16))
            v_new = it["u"] - ws_qs[:c]
            av_kv = _dot(it["ak"], v_new.astype(BF16))
            o = ws_qs[c:] + av_kv[:c]
            states[h] = states[h] * it["decay"] + av_kv[c:]
            ms = jnp.mean(o * o, axis=-1, keepdims=True)
            o = o * lax.rsqrt(ms + EPS) * gn
            o_ref[rows, cols] = (o * _silu(gate_ref[rows, cols])).astype(o_ref.dtype)
    for h in range(n_heads):
        state_ref[h] = states[h]


def _gdn(proj, a_log, dt_bias, gdn_norm, batch, seq):
    n = proj.shape[0]
    width = (proj.shape[1] - LANES) // 4
    n_heads = a_log.shape[0]
    dk = width // n_heads
    tb = GDN_STEP
    nt = seq // tb
    rows = lambda b, t: b * nt + t
    pad = lambda x: jnp.zeros((1, LANES), F32).at[0, :n_heads].set(x)
    col_spec = lambda k: pl.BlockSpec((tb, width), lambda b, t: (rows(b, t), k))
    const = lambda b, t: (0, 0)
    return pl.pallas_call(
        _gdn_kernel,
        grid=(batch, nt),
        in_specs=[col_spec(0), col_spec(1), col_spec(2), col_spec(3),
                  pl.BlockSpec((tb, LANES), lambda b, t: (rows(b, t), 4 * width // LANES)),
                  pl.BlockSpec((1, LANES), const), pl.BlockSpec((1, LANES), const),
                  pl.BlockSpec((1, dk), const)],
        out_specs=pl.BlockSpec((tb, width), lambda b, t: (rows(b, t), 0)),
        out_shape=jax.ShapeDtypeStruct((n, width), BF16),
        scratch_shapes=[pltpu.VMEM((n_heads, dk, dk), F32)],
        compiler_params=_params(2),
        name="gated_delta_rule",
    )(proj, proj, proj, proj, proj, pad(a_log), pad(dt_bias), gdn_norm.reshape(1, dk))


def _sb_kernel(q_ref, k_ref, v_ref, m_ref, o_ref, q2_ref, acc_ref, carry_ref):
    i = pl.program_id(2)
    tq = SB_TILE
    n_tiles = q_ref.shape[0] // tq
    lane = lax.broadcasted_iota(jnp.int32, (tq, LANES), 1)
    suffix = m_ref[...]
    strip = tq // SB_STRIPS

    for t in range(n_tiles):
        q = q_ref[t * tq:(t + 1) * tq, :]
        zero = jnp.zeros_like(q)
        q2_ref[2 * t * tq:(2 * t + 1) * tq, :] = jnp.where(lane < SB_DH, q, zero)
        q2_ref[(2 * t + 1) * tq:(2 * t + 2) * tq, :] = jnp.where(lane >= SB_DH, q, zero)

    def strips_of(t):
        return [slice(2 * t * tq + n * strip, 2 * t * tq + (n + 1) * strip) for n in range(2 * SB_STRIPS)]

    def weigh(pairs):
        work = []
        for t, j, diagonal in pairs:
            start = pl.multiple_of(j * tq, tq)
            k = k_ref[pl.ds(start, tq), :]
            v = v_ref[pl.ds(start, tq), :]
            for n, rows in enumerate(strips_of(t)):
                work.append(dict(n=n, rows=rows, k=k, v=v, diagonal=diagonal))
        for w in work:
            w["y"] = _dot_nt(q2_ref[w["rows"], :], w["k"]) * LOG2_E
        for w in work:
            y = w["y"]
            w["soft"] = jnp.maximum(y, 0.0) + jnp.log2(1.0 + jnp.exp2(-jnp.abs(y)))
            if w["diagonal"]:
                s = lax.broadcasted_iota(jnp.int32, (strip, tq), 1)
                r = lax.broadcasted_iota(jnp.int32, (strip, tq), 0) + (w["n"] % SB_STRIPS) * strip
                w["mask"] = s < r
                w["cost"] = jnp.where(w["mask"], w["soft"], 0.0)
            else:
                w["cost"] = w["soft"]
        for w in work:
            w["later"] = _dot_exact_rhs(w["cost"], suffix)
        for w in work:
            a = jnp.exp2(w["y"] - w["soft"] - w["later"])
            if w["diagonal"]:
                a = jnp.where(w["mask"], a, 0.0)
            w["part"] = _dot(a.astype(BF16), w["v"])
            w["total"] = jnp.sum(w["cost"], axis=1, keepdims=True)
        per_pair = 2 * SB_STRIPS
        return [work[p * per_pair:(p + 1) * per_pair] for p in range(len(pairs))]

    first = i * n_tiles
    pairs = [(t, first + t, True) for t in range(n_tiles)]
    pairs += [(t, jnp.maximum(first + t - 1, 0), False) for t in range(n_tiles)]
    results = weigh(pairs)
    for t in range(n_tiles):
        has_prev = (first + t > 0).astype(F32)
        for d, p in zip(results[t], results[n_tiles + t]):
            acc_ref[d["rows"], :] = d["part"] + p["part"] * (jnp.exp2(-d["total"]) * has_prev)
            carry_ref[d["rows"], :] = d["total"] + p["total"] * has_prev

    @pl.when(jnp.min(carry_ref[...]) < SB_UNDERFLOW_LOG2)
    def _():
        for t in range(n_tiles):
            rows_t = slice(2 * t * tq, 2 * (t + 1) * tq)
            remaining = first + t - 1

            def cond(state):
                step, least = state
                return jnp.logical_and(step < remaining, least < SB_UNDERFLOW_LOG2)

            def body(state, t=t, rows_t=rows_t, remaining=remaining):
                step, _ = state
                (tile,) = weigh([(t, remaining - 1 - step, False)])
                for w in tile:
                    carry = carry_ref[w["rows"], :]
                    acc_ref[w["rows"], :] += w["part"] * jnp.exp2(-carry)
                    carry_ref[w["rows"], :] = carry + w["total"]
                return step + 1, jnp.min(carry_ref[rows_t, :])

            lax.while_loop(cond, body, (jnp.int32(0), jnp.min(carry_ref[rows_t, :])))

    for t in range(n_tiles):
        acc0 = acc_ref[2 * t * tq:(2 * t + 1) * tq, :]
        acc1 = acc_ref[(2 * t + 1) * tq:(2 * t + 2) * tq, :]
        o_ref[t * tq:(t + 1) * tq, :] = jnp.where(lane < SB_DH, acc0, acc1).astype(o_ref.dtype)


def _sb_attention(qkv, suffix, batch, seq):
    n = qkv.shape[0]
    width = qkv.shape[1] // 3
    pairs = width // LANES
    tq = SB_TILE
    rows = SB_TILES_PER_STEP * tq
    nq = seq // rows
    return pl.pallas_call(
        _sb_kernel,
        grid=(batch, pairs, nq),
        in_specs=[pl.BlockSpec((rows, LANES), lambda b, p, i: (b * nq + i, p)),
                  pl.BlockSpec((seq, LANES), lambda b, p, i: (b, pairs + p)),
                  pl.BlockSpec((seq, LANES), lambda b, p, i: (b, 2 * pairs + p)),
                  pl.BlockSpec((tq, tq), lambda b, p, i: (0, 0))],
        out_specs=pl.BlockSpec((rows, LANES), lambda b, p, i: (b * nq + i, p)),
        out_shape=jax.ShapeDtypeStruct((n, width), BF16),
        scratch_shapes=[pltpu.VMEM((2 * rows, LANES), BF16),
                        pltpu.VMEM((2 * rows, LANES), F32),
                        pltpu.VMEM((2 * rows, 1), F32)],
        compiler_params=_params(3),
        name="stick_breaking_attention",
    )(qkv, qkv, qkv, suffix)


def _ret_proj_kernel(x_ref, g_ref, sc_ref, sh_ref, w_ref, cos_ref, sin_ref, qs_ref, ks_ref,
                     q_ref, k_ref, v_ref, gate_ref):
    tm = x_ref.shape[0]
    n_heads, chunk, _ = qs_ref.shape
    w_qk = q_ref.shape[1]
    dk = w_qk // n_heads
    half = dk // 2
    groups = [slice(r0, r0 + chunk) for r0 in range(0, tm, chunk)]
    hbs = [_norm_mod(x_ref[rows, :], g_ref[...], sc_ref[0], sh_ref[0]).astype(BF16) for rows in groups]

    def rot(x, rows, scale):
        cos = cos_ref[rows, :]
        sin = sin_ref[rows, :]
        x1, x2 = x[:, :half], x[:, half:]
        return jnp.concatenate([(x1 * cos - x2 * sin) * scale, (x1 * sin + x2 * cos) * scale], axis=-1)

    for h in range(n_heads):
        cols = slice(h * dk, (h + 1) * dk)
        for rows, hb in zip(groups, hbs):
            q_ref[rows, cols] = rot(_dot(hb, w_ref[:, cols]), rows, qs_ref[h]).astype(BF16)
    for h in range(n_heads):
        cols = slice(h * dk, (h + 1) * dk)
        for rows, hb in zip(groups, hbs):
            pre = _dot(hb, w_ref[:, w_qk + h * dk:w_qk + (h + 1) * dk])
            k_ref[rows, cols] = rot(pre, rows, ks_ref[h]).astype(BF16)
    step = 512
    off = 2 * w_qk
    for ref in (v_ref, gate_ref):
        for o in range(0, ref.shape[1], step):
            for rows, hb in zip(groups, hbs):
                ref[rows, o:o + step] = _dot(hb, w_ref[:, off + o:off + o + step]).astype(ref.dtype)
        off += ref.shape[1]


def _ret_proj(x, gain, sc, sh, w, cos, sin, q_scale, k_scale, n_qk, n_v, seq):
    n, d = x.shape
    tm = ROW_TILE_PROJ
    per_b = seq // tm
    bmap = lambda i: (i // per_b, 0, 0)
    row = lambda i: (i, 0)
    const = lambda i: (0, 0)
    pos = lambda i: (i % per_b, 0)
    return pl.pallas_call(
        _ret_proj_kernel,
        grid=(n // tm,),
        in_specs=[pl.BlockSpec((tm, d), row),
                  pl.BlockSpec((1, d), const),
                  pl.BlockSpec((1, 1, d), bmap),
                  pl.BlockSpec((1, 1, d), bmap),
                  _resident(w.shape, const),
                  pl.BlockSpec((tm, cos.shape[1]), pos),
                  pl.BlockSpec((tm, cos.shape[1]), pos),
                  pl.BlockSpec(q_scale.shape, lambda i: (0, 0, 0)),
                  pl.BlockSpec(k_scale.shape, lambda i: (0, 0, 0))],
        out_specs=[pl.BlockSpec((tm, n_qk), row),
                   pl.BlockSpec((tm, n_qk), row),
                   pl.BlockSpec((tm, n_v), row),
                   pl.BlockSpec((tm, n_v), row)],
        out_shape=[jax.ShapeDtypeStruct((n, n_qk), BF16),
                   jax.ShapeDtypeStruct((n, n_qk), BF16),
                   jax.ShapeDtypeStruct((n, n_v), BF16),
                   jax.ShapeDtypeStruct((n, n_v), F32)],
        compiler_params=_params(1),
        name="retention_in_proj",
    )(x, gain, sc, sh, w, cos, sin, q_scale, k_scale)


def _ret_kernel(q_ref, k_ref, v_ref, gate_ref, decay_ref, o_ref, state_ref):
    t = pl.program_id(1)
    c = RET_CHUNK
    n_heads = state_ref.shape[0]
    dk = q_ref.shape[1] // n_heads
    dv = v_ref.shape[1] // n_heads

    @pl.when(t == 0)
    def _():
        state_ref[...] = jnp.zeros_like(state_ref)

    row = lax.broadcasted_iota(jnp.int32, (c, c), 0)
    col = lax.broadcasted_iota(jnp.int32, (c, c), 1)
    causal = row >= col

    for r0 in range(0, q_ref.shape[0], c):
        rows = slice(r0, r0 + c)
        for h in range(n_heads):
            q = q_ref[rows, h * dk:(h + 1) * dk]
            k = k_ref[rows, h * dk:(h + 1) * dk]
            v = v_ref[rows, h * dv:(h + 1) * dv]
            intra = jnp.where(causal, _dot_nt(q, k), 0.0)
            state = state_ref[h]
            o = _dot(intra.astype(BF16), v) + _dot(q, state.astype(BF16))
            k_t = k.astype(F32).T.astype(BF16)
            state_ref[h] = (state + _dot(k_t, v)) * decay_ref[h]
            ms = jnp.mean(o * o, axis=-1, keepdims=True)
            o = o * lax.rsqrt(ms + EPS)
            gate = gate_ref[rows, h * dv:(h + 1) * dv]
            o_ref[rows, h * dv:(h + 1) * dv] = (o * _silu(gate)).astype(o_ref.dtype)


def _retention(q, k, v, gate, chunk_decay, batch, seq):
    n, w_qk = q.shape
    n_heads = chunk_decay.shape[0]
    w_v = v.shape[1]
    c = RET_CHUNKS_PER_STEP * RET_CHUNK
    nt = seq // c
    rows = lambda b, t: (b * nt + t, 0)
    return pl.pallas_call(
        _ret_kernel,
        grid=(batch, nt),
        in_specs=[pl.BlockSpec((c, w_qk), rows),
                  pl.BlockSpec((c, w_qk), rows),
                  pl.BlockSpec((c, w_v), rows),
                  pl.BlockSpec((c, w_v), rows),
                  pl.BlockSpec(chunk_decay.shape, lambda b, t: (0, 0, 0))],
        out_specs=pl.BlockSpec((c, w_v), rows),
        out_shape=jax.ShapeDtypeStruct((n, w_v), BF16),
        scratch_shapes=[pltpu.VMEM((n_heads, w_qk // n_heads, w_v // n_heads), F32)],
        compiler_params=_params(2),
        name="retention_chunkwise",
    )(q, k, v, gate, chunk_decay)


def _post_kernel(*refs, n_mix):
    x_ref = refs[0]
    o_refs = refs[1:1 + n_mix]
    wm_refs = refs[1 + n_mix:1 + 2 * n_mix]
    gtm_ref, g_ref, sc_ref, sh_ref, gtf_ref, win_ref, wout_ref, out_ref, acc_ref = refs[1 + 2 * n_mix:]
    y = _dot(o_refs[0][...], wm_refs[0][...])
    for o_ref, w_ref in zip(o_refs[1:], wm_refs[1:]):
        y = y + _dot(o_ref[...], w_ref[...])
    x1 = x_ref[...] + gtm_ref[0] * y
    hb = _norm_mod(x1, g_ref[...], sc_ref[0], sh_ref[0]).astype(BF16)
    hidden = wout_ref.shape[0]
    for idx, off in enumerate(range(0, hidden, FFN_CHUNK)):
        g = _dot(hb, win_ref[:, off:off + FFN_CHUNK])
        u = _dot(hb, win_ref[:, hidden + off:hidden + off + FFN_CHUNK])
        part = _dot((_silu(g) * u).astype(BF16), wout_ref[off:off + FFN_CHUNK, :])
        if idx == 0:
            acc_ref[...] = part
        else:
            acc_ref[...] += part
    out_ref[...] = x1 + gtf_ref[0] * acc_ref[...]


def _post(x, mix_outs, mix_ws, gt_m, gain, sc, sh, gt_f, w_in, w_out, seq):
    n, d = x.shape
    tm = ROW_TILE_FFN
    per_b = seq // tm
    bmap = lambda i: (i // per_b, 0, 0)
    row = lambda i: (i, 0)
    const = lambda i: (0, 0)
    vec = pl.BlockSpec((1, 1, d), bmap)
    n_mix = len(mix_outs)
    return pl.pallas_call(
        functools.partial(_post_kernel, n_mix=n_mix),
        grid=(n // tm,),
        in_specs=([pl.BlockSpec((tm, d), row)]
                  + [pl.BlockSpec((tm, o.shape[1]), row) for o in mix_outs]
                  + [_resident(w.shape, const) for w in mix_ws]
                  + [vec, pl.BlockSpec((1, d), const), vec, vec, vec,
                     _resident(w_in.shape, const), _resident(w_out.shape, const)]),
        out_specs=pl.BlockSpec((tm, d), row),
        out_shape=jax.ShapeDtypeStruct((n, d), F32),
        scratch_shapes=[pltpu.VMEM((tm, d), F32)],
        compiler_params=_params(1),
        name="out_proj_swiglu",
    )(x, *mix_outs, *mix_ws, gt_m, gain, sc, sh, gt_f, w_in, w_out)


def kernel(x, c, ada_w, ada_b, norm_mix, norm_ffn, hyb_w_in, hyb_conv, gdn_a_log, gdn_dt_bias, gdn_norm, sb_q_norm, sb_k_norm, hyb_w_out, ret_w_in, ret_w_out, ffn_w_in, ffn_w_out):
    batch, seq, d = x.shape
    depth = ada_w.shape[0]
    gdn_heads = gdn_a_log.shape[1]
    gdn_dv = gdn_norm.shape[1]
    gdn_w = gdn_heads * gdn_dv
    sb_dh = sb_q_norm.shape[1]
    assert sb_dh == SB_DH
    sb_w = hyb_w_out.shape[1] - gdn_w
    sb_heads = sb_w // sb_dh
    ret_heads = 4
    ret_qk = ret_w_in.shape[2] // 6
    ret_v = 2 * ret_qk
    ret_dk = ret_qk // ret_heads

    xf = x.reshape(batch * seq, d)
    mod = _modulation(c, ada_w, ada_b)

    head_mean = jnp.kron(jnp.eye(MXU_DIM // sb_dh, dtype=F32), jnp.full((sb_dh, sb_dh), 1.0 / sb_dh, F32)).astype(BF16)
    idx = jnp.arange(SB_TILE)
    suffix = (idx[:, None] > idx[None, :]).astype(BF16)
    pos = jnp.arange(seq, dtype=F32)
    inv = 1.0 / (ROPE_BASE ** (jnp.arange(0, ret_dk, 2, dtype=F32) / ret_dk))
    ang = pos[:, None] * inv[None, :]
    cos, sin = jnp.cos(ang), jnp.sin(ang)
    log_gamma = jnp.log1p(-jnp.exp2(-5.0 - jnp.arange(ret_heads, dtype=F32)))
    steps = jnp.arange(1, RET_CHUNK + 1, dtype=F32)
    ret_q_scale = jnp.broadcast_to(jnp.exp(log_gamma[:, None] * steps[None, :])[:, :, None],
                                   (ret_heads, RET_CHUNK, ret_dk // 2))
    ret_k_scale = jnp.broadcast_to((jnp.exp(-log_gamma[:, None] * steps[None, :]) * ret_dk ** -0.5)[:, :, None],
                                   (ret_heads, RET_CHUNK, ret_dk // 2))
    ret_chunk_decay = jnp.broadcast_to(jnp.exp(log_gamma * RET_CHUNK)[:, None, None], (ret_heads, 1, ret_v // ret_heads))

    for l in range(depth):
        sh_m, sc_m, gt_m, sh_f, sc_f, gt_f = [m.reshape(batch, 1, d) for m in jnp.split(mod[l], 6, axis=-1)]
        i = l // 2
        gain_m = norm_mix[l].reshape(1, d)
        if l % 2 == 0:
            w_in = hyb_w_in[i]
            n_main = 4 * gdn_w
            n_ab = 2 * gdn_heads
            ab_cols = jnp.zeros((d, LANES), F32).at[:, :n_ab].set(w_in[:, n_main:n_main + n_ab])
            w_re = jnp.concatenate([w_in[:, :n_main], ab_cols, w_in[:, n_main + n_ab:]], axis=1).astype(BF16)
            gdn_proj, sb_qkv = _hyb_proj(
                xf, gain_m, sc_m, sh_m, w_re, hyb_conv[i], head_mean,
                jnp.tile(sb_q_norm[i], sb_heads).reshape(1, sb_w),
                jnp.tile(sb_k_norm[i], sb_heads).reshape(1, sb_w), seq, gdn_dv)
            o_a = _gdn(gdn_proj, gdn_a_log[i], gdn_dt_bias[i], gdn_norm[i], batch, seq)
            o_b = _sb_attention(sb_qkv, suffix, batch, seq)
            w_out = hyb_w_out[i].astype(BF16)
            mix_outs = [o_a, o_b]
            mix_ws = [w_out[:gdn_w], w_out[gdn_w:]]
        else:
            rq, rk, rv, gate = _ret_proj(xf, gain_m, sc_m, sh_m, ret_w_in[i].astype(BF16), cos, sin,
                                         ret_q_scale, ret_k_scale, ret_qk, ret_v, seq)
            o_r = _retention(rq, rk, rv, gate, ret_chunk_decay, batch, seq)
            mix_outs = [o_r]
            mix_ws = [ret_w_out[i].astype(BF16)]
        xf = _post(xf, mix_outs, mix_ws, gt_m, norm_ffn[l].reshape(1, d), sc_f, sh_f, gt_f,
                   ffn_w_in[l].astype(BF16), ffn_w_out[l].astype(BF16), seq)
    return xf.reshape(batch, seq, d)
```

```python
import functools

import jax
import jax.numpy as jnp
from jax import lax
from jax.experimental import pallas as pl
from jax.experimental.pallas import tpu as pltpu

F32 = jnp.float32
BF16 = jnp.bfloat16
EPS = 1e-6
ROPE_BASE = 10000.0

LANES = 128
MXU_DIM = 256
VMEM_LIMIT = 56 * 1024 * 1024

GDN_CONV = 4
GDN_BLOCK = 128
GDN_STEP = 512
SB_TILE = 256
SB_DH = 64
SB_STRIPS = 2
SB_TILES_PER_STEP = 8
LOG2_E = 1.4426950408889634
SB_UNDERFLOW_LOG2 = 180.0
RET_CHUNK = 256
RET_CHUNKS_PER_STEP = 4
ROW_TILE_PROJ = 512
PROJ_GROUP = 256
ROW_TILE_FFN = 512
FFN_CHUNK = 256
CONV_PAD = 8


def _dot(a, b):
    return jnp.dot(a, b, preferred_element_type=F32)


def _dot_nt(a, b):
    return lax.dot_general(a, b, (((1,), (1,)), ((), ())), preferred_element_type=F32)


def _split2(a):
    hi = a.astype(BF16)
    lo = (a - hi.astype(F32)).astype(BF16)
    return hi, lo


def _dot3(a, b):
    ah, al = _split2(a)
    bh, bl = _split2(b)
    n = b.shape[1]
    if 2 * n <= MXU_DIM:
        both = _dot(ah, jnp.concatenate([bh, bl], axis=1))
        return (both[:, :n] + both[:, n:]) + _dot(al, bh)
    return _dot(ah, bh) + (_dot(ah, bl) + _dot(al, bh))


def _dot_exact_rhs(a, m):
    ah, al = _split2(a)
    return _dot(ah, m) + _dot(al, m)


def _dot_exact_lhs(m, a):
    ah, al = _split2(a)
    n = a.shape[1]
    if 2 * n <= MXU_DIM:
        both = _dot(m, jnp.concatenate([ah, al], axis=1))
        return both[:, :n] + both[:, n:]
    return _dot(m, ah) + _dot(m, al)


def _silu(x):
    return x * jax.nn.sigmoid(x)


def _softplus(x):
    return jnp.maximum(x, 0.0) + jnp.log1p(jnp.exp(-jnp.abs(x)))


def _params(n_axes):
    return pltpu.CompilerParams(dimension_semantics=("arbitrary",) * n_axes,
                                vmem_limit_bytes=VMEM_LIMIT)


def _resident(shape, index_map):
    return pl.BlockSpec(shape, index_map, pipeline_mode=pl.Buffered(1))


def _mod_kernel(c_ref, w_ref, b_ref, o_ref):
    c = c_ref[...]
    o_ref[0] = _dot(_silu(c).astype(BF16), w_ref[0].astype(BF16)) + b_ref[0]


def _modulation(c, ada_w, ada_b):
    depth, d, n = ada_w.shape
    b = c.shape[0]
    rows = 8
    c_pad = jnp.zeros((rows, d), F32).at[:b].set(c)
    tn = 1024
    out = pl.pallas_call(
        _mod_kernel,
        grid=(depth, n // tn),
        in_specs=[pl.BlockSpec((rows, d), lambda l, j: (0, 0)),
                  pl.BlockSpec((1, d, tn), lambda l, j: (l, 0, j)),
                  pl.BlockSpec((1, 1, tn), lambda l, j: (l, 0, j))],
        out_specs=pl.BlockSpec((1, rows, tn), lambda l, j: (l, 0, j)),
        out_shape=jax.ShapeDtypeStruct((depth, rows, n), F32),
        compiler_params=_params(2),
        name="adaln_modulation",
    )(c_pad, ada_w, ada_b.reshape(depth, 1, n))
    return out[:, :b]


def _norm_mod(x, gain, sc, sh):
    ms = jnp.mean(x * x, axis=-1, keepdims=True)
    y = x * lax.rsqrt(ms + EPS)
    y = y * gain
    return y * (1.0 + sc) + sh


def _hyb_proj_kernel(x_ref, g_ref, sc_ref, sh_ref, w_ref, cw_ref, gm_ref, qg_ref, kg_ref,
                     gdn_ref, sb_ref, ext_ref, *, per_batch, dk):
    i = pl.program_id(0)
    tm = x_ref.shape[0]
    n_conv = cw_ref.shape[1]
    n_gate = gdn_ref.shape[1] - n_conv - LANES
    w_sb = qg_ref.shape[1]
    groups = [slice(r0, r0 + PROJ_GROUP) for r0 in range(0, tm, PROJ_GROUP)]
    hbs = [_norm_mod(x_ref[rows, :], g_ref[...], sc_ref[0], sh_ref[0]).astype(BF16) for rows in groups]

    @pl.when(i % per_batch == 0)
    def _():
        ext_ref[0:CONV_PAD, :] = jnp.zeros((CONV_PAD, n_conv), F32)

    def l2n(y):
        return y * lax.rsqrt(jnp.sum(y * y, axis=-1, keepdims=True) + EPS)

    for c0 in range(0, n_conv, MXU_DIM):
        for rows, hb in zip(groups, hbs):
            ext_ref[CONV_PAD + rows.start:CONV_PAD + rows.stop, c0:c0 + MXU_DIM] = _dot(hb, w_ref[:, c0:c0 + MXU_DIM])

    def conv_chunk(c0, rows):
        cols = slice(c0, c0 + MXU_DIM)
        w = cw_ref[:, cols]
        y = ext_ref[CONV_PAD + rows.start:CONV_PAD + rows.stop, cols] * w[GDN_CONV - 1:GDN_CONV, :]
        for j in range(GDN_CONV - 1):
            shift = GDN_CONV - 1 - j
            y = y + ext_ref[CONV_PAD + rows.start - shift:CONV_PAD + rows.stop - shift, cols] * w[j:j + 1, :]
        y = _silu(y)
        if c0 < 2 * n_gate:
            y = jnp.concatenate([l2n(y[:, h0:h0 + dk]) for h0 in range(0, MXU_DIM, dk)], axis=1)
            if c0 < n_gate:
                y = y * (dk ** -0.5)
        gdn_ref[rows, cols] = y

    def head_rms(xf, gain):
        sq = xf * xf
        ms = jnp.concatenate([_dot_exact_rhs(sq[:, c0:c0 + MXU_DIM], gm_ref[...])
                              for c0 in range(0, xf.shape[1], MXU_DIM)], axis=1)
        return xf * lax.rsqrt(ms + EPS) * gain

    off_gate = n_conv
    off_ab = off_gate + n_gate
    off_q = off_ab + LANES
    off_k = off_q + w_sb
    off_v = off_k + w_sb
    conv_starts = list(range(0, n_conv, MXU_DIM))
    for rows, hb in zip(groups, hbs):
        gdn_ref[rows, off_gate:off_ab] = _dot(hb, w_ref[:, off_gate:off_gate + n_gate])
    for rows, hb in zip(groups, hbs):
        sq = _dot(hb, w_ref[:, off_q:off_q + w_sb])
        sb_ref[rows, 0:w_sb] = (head_rms(sq, qg_ref[...]) * (SB_DH ** -0.5)).astype(BF16)
    for rows, hb in zip(groups, hbs):
        sk = _dot(hb, w_ref[:, off_k:off_k + w_sb])
        sb_ref[rows, w_sb:2 * w_sb] = head_rms(sk, kg_ref[...]).astype(BF16)
    for rows, hb in zip(groups, hbs):
        sb_ref[rows, 2 * w_sb:3 * w_sb] = _dot(hb, w_ref[:, off_v:off_v + w_sb]).astype(BF16)
        gdn_ref[rows, off_ab:off_q] = _dot(hb, w_ref[:, off_ab:off_ab + LANES])
    for c0 in conv_starts:
        for rows in groups:
            conv_chunk(c0, rows)
    ext_ref[0:CONV_PAD, :] = ext_ref[tm:tm + CONV_PAD, :]


def _hyb_proj(x, gain, sc, sh, w, conv_w, gmat, qg, kg, seq, gdn_dk):
    n, d = x.shape
    tm = ROW_TILE_PROJ
    per_b = seq // tm
    w_sb = qg.shape[1]
    n_conv = conv_w.shape[1]
    n_gate = w.shape[1] - n_conv - LANES - 3 * w_sb
    bmap = lambda i: (i // per_b, 0, 0)
    row = lambda i: (i, 0)
    const = lambda i: (0, 0)
    return pl.pallas_call(
        functools.partial(_hyb_proj_kernel, per_batch=per_b, dk=gdn_dk),
        grid=(n // tm,),
        in_specs=[pl.BlockSpec((tm, d), row),
                  pl.BlockSpec((1, d), const),
                  pl.BlockSpec((1, 1, d), bmap),
                  pl.BlockSpec((1, 1, d), bmap),
                  _resident(w.shape, const),
                  pl.BlockSpec(conv_w.shape, const),
                  _resident(gmat.shape, const),
                  pl.BlockSpec((1, w_sb), const),
                  pl.BlockSpec((1, w_sb), const)],
        out_specs=[pl.BlockSpec((tm, n_conv + n_gate + LANES), row),
                   pl.BlockSpec((tm, 3 * w_sb), row)],
        out_shape=[jax.ShapeDtypeStruct((n, n_conv + n_gate + LANES), F32),
                   jax.ShapeDtypeStruct((n, 3 * w_sb), BF16)],
        scratch_shapes=[pltpu.VMEM((CONV_PAD + tm, n_conv), F32)],
        compiler_params=_params(1),
        name="hybrid_in_proj",
    )(x, gain, sc, sh, w, conv_w, gmat, qg, kg)


def _dot3s(a, b):
    ah, al = a
    bh, bl = b
    n = bh.shape[1]
    both = _dot(ah, jnp.concatenate([bh, bl], axis=1))
    return (both[:, :n] + both[:, n:]) + _dot(al, bh)


def _lower_child_rows(x, size):
    return jnp.concatenate([x[r + size:r + 2 * size] for r in range(0, x.shape[0], 2 * size)], axis=0)


def _unit_lower_inverse_all(lowers, eye, row, col):
    bits = 3
    same = (row >> bits) == (col >> bits)
    diags = [jnp.where(same, l, 0.0) for l in lowers]
    d_s = [_split2(d) for d in diags]
    x0_s = [_split2(eye - d) for d in diags]
    p2_s = [_split2(_dot3s(d, d)) for d in d_s]
    xs = [(eye - d) + _dot3s(x, p) for d, x, p in zip(diags, x0_s, p2_s)]
    p4_s = [_split2(_dot3s(p, p)) for p in p2_s]
    xs = [x + _dot3s(_split2(x), p) for x, p in zip(xs, p4_s)]
    n = lowers[0].shape[0]
    size = 1 << bits
    while size < n:
        parent = (row >> (bits + 1)) == (col >> (bits + 1))
        child = (row >> bits) == (col >> bits)
        off_s = [_split2(jnp.where(parent, jnp.where(child, 0.0, l), 0.0)) for l in lowers]
        x_s = [_split2(x) for x in xs]
        ys = [_dot3s(_split2(_lower_child_rows(x, size)), o) for x, o in zip(xs, off_s)]
        zs = [_dot3s(_split2(y), x) for y, x in zip(ys, x_s)]
        new_xs = []
        for x, z in zip(xs, zs):
            pieces = []
            for k, r in enumerate(range(0, n, 2 * size)):
                pieces.append(x[r:r + size])
                pieces.append(x[r + size:r + 2 * size] - z[k * size:(k + 1) * size])
            new_xs.append(jnp.concatenate(pieces, axis=0))
        xs = new_xs
        bits += 1
        size *= 2
    return xs


def _gdn_kernel(q_ref, k_ref, v_ref, gate_ref, ab_ref, alog_ref, dtb_ref, gn_ref, o_ref, state_ref):
    t = pl.program_id(1)
    tb, width = q_ref.shape
    c = GDN_BLOCK
    n_heads = state_ref.shape[0]
    dk = width // n_heads
    n_chunks = tb // c

    @pl.when(t == 0)
    def _():
        state_ref[...] = jnp.zeros_like(state_ref)

    lane = lax.broadcasted_iota(jnp.int32, (tb, LANES), 1)
    ab = ab_ref[...]
    g_lanes = -jnp.exp(alog_ref[...]) * _softplus(ab + dtb_ref[...])
    beta_lanes = jax.nn.sigmoid(ab)

    def lane_column(x, idx):
        return jnp.broadcast_to(jnp.sum(jnp.where(lane == idx, x, 0.0), axis=1, keepdims=True), (tb, LANES))

    row = lax.broadcasted_iota(jnp.int32, (c, c), 0)
    col = lax.broadcasted_iota(jnp.int32, (c, c), 1)
    incl = row >= col
    strict = row > col
    eye = jnp.where(row == col, 1.0, 0.0).astype(F32)
    tri = jnp.where(incl, 1.0, 0.0).astype(BF16)

    items = []
    for h in range(n_heads):
        g_h = lane_column(g_lanes, h)
        beta_h = lane_column(beta_lanes, h + n_heads)
        cols = slice(h * dk, (h + 1) * dk)
        for n in range(n_chunks):
            rows = slice(n * c, (n + 1) * c)
            items.append(dict(q=q_ref[rows, cols], k=k_ref[rows, cols], v=v_ref[rows, cols],
                              beta=beta_h[rows], g=g_h[rows]))

    def prepare(group):
        for it in group:
            it["gcol"] = _dot_exact_lhs(tri, it["g"])
        for it in group:
            gcol = it["gcol"]
            it["dec"] = jnp.exp(jnp.where(incl, gcol - gcol.T, 0.0))
            it["eg"] = jnp.exp(gcol)
            it["g_last"] = gcol[c - 1:c, :]
            it["kb"] = it["k"] * it["beta"]
            it["kk"] = it["k"].astype(BF16)
        for it in group:
            it["lower"] = jnp.where(strict, _dot_nt(it["kb"].astype(BF16), it["kk"]) * it["dec"], 0.0)
        tinvs = _unit_lower_inverse_all([it["lower"] for it in group], eye, row, col)
        for it, tinv in zip(group, tinvs):
            rhs = jnp.concatenate([it["v"] * it["beta"], it["kb"] * it["eg"]], axis=1)
            uw = _dot3(tinv, rhs)
            it["u"] = uw[:, :dk]
            a = jnp.where(incl, _dot_nt(it["q"].astype(BF16), it["kk"]) * it["dec"], 0.0)
            q_dec = it["q"] * it["eg"]
            k_dec = it["k"] * jnp.exp(it["g_last"] - it["gcol"])
            it["wq"] = jnp.concatenate([uw[:, dk:], q_dec], axis=0).astype(BF16)
            it["ak"] = jnp.concatenate([a, k_dec.T], axis=0).astype(BF16)
            it["decay"] = jnp.exp(it["g_last"])

    early = n_chunks // 2
    prepare([it for i, it in enumerate(items) if i % n_chunks < early])
    prepare([it for i, it in enumerate(items) if i % n_chunks >= early])

    gn = gn_ref[...]
    states = [state_ref[h] for h in range(n_heads)]
    for n in range(n_chunks):
        rows = slice(n * c, (n + 1) * c)
        for h in range(n_heads):
            it = items[h * n_chunks + n]
            cols = slice(h * dk, (h + 1) * dk)
            ws_qs = _dot(it["wq"], states[h].astype(BF16))
            v_new = it["u"] - ws_qs[:c]
            av_kv = _dot(it["ak"], v_new.astype(BF16))
            o = ws_qs[c:] + av_kv[:c]
            states[h] = states[h] * it["decay"] + av_kv[c:]
            ms = jnp.mean(o * o, axis=-1, keepdims=True)
            o = o * lax.rsqrt(ms + EPS) * gn
            o_ref[rows, cols] = (o * _silu(gate_ref[rows, cols])).astype(o_ref.dtype)
    for h in range(n_heads):
        state_ref[h] = states[h]


def _gdn(proj, a_log, dt_bias, gdn_norm, batch, seq):
    n = proj.shape[0]
    width = (proj.shape[1] - LANES) // 4
    n_heads = a_log.shape[0]
    dk = width // n_heads
    tb = GDN_STEP
    nt = seq // tb
    rows = lambda b, t: b * nt + t
    pad = lambda x: jnp.zeros((1, LANES), F32).at[0, :n_heads].set(x)
    col_spec = lambda k: pl.BlockSpec((tb, width), lambda b, t: (rows(b, t), k))
    const = lambda b, t: (0, 0)
    return pl.pallas_call(
        _gdn_kernel,
        grid=(batch, nt),
        in_specs=[col_spec(0), col_spec(1), col_spec(2), col_spec(3),
                  pl.BlockSpec((tb, LANES), lambda b, t: (rows(b, t), 4 * width // LANES)),
                  pl.BlockSpec((1, LANES), const), pl.BlockSpec((1, LANES), const),
                  pl.BlockSpec((1, dk), const)],
        out_specs=pl.BlockSpec((tb, width), lambda b, t: (rows(b, t), 0)),
        out_shape=jax.ShapeDtypeStruct((n, width), BF16),
        scratch_shapes=[pltpu.VMEM((n_heads, dk, dk), F32)],
        compiler_params=_params(2),
        name="gated_delta_rule",
    )(proj, proj, proj, proj, proj, pad(a_log), pad(dt_bias), gdn_norm.reshape(1, dk))


def _sb_kernel(q_ref, k_ref, v_ref, m_ref, o_ref, q2_ref, acc_ref, carry_ref):
    i = pl.program_id(2)
    tq = SB_TILE
    n_tiles = q_ref.shape[0] // tq
    lane = lax.broadcasted_iota(jnp.int32, (tq, LANES), 1)
    suffix = m_ref[...]
    strip = tq // SB_STRIPS

    for t in range(n_tiles):
        q = q_ref[t * tq:(t + 1) * tq, :]
        zero = jnp.zeros_like(q)
        q2_ref[2 * t * tq:(2 * t + 1) * tq, :] = jnp.where(lane < SB_DH, q, zero)
        q2_ref[(2 * t + 1) * tq:(2 * t + 2) * tq, :] = jnp.where(lane >= SB_DH, q, zero)

    def strips_of(t):
        return [slice(2 * t * tq + n * strip, 2 * t * tq + (n + 1) * strip) for n in range(2 * SB_STRIPS)]

    def weigh(pairs):
        work = []
        for t, j, diagonal in pairs:
            start = pl.multiple_of(j * tq, tq)
            k = k_ref[pl.ds(start, tq), :]
            v = v_ref[pl.ds(start, tq), :]
            for n, rows in enumerate(strips_of(t)):
                work.append(dict(n=n, rows=rows, k=k, v=v, diagonal=diagonal))
        for w in work:
            w["y"] = _dot_nt(q2_ref[w["rows"], :], w["k"]) * LOG2_E
        for w in work:
            y = w["y"]
            w["soft"] = jnp.maximum(y, 0.0) + jnp.log2(1.0 + jnp.exp2(-jnp.abs(y)))
            if w["diagonal"]:
                s = lax.broadcasted_iota(jnp.int32, (strip, tq), 1)
                r = lax.broadcasted_iota(jnp.int32, (strip, tq), 0) + (w["n"] % SB_STRIPS) * strip
                w["mask"] = s < r
                w["cost"] = jnp.where(w["mask"], w["soft"], 0.0)
            else:
                w["cost"] = w["soft"]
        for w in work:
            w["later"] = _dot_exact_rhs(w["cost"], suffix)
        for w in work:
            a = jnp.exp2(w["y"] - w["soft"] - w["later"])
            if w["diagonal"]:
                a = jnp.where(w["mask"], a, 0.0)
            w["part"] = _dot(a.astype(BF16), w["v"])
            w["total"] = jnp.sum(w["cost"], axis=1, keepdims=True)
        per_pair = 2 * SB_STRIPS
        return [work[p * per_pair:(p + 1) * per_pair] for p in range(len(pairs))]

    first = i * n_tiles
    pairs = [(t, first + t, True) for t in range(n_tiles)]
    pairs += [(t, jnp.maximum(first + t - 1, 0), False) for t in range(n_tiles)]
    results = weigh(pairs)
    for t in range(n_tiles):
        has_prev = (first + t > 0).astype(F32)
        for d, p in zip(results[t], results[n_tiles + t]):
            acc_ref[d["rows"], :] = d["part"] + p["part"] * (jnp.exp2(-d["total"]) * has_prev)
            carry_ref[d["rows"], :] = d["total"] + p["total"] * has_prev

    @pl.when(jnp.min(carry_ref[...]) < SB_UNDERFLOW_LOG2)
    def _():
        for t in range(n_tiles):
            rows_t = slice(2 * t * tq, 2 * (t + 1) * tq)
            remaining = first + t - 1

            def cond(state):
                step, least = state
                return jnp.logical_and(step < remaining, least < SB_UNDERFLOW_LOG2)

            def body(state, t=t, rows_t=rows_t, remaining=remaining):
                step, _ = state
                (tile,) = weigh([(t, remaining - 1 - step, False)])
                for w in tile:
                    carry = carry_ref[w["rows"], :]
                    acc_ref[w["rows"], :] += w["part"] * jnp.exp2(-carry)
                    carry_ref[w["rows"], :] = carry + w["total"]
                return step + 1, jnp.min(carry_ref[rows_t, :])

            lax.while_loop(cond, body, (jnp.int32(0), jnp.min(carry_ref[rows_t, :])))

    for t in range(n_tiles):
        acc0 = acc_ref[2 * t * tq:(2 * t + 1) * tq, :]
        acc1 = acc_ref[(2 * t + 1) * tq:(2 * t + 2) * tq, :]
        o_ref[t * tq:(t + 1) * tq, :] = jnp.where(lane < SB_DH, acc0, acc1).astype(o_ref.dtype)


def _sb_attention(qkv, suffix, batch, seq):
    n = qkv.shape[0]
    width = qkv.shape[1] // 3
    pairs = width // LANES
    tq = SB_TILE
    rows = SB_TILES_PER_STEP * tq
    nq = seq // rows
    return pl.pallas_call(
        _sb_kernel,
        grid=(batch, pairs, nq),
        in_specs=[pl.BlockSpec((rows, LANES), lambda b, p, i: (b * nq + i, p)),
                  pl.BlockSpec((seq, LANES), lambda b, p, i: (b, pairs + p)),
                  pl.BlockSpec((seq, LANES), lambda b, p, i: (b, 2 * pairs + p)),
                  pl.BlockSpec((tq, tq), lambda b, p, i: (0, 0))],
        out_specs=pl.BlockSpec((rows, LANES), lambda b, p, i: (b * nq + i, p)),
        out_shape=jax.ShapeDtypeStruct((n, width), BF16),
        scratch_shapes=[pltpu.VMEM((2 * rows, LANES), BF16),
                        pltpu.VMEM((2 * rows, LANES), F32),
                        pltpu.VMEM((2 * rows, 1), F32)],
        compiler_params=_params(3),
        name="stick_breaking_attention",
    )(qkv, qkv, qkv, suffix)


def _ret_proj_kernel(x_ref, g_ref, sc_ref, sh_ref, w_ref, cos_ref, sin_ref, qs_ref, ks_ref,
                     q_ref, k_ref, v_ref, gate_ref):
    tm = x_ref.shape[0]
    n_heads, chunk, _ = qs_ref.shape
    w_qk = q_ref.shape[1]
    dk = w_qk // n_heads
    half = dk // 2
    groups = [slice(r0, r0 + chunk) for r0 in range(0, tm, chunk)]
    hbs = [_norm_mod(x_ref[rows, :], g_ref[...], sc_ref[0], sh_ref[0]).astype(BF16) for rows in groups]

    def rot(x, rows, scale):
        cos = cos_ref[rows, :]
        sin = sin_ref[rows, :]
        x1, x2 = x[:, :half], x[:, half:]
        return jnp.concatenate([(x1 * cos - x2 * sin) * scale, (x1 * sin + x2 * cos) * scale], axis=-1)

    for h in range(n_heads):
        cols = slice(h * dk, (h + 1) * dk)
        for rows, hb in zip(groups, hbs):
            q_ref[rows, cols] = rot(_dot(hb, w_ref[:, cols]), rows, qs_ref[h]).astype(BF16)
    for h in range(n_heads):
        cols = slice(h * dk, (h + 1) * dk)
        for rows, hb in zip(groups, hbs):
            pre = _dot(hb, w_ref[:, w_qk + h * dk:w_qk + (h + 1) * dk])
            k_ref[rows, cols] = rot(pre, rows, ks_ref[h]).astype(BF16)
    step = 512
    off = 2 * w_qk
    for ref in (v_ref, gate_ref):
        for o in range(0, ref.shape[1], step):
            for rows, hb in zip(groups, hbs):
                ref[rows, o:o + step] = _dot(hb, w_ref[:, off + o:off + o + step]).astype(ref.dtype)
        off += ref.shape[1]


def _ret_proj(x, gain, sc, sh, w, cos, sin, q_scale, k_scale, n_qk, n_v, seq):
    n, d = x.shape
    tm = ROW_TILE_PROJ
    per_b = seq // tm
    bmap = lambda i: (i // per_b, 0, 0)
    row = lambda i: (i, 0)
    const = lambda i: (0, 0)
    pos = lambda i: (i % per_b, 0)
    return pl.pallas_call(
        _ret_proj_kernel,
        grid=(n // tm,),
        in_specs=[pl.BlockSpec((tm, d), row),
                  pl.BlockSpec((1, d), const),
                  pl.BlockSpec((1, 1, d), bmap),
                  pl.BlockSpec((1, 1, d), bmap),
                  _resident(w.shape, const),
                  pl.BlockSpec((tm, cos.shape[1]), pos),
                  pl.BlockSpec((tm, cos.shape[1]), pos),
                  pl.BlockSpec(q_scale.shape, lambda i: (0, 0, 0)),
                  pl.BlockSpec(k_scale.shape, lambda i: (0, 0, 0))],
        out_specs=[pl.BlockSpec((tm, n_qk), row),
                   pl.BlockSpec((tm, n_qk), row),
                   pl.BlockSpec((tm, n_v), row),
                   pl.BlockSpec((tm, n_v), row)],
        out_shape=[jax.ShapeDtypeStruct((n, n_qk), BF16),
                   jax.ShapeDtypeStruct((n, n_qk), BF16),
                   jax.ShapeDtypeStruct((n, n_v), BF16),
                   jax.ShapeDtypeStruct((n, n_v), F32)],
        compiler_params=_params(1),
        name="retention_in_proj",
    )(x, gain, sc, sh, w, cos, sin, q_scale, k_scale)


def _ret_kernel(q_ref, k_ref, v_ref, gate_ref, decay_ref, o_ref, state_ref):
    t = pl.program_id(1)
    c = RET_CHUNK
    n_heads = state_ref.shape[0]
    dk = q_ref.shape[1] // n_heads
    dv = v_ref.shape[1] // n_heads

    @pl.when(t == 0)
    def _():
        state_ref[...] = jnp.zeros_like(state_ref)

    row = lax.broadcasted_iota(jnp.int32, (c, c), 0)
    col = lax.broadcasted_iota(jnp.int32, (c, c), 1)
    causal = row >= col

    for r0 in range(0, q_ref.shape[0], c):
        rows = slice(r0, r0 + c)
        for h in range(n_heads):
            q = q_ref[rows, h * dk:(h + 1) * dk]
            k = k_ref[rows, h * dk:(h + 1) * dk]
            v = v_ref[rows, h * dv:(h + 1) * dv]
            intra = jnp.where(causal, _dot_nt(q, k), 0.0)
            state = state_ref[h]
            o = _dot(intra.astype(BF16), v) + _dot(q, state.astype(BF16))
            k_t = k.astype(F32).T.astype(BF16)
            state_ref[h] = (state + _dot(k_t, v)) * decay_ref[h]
            ms = jnp.mean(o * o, axis=-1, keepdims=True)
            o = o * lax.rsqrt(ms + EPS)
            gate = gate_ref[rows, h * dv:(h + 1) * dv]
            o_ref[rows, h * dv:(h + 1) * dv] = (o * _silu(gate)).astype(o_ref.dtype)


def _retention(q, k, v, gate, chunk_decay, batch, seq):
    n, w_qk = q.shape
    n_heads = chunk_decay.shape[0]
    w_v = v.shape[1]
    c = RET_CHUNKS_PER_STEP * RET_CHUNK
    nt = seq // c
    rows = lambda b, t: (b * nt + t, 0)
    return pl.pallas_call(
        _ret_kernel,
        grid=(batch, nt),
        in_specs=[pl.BlockSpec((c, w_qk), rows),
                  pl.BlockSpec((c, w_qk), rows),
                  pl.BlockSpec((c, w_v), rows),
                  pl.BlockSpec((c, w_v), rows),
                  pl.BlockSpec(chunk_decay.shape, lambda b, t: (0, 0, 0))],
        out_specs=pl.BlockSpec((c, w_v), rows),
        out_shape=jax.ShapeDtypeStruct((n, w_v), BF16),
        scratch_shapes=[pltpu.VMEM((n_heads, w_qk // n_heads, w_v // n_heads), F32)],
        compiler_params=_params(2),
        name="retention_chunkwise",
    )(q, k, v, gate, chunk_decay)


def _post_kernel(*refs, n_mix):
    x_ref = refs[0]
    o_refs = refs[1:1 + n_mix]
    wm_refs = refs[1 + n_mix:1 + 2 * n_mix]
    gtm_ref, g_ref, sc_ref, sh_ref, gtf_ref, win_ref, wout_ref, out_ref, acc_ref = refs[1 + 2 * n_mix:]
    y = _dot(o_refs[0][...], wm_refs[0][...])
    for o_ref, w_ref in zip(o_refs[1:], wm_refs[1:]):
        y = y + _dot(o_ref[...], w_ref[...])
    x1 = x_ref[...] + gtm_ref[0] * y
    hb = _norm_mod(x1, g_ref[...], sc_ref[0], sh_ref[0]).astype(BF16)
    hidden = wout_ref.shape[0]
    for idx, off in enumerate(range(0, hidden, FFN_CHUNK)):
        g = _dot(hb, win_ref[:, off:off + FFN_CHUNK])
        u = _dot(hb, win_ref[:, hidden + off:hidden + off + FFN_CHUNK])
        part = _dot((_silu(g) * u).astype(BF16), wout_ref[off:off + FFN_CHUNK, :])
        if idx == 0:
            acc_ref[...] = part
        else:
            acc_ref[...] += part
    out_ref[...] = x1 + gtf_ref[0] * acc_ref[...]


def _post(x, mix_outs, mix_ws, gt_m, gain, sc, sh, gt_f, w_in, w_out, seq):
    n, d = x.shape
    tm = ROW_TILE_FFN
    per_b = seq // tm
    bmap = lambda i: (i // per_b, 0, 0)
    row = lambda i: (i, 0)
    const = lambda i: (0, 0)
    vec = pl.BlockSpec((1, 1, d), bmap)
    n_mix = len(mix_outs)
    return pl.pallas_call(
        functools.partial(_post_kernel, n_mix=n_mix),
        grid=(n // tm,),
        in_specs=([pl.BlockSpec((tm, d), row)]
                  + [pl.BlockSpec((tm, o.shape[1]), row) for o in mix_outs]
                  + [_resident(w.shape, const) for w in mix_ws]
                  + [vec, pl.BlockSpec((1, d), const), vec, vec, vec,
                     _resident(w_in.shape, const), _resident(w_out.shape, const)]),
        out_specs=pl.BlockSpec((tm, d), row),
        out_shape=jax.ShapeDtypeStruct((n, d), F32),
        scratch_shapes=[pltpu.VMEM((tm, d), F32)],
        compiler_params=_params(1),
        name="out_proj_swiglu",
    )(x, *mix_outs, *mix_ws, gt_m, gain, sc, sh, gt_f, w_in, w_out)


def kernel(x, c, ada_w, ada_b, norm_mix, norm_ffn, hyb_w_in, hyb_conv, gdn_a_log, gdn_dt_bias, gdn_norm, sb_q_norm, sb_k_norm, hyb_w_out, ret_w_in, ret_w_out, ffn_w_in, ffn_w_out):
    batch, seq, d = x.shape
    depth = ada_w.shape[0]
    gdn_heads = gdn_a_log.shape[1]
    gdn_dv = gdn_norm.shape[1]
    gdn_w = gdn_heads * gdn_dv
    sb_dh = sb_q_norm.shape[1]
    assert sb_dh == SB_DH
    sb_w = hyb_w_out.shape[1] - gdn_w
    sb_heads = sb_w // sb_dh
    ret_heads = 4
    ret_qk = ret_w_in.shape[2] // 6
    ret_v = 2 * ret_qk
    ret_dk = ret_qk // ret_heads

    xf = x.reshape(batch * seq, d)
    mod = _modulation(c, ada_w, ada_b)

    head_mean = jnp.kron(jnp.eye(MXU_DIM // sb_dh, dtype=F32), jnp.full((sb_dh, sb_dh), 1.0 / sb_dh, F32)).astype(BF16)
    idx = jnp.arange(SB_TILE)
    suffix = (idx[:, None] > idx[None, :]).astype(BF16)
    pos = jnp.arange(seq, dtype=F32)
    inv = 1.0 / (ROPE_BASE ** (jnp.arange(0, ret_dk, 2, dtype=F32) / ret_dk))
    ang = pos[:, None] * inv[None, :]
    cos, sin = jnp.cos(ang), jnp.sin(ang)
    log_gamma = jnp.log1p(-jnp.exp2(-5.0 - jnp.arange(ret_heads, dtype=F32)))
    steps = jnp.arange(1, RET_CHUNK + 1, dtype=F32)
    ret_q_scale = jnp.broadcast_to(jnp.exp(log_gamma[:, None] * steps[None, :])[:, :, None],
                                   (ret_heads, RET_CHUNK, ret_dk // 2))
    ret_k_scale = jnp.broadcast_to((jnp.exp(-log_gamma[:, None] * steps[None, :]) * ret_dk ** -0.5)[:, :, None],
                                   (ret_heads, RET_CHUNK, ret_dk // 2))
    ret_chunk_decay = jnp.broadcast_to(jnp.exp(log_gamma * RET_CHUNK)[:, None, None], (ret_heads, 1, ret_v // ret_heads))

    for l in range(depth):
        sh_m, sc_m, gt_m, sh_f, sc_f, gt_f = [m.reshape(batch, 1, d) for m in jnp.split(mod[l], 6, axis=-1)]
        i = l // 2
        gain_m = norm_mix[l].reshape(1, d)
        if l % 2 == 0:
            w_in = hyb_w_in[i]
            n_main = 4 * gdn_w
            n_ab = 2 * gdn_heads
            ab_cols = jnp.zeros((d, LANES), F32).at[:, :n_ab].set(w_in[:, n_main:n_main + n_ab])
            w_re = jnp.concatenate([w_in[:, :n_main], ab_cols, w_in[:, n_main + n_ab:]], axis=1).astype(BF16)
            gdn_proj, sb_qkv = _hyb_proj(
                xf, gain_m, sc_m, sh_m, w_re, hyb_conv[i], head_mean,
                jnp.tile(sb_q_norm[i], sb_heads).reshape(1, sb_w),
                jnp.tile(sb_k_norm[i], sb_heads).reshape(1, sb_w), seq, gdn_dv)
            o_a = _gdn(gdn_proj, gdn_a_log[i], gdn_dt_bias[i], gdn_norm[i], batch, seq)
            o_b = _sb_attention(sb_qkv, suffix, batch, seq)
            w_out = hyb_w_out[i].astype(BF16)
            mix_outs = [o_a, o_b]
            mix_ws = [w_out[:gdn_w], w_out[gdn_w:]]
        else:
            rq, rk, rv, gate = _ret_proj(xf, gain_m, sc_m, sh_m, ret_w_in[i].astype(BF16), cos, sin,
                                         ret_q_scale, ret_k_scale, ret_qk, ret_v, seq)
            o_r = _retention(rq, rk, rv, gate, ret_chunk_decay, batch, seq)
            mix_outs = [o_r]
            mix_ws = [ret_w_out[i].astype(BF16)]
        xf = _post(xf, mix_outs, mix_ws, gt_m, norm_ffn[l].reshape(1, d), sc_f, sh_f, gt_f,
                   ffn_w_in[l].astype(BF16), ffn_w_out[l].astype(BF16), seq)
    return xf.reshape(batch, seq, d)
```

```python
import functools

import jax
import jax.numpy as jnp
from jax import lax
from jax.experimental import pallas as pl
from jax.experimental.pallas import tpu as pltpu

F32 = jnp.float32
BF16 = jnp.bfloat16
EPS = 1e-6
ROPE_BASE = 10000.0

LANES = 128
MXU_DIM = 256
VMEM_LIMIT = 56 * 1024 * 1024

GDN_CONV = 4
GDN_BLOCK = 128
GDN_STEP = 512
SB_TILE = 256
SB_DH = 64
SB_STRIPS = 2
SB_TILES_PER_STEP = 4
LOG2_E = 1.4426950408889634
SB_UNDERFLOW_LOG2 = 180.0
RET_CHUNK = 256
RET_CHUNKS_PER_STEP = 2
ROW_TILE_PROJ = 512
PROJ_GROUP = 256
ROW_TILE_FFN = 512
FFN_CHUNK = 256
CONV_PAD = 8


def _dot(a, b):
    return jnp.dot(a, b, preferred_element_type=F32)


def _dot_nt(a, b):
    return lax.dot_general(a, b, (((1,), (1,)), ((), ())), preferred_element_type=F32)


def _split2(a):
    hi = a.astype(BF16)
    lo = (a - hi.astype(F32)).astype(BF16)
    return hi, lo


def _dot3(a, b):
    ah, al = _split2(a)
    bh, bl = _split2(b)
    n = b.shape[1]
    if 2 * n <= MXU_DIM:
        both = _dot(ah, jnp.concatenate([bh, bl], axis=1))
        return (both[:, :n] + both[:, n:]) + _dot(al, bh)
    return _dot(ah, bh) + (_dot(ah, bl) + _dot(al, bh))


def _dot_exact_rhs(a, m):
    ah, al = _split2(a)
    return _dot(ah, m) + _dot(al, m)


def _dot_exact_lhs(m, a):
    ah, al = _split2(a)
    n = a.shape[1]
    if 2 * n <= MXU_DIM:
        both = _dot(m, jnp.concatenate([ah, al], axis=1))
        return both[:, :n] + both[:, n:]
    return _dot(m, ah) + _dot(m, al)


def _silu(x):
    return x * jax.nn.sigmoid(x)


def _softplus(x):
    return jnp.maximum(x, 0.0) + jnp.log1p(jnp.exp(-jnp.abs(x)))


def _params(n_axes):
    return pltpu.CompilerParams(dimension_semantics=("arbitrary",) * n_axes,
                                vmem_limit_bytes=VMEM_LIMIT)


def _resident(shape, index_map):
    return pl.BlockSpec(shape, index_map, pipeline_mode=pl.Buffered(1))


def _mod_kernel(c_ref, w_ref, b_ref, o_ref):
    c = c_ref[...]
    o_ref[0] = _dot(_silu(c).astype(BF16), w_ref[0].astype(BF16)) + b_ref[0]


def _modulation(c, ada_w, ada_b):
    depth, d, n = ada_w.shape
    b = c.shape[0]
    rows = 8
    c_pad = jnp.zeros((rows, d), F32).at[:b].set(c)
    tn = 1024
    out = pl.pallas_call(
        _mod_kernel,
        grid=(depth, n // tn),
        in_specs=[pl.BlockSpec((rows, d), lambda l, j: (0, 0)),
                  pl.BlockSpec((1, d, tn), lambda l, j: (l, 0, j)),
                  pl.BlockSpec((1, 1, tn), lambda l, j: (l, 0, j))],
        out_specs=pl.BlockSpec((1, rows, tn), lambda l, j: (l, 0, j)),
        out_shape=jax.ShapeDtypeStruct((depth, rows, n), F32),
        compiler_params=_params(2),
        name="adaln_modulation",
    )(c_pad, ada_w, ada_b.reshape(depth, 1, n))
    return out[:, :b]


def _norm_mod(x, gain, sc, sh):
    ms = jnp.mean(x * x, axis=-1, keepdims=True)
    y = x * lax.rsqrt(ms + EPS)
    y = y * gain
    return y * (1.0 + sc) + sh


def _hyb_proj_kernel(x_ref, g_ref, sc_ref, sh_ref, w_ref, cw_ref, gm_ref, qg_ref, kg_ref,
                     gdn_ref, sb_ref, ext_ref, *, per_batch, dk):
    i = pl.program_id(0)
    tm = x_ref.shape[0]
    n_conv = cw_ref.shape[1]
    n_gate = gdn_ref.shape[1] - n_conv - LANES
    w_sb = qg_ref.shape[1]
    groups = [slice(r0, r0 + PROJ_GROUP) for r0 in range(0, tm, PROJ_GROUP)]
    hbs = [_norm_mod(x_ref[rows, :], g_ref[...], sc_ref[0], sh_ref[0]).astype(BF16) for rows in groups]

    @pl.when(i % per_batch == 0)
    def _():
        ext_ref[0:CONV_PAD, :] = jnp.zeros((CONV_PAD, n_conv), F32)

    def l2n(y):
        return y * lax.rsqrt(jnp.sum(y * y, axis=-1, keepdims=True) + EPS)

    for c0 in range(0, n_conv, MXU_DIM):
        for rows, hb in zip(groups, hbs):
            ext_ref[CONV_PAD + rows.start:CONV_PAD + rows.stop, c0:c0 + MXU_DIM] = _dot(hb, w_ref[:, c0:c0 + MXU_DIM])

    def conv_chunk(c0, rows):
        cols = slice(c0, c0 + MXU_DIM)
        w = cw_ref[:, cols]
        y = ext_ref[CONV_PAD + rows.start:CONV_PAD + rows.stop, cols] * w[GDN_CONV - 1:GDN_CONV, :]
        for j in range(GDN_CONV - 1):
            shift = GDN_CONV - 1 - j
            y = y + ext_ref[CONV_PAD + rows.start - shift:CONV_PAD + rows.stop - shift, cols] * w[j:j + 1, :]
        y = _silu(y)
        if c0 < 2 * n_gate:
            y = jnp.concatenate([l2n(y[:, h0:h0 + dk]) for h0 in range(0, MXU_DIM, dk)], axis=1)
            if c0 < n_gate:
                y = y * (dk ** -0.5)
        gdn_ref[rows, cols] = y

    def head_rms(xf, gain):
        sq = xf * xf
        ms = jnp.concatenate([_dot_exact_rhs(sq[:, c0:c0 + MXU_DIM], gm_ref[...])
                              for c0 in range(0, xf.shape[1], MXU_DIM)], axis=1)
        return xf * lax.rsqrt(ms + EPS) * gain

    off_gate = n_conv
    off_ab = off_gate + n_gate
    off_q = off_ab + LANES
    off_k = off_q + w_sb
    off_v = off_k + w_sb
    conv_starts = list(range(0, n_conv, MXU_DIM))
    for rows, hb in zip(groups, hbs):
        gdn_ref[rows, off_gate:off_ab] = _dot(hb, w_ref[:, off_gate:off_gate + n_gate])
    for rows, hb in zip(groups, hbs):
        sq = _dot(hb, w_ref[:, off_q:off_q + w_sb])
        sb_ref[rows, 0:w_sb] = (head_rms(sq, qg_ref[...]) * (SB_DH ** -0.5)).astype(BF16)
    for rows, hb in zip(groups, hbs):
        sk = _dot(hb, w_ref[:, off_k:off_k + w_sb])
        sb_ref[rows, w_sb:2 * w_sb] = head_rms(sk, kg_ref[...]).astype(BF16)
    for rows, hb in zip(groups, hbs):
        sb_ref[rows, 2 * w_sb:3 * w_sb] = _dot(hb, w_ref[:, off_v:off_v + w_sb]).astype(BF16)
        gdn_ref[rows, off_ab:off_q] = _dot(hb, w_ref[:, off_ab:off_ab + LANES])
    for c0 in conv_starts:
        for rows in groups:
            conv_chunk(c0, rows)
    ext_ref[0:CONV_PAD, :] = ext_ref[tm:tm + CONV_PAD, :]


def _hyb_proj(x, gain, sc, sh, w, conv_w, gmat, qg, kg, seq, gdn_dk):
    n, d = x.shape
    tm = ROW_TILE_PROJ
    per_b = seq // tm
    w_sb = qg.shape[1]
    n_conv = conv_w.shape[1]
    n_gate = w.shape[1] - n_conv - LANES - 3 * w_sb
    bmap = lambda i: (i // per_b, 0, 0)
    row = lambda i: (i, 0)
    const = lambda i: (0, 0)
    return pl.pallas_call(
        functools.partial(_hyb_proj_kernel, per_batch=per_b, dk=gdn_dk),
        grid=(n // tm,),
        in_specs=[pl.BlockSpec((tm, d), row),
                  pl.BlockSpec((1, d), const),
                  pl.BlockSpec((1, 1, d), bmap),
                  pl.BlockSpec((1, 1, d), bmap),
                  _resident(w.shape, const),
                  pl.BlockSpec(conv_w.shape, const),
                  _resident(gmat.shape, const),
                  pl.BlockSpec((1, w_sb), const),
                  pl.BlockSpec((1, w_sb), const)],
        out_specs=[pl.BlockSpec((tm, n_conv + n_gate + LANES), row),
                   pl.BlockSpec((tm, 3 * w_sb), row)],
        out_shape=[jax.ShapeDtypeStruct((n, n_conv + n_gate + LANES), F32),
                   jax.ShapeDtypeStruct((n, 3 * w_sb), BF16)],
        scratch_shapes=[pltpu.VMEM((CONV_PAD + tm, n_conv), F32)],
        compiler_params=_params(1),
        name="hybrid_in_proj",
    )(x, gain, sc, sh, w, conv_w, gmat, qg, kg)


def _dot3s(a, b):
    ah, al = a
    bh, bl = b
    n = bh.shape[1]
    both = _dot(ah, jnp.concatenate([bh, bl], axis=1))
    return (both[:, :n] + both[:, n:]) + _dot(al, bh)


def _lower_child_rows(x, size):
    return jnp.concatenate([x[r + size:r + 2 * size] for r in range(0, x.shape[0], 2 * size)], axis=0)


def _unit_lower_inverse_all(lowers, eye, row, col):
    bits = 3
    same = (row >> bits) == (col >> bits)
    diags = [jnp.where(same, l, 0.0) for l in lowers]
    d_s = [_split2(d) for d in diags]
    x0_s = [_split2(eye - d) for d in diags]
    p2_s = [_split2(_dot3s(d, d)) for d in d_s]
    xs = [(eye - d) + _dot3s(x, p) for d, x, p in zip(diags, x0_s, p2_s)]
    p4_s = [_split2(_dot3s(p, p)) for p in p2_s]
    xs = [x + _dot3s(_split2(x), p) for x, p in zip(xs, p4_s)]
    n = lowers[0].shape[0]
    size = 1 << bits
    while size < n:
        parent = (row >> (bits + 1)) == (col >> (bits + 1))
        child = (row >> bits) == (col >> bits)
        off_s = [_split2(jnp.where(parent, jnp.where(child, 0.0, l), 0.0)) for l in lowers]
        x_s = [_split2(x) for x in xs]
        ys = [_dot3s(_split2(_lower_child_rows(x, size)), o) for x, o in zip(xs, off_s)]
        zs = [_dot3s(_split2(y), x) for y, x in zip(ys, x_s)]
        new_xs = []
        for x, z in zip(xs, zs):
            pieces = []
            for k, r in enumerate(range(0, n, 2 * size)):
                pieces.append(x[r:r + size])
                pieces.append(x[r + size:r + 2 * size] - z[k * size:(k + 1) * size])
            new_xs.append(jnp.concatenate(pieces, axis=0))
        xs = new_xs
        bits += 1
        size *= 2
    return xs


def _gdn_kernel(q_ref, k_ref, v_ref, gate_ref, ab_ref, alog_ref, dtb_ref, gn_ref, o_ref, state_ref):
    t = pl.program_id(1)
    tb, width = q_ref.shape
    c = GDN_BLOCK
    n_heads = state_ref.shape[0]
    dk = width // n_heads
    n_chunks = tb // c

    @pl.when(t == 0)
    def _():
        state_ref[...] = jnp.zeros_like(state_ref)

    lane = lax.broadcasted_iota(jnp.int32, (tb, LANES), 1)
    ab = ab_ref[...]
    g_lanes = -jnp.exp(alog_ref[...]) * _softplus(ab + dtb_ref[...])
    beta_lanes = jax.nn.sigmoid(ab)

    def lane_column(x, idx):
        return jnp.broadcast_to(jnp.sum(jnp.where(lane == idx, x, 0.0), axis=1, keepdims=True), (tb, LANES))

    row = lax.broadcasted_iota(jnp.int32, (c, c), 0)
    col = lax.broadcasted_iota(jnp.int32, (c, c), 1)
    incl = row >= col
    strict = row > col
    eye = jnp.where(row == col, 1.0, 0.0).astype(F32)
    tri = jnp.where(incl, 1.0, 0.0).astype(BF16)

    items = []
    for h in range(n_heads):
        g_h = lane_column(g_lanes, h)
        beta_h = lane_column(beta_lanes, h + n_heads)
        cols = slice(h * dk, (h + 1) * dk)
        for n in range(n_chunks):
            rows = slice(n * c, (n + 1) * c)
            items.append(dict(q=q_ref[rows, cols], k=k_ref[rows, cols], v=v_ref[rows, cols],
                              beta=beta_h[rows], g=g_h[rows]))

    def prepare(group):
        for it in group:
            it["gcol"] = _dot_exact_lhs(tri, it["g"])
        for it in group:
            gcol = it["gcol"]
            it["dec"] = jnp.exp(jnp.where(incl, gcol - gcol.T, 0.0))
            it["eg"] = jnp.exp(gcol)
            it["g_last"] = gcol[c - 1:c, :]
            it["kb"] = it["k"] * it["beta"]
            it["kk"] = it["k"].astype(BF16)
        for it in group:
            it["lower"] = jnp.where(strict, _dot_nt(it["kb"].astype(BF16), it["kk"]) * it["dec"], 0.0)
        tinvs = _unit_lower_inverse_all([it["lower"] for it in group], eye, row, col)
        for it, tinv in zip(group, tinvs):
            rhs = jnp.concatenate([it["v"] * it["beta"], it["kb"] * it["eg"]], axis=1)
            uw = _dot3(tinv, rhs)
            it["u"] = uw[:, :dk]
            a = jnp.where(incl, _dot_nt(it["q"].astype(BF16), it["kk"]) * it["dec"], 0.0)
            q_dec = it["q"] * it["eg"]
            k_dec = it["k"] * jnp.exp(it["g_last"] - it["gcol"])
            it["wq"] = jnp.concatenate([uw[:, dk:], q_dec], axis=0).astype(BF16)
            it["ak"] = jnp.concatenate([a, k_dec.T], axis=0).astype(BF16)
            it["decay"] = jnp.exp(it["g_last"])

    early = n_chunks // 2
    prepare([it for i, it in enumerate(items) if i % n_chunks < early])
    prepare([it for i, it in enumerate(items) if i % n_chunks >= early])

    gn = gn_ref[...]
    states = [state_ref[h] for h in range(n_heads)]
    for n in range(n_chunks):
        rows = slice(n * c, (n + 1) * c)
        for h in range(n_heads):
            it = items[h * n_chunks + n]
            cols = slice(h * dk, (h + 1) * dk)
            ws_qs = _dot(it["wq"], states[h].astype(BF16))
            v_new = it["u"] - ws_qs[:c]
            av_kv = _dot(it["ak"], v_new.astype(BF16))
            o = ws_qs[c:] + av_kv[:c]
            states[h] = states[h] * it["decay"] + av_kv[c:]
            ms = jnp.mean(o * o, axis=-1, keepdims=True)
            o = o * lax.rsqrt(ms + EPS) * gn
            o_ref[rows, cols] = (o * _silu(gate_ref[rows, cols])).astype(o_ref.dtype)
    for h in range(n_heads):
        state_ref[h] = states[h]


def _gdn(proj, a_log, dt_bias, gdn_norm, batch, seq):
    n = proj.shape[0]
    width = (proj.shape[1] - LANES) // 4
    n_heads = a_log.shape[0]
    dk = width // n_heads
    tb = GDN_STEP
    nt = seq // tb
    rows = lambda b, t: b * nt + t
    pad = lambda x: jnp.zeros((1, LANES), F32).at[0, :n_heads].set(x)
    col_spec = lambda k: pl.BlockSpec((tb, width), lambda b, t: (rows(b, t), k))
    const = lambda b, t: (0, 0)
    return pl.pallas_call(
        _gdn_kernel,
        grid=(batch, nt),
        in_specs=[col_spec(0), col_spec(1), col_spec(2), col_spec(3),
                  pl.BlockSpec((tb, LANES), lambda b, t: (rows(b, t), 4 * width // LANES)),
                  pl.BlockSpec((1, LANES), const), pl.BlockSpec((1, LANES), const),
                  pl.BlockSpec((1, dk), const)],
        out_specs=pl.BlockSpec((tb, width), lambda b, t: (rows(b, t), 0)),
        out_shape=jax.ShapeDtypeStruct((n, width), BF16),
        scratch_shapes=[pltpu.VMEM((n_heads, dk, dk), F32)],
        compiler_params=_params(2),
        name="gated_delta_rule",
    )(proj, proj, proj, proj, proj, pad(a_log), pad(dt_bias), gdn_norm.reshape(1, dk))


def _sb_kernel(q_ref, k_ref, v_ref, m_ref, o_ref, q2_ref, acc_ref, carry_ref):
    i = pl.program_id(2)
    tq = SB_TILE
    n_tiles = q_ref.shape[0] // tq
    lane = lax.broadcasted_iota(jnp.int32, (tq, LANES), 1)
    suffix = m_ref[...]
    strip = tq // SB_STRIPS

    for t in range(n_tiles):
        q = q_ref[t * tq:(t + 1) * tq, :]
        zero = jnp.zeros_like(q)
        q2_ref[2 * t * tq:(2 * t + 1) * tq, :] = jnp.where(lane < SB_DH, q, zero)
        q2_ref[(2 * t + 1) * tq:(2 * t + 2) * tq, :] = jnp.where(lane >= SB_DH, q, zero)

    def strips_of(t):
        return [slice(2 * t * tq + n * strip, 2 * t * tq + (n + 1) * strip) for n in range(2 * SB_STRIPS)]

    def weigh(pairs):
        work = []
        for t, j, diagonal in pairs:
            start = pl.multiple_of(j * tq, tq)
            k = k_ref[pl.ds(start, tq), :]
            v = v_ref[pl.ds(start, tq), :]
            for n, rows in enumerate(strips_of(t)):
                work.append(dict(n=n, rows=rows, k=k, v=v, diagonal=diagonal))
        for w in work:
            w["y"] = _dot_nt(q2_ref[w["rows"], :], w["k"]) * LOG2_E
        for w in work:
            y = w["y"]
            w["soft"] = jnp.maximum(y, 0.0) + jnp.log2(1.0 + jnp.exp2(-jnp.abs(y)))
            if w["diagonal"]:
                s = lax.broadcasted_iota(jnp.int32, (strip, tq), 1)
                r = lax.broadcasted_iota(jnp.int32, (strip, tq), 0) + (w["n"] % SB_STRIPS) * strip
                w["mask"] = s < r
                w["cost"] = jnp.where(w["mask"], w["soft"], 0.0)
            else:
                w["cost"] = w["soft"]
        for w in work:
            w["later"] = _dot_exact_rhs(w["cost"], suffix)
        for w in work:
            a = jnp.exp2(w["y"] - w["soft"] - w["later"])
            if w["diagonal"]:
                a = jnp.where(w["mask"], a, 0.0)
            w["part"] = _dot(a.astype(BF16), w["v"])
            w["total"] = jnp.sum(w["cost"], axis=1, keepdims=True)
        per_pair = 2 * SB_STRIPS
        return [work[p * per_pair:(p + 1) * per_pair] for p in range(len(pairs))]

    first = i * n_tiles
    pairs = [(t, first + t, True) for t in range(n_tiles)]
    pairs += [(t, jnp.maximum(first + t - 1, 0), False) for t in range(n_tiles)]
    results = weigh(pairs)
    for t in range(n_tiles):
        has_prev = (first + t > 0).astype(F32)
        for d, p in zip(results[t], results[n_tiles + t]):
            acc_ref[d["rows"], :] = d["part"] + p["part"] * (jnp.exp2(-d["total"]) * has_prev)
            carry_ref[d["rows"], :] = d["total"] + p["total"] * has_prev

    @pl.when(jnp.min(carry_ref[...]) < SB_UNDERFLOW_LOG2)
    def _():
        for t in range(n_tiles):
            rows_t = slice(2 * t * tq, 2 * (t + 1) * tq)
            remaining = first + t - 1

            def cond(state):
                step, least = state
                return jnp.logical_and(step < remaining, least < SB_UNDERFLOW_LOG2)

            def body(state, t=t, rows_t=rows_t, remaining=remaining):
                step, _ = state
                (tile,) = weigh([(t, remaining - 1 - step, False)])
                for w in tile:
                    carry = carry_ref[w["rows"], :]
                    acc_ref[w["rows"], :] += w["part"] * jnp.exp2(-carry)
                    carry_ref[w["rows"], :] = carry + w["total"]
                return step + 1, jnp.min(carry_ref[rows_t, :])

            lax.while_loop(cond, body, (jnp.int32(0), jnp.min(carry_ref[rows_t, :])))

    for t in range(n_tiles):
        acc0 = acc_ref[2 * t * tq:(2 * t + 1) * tq, :]
        acc1 = acc_ref[(2 * t + 1) * tq:(2 * t + 2) * tq, :]
        o_ref[t * tq:(t + 1) * tq, :] = jnp.where(lane < SB_DH, acc0, acc1).astype(o_ref.dtype)


def _sb_attention(qkv, suffix, batch, seq):
    n = qkv.shape[0]
    width = qkv.shape[1] // 3
    pairs = width // LANES
    tq = SB_TILE
    rows = SB_TILES_PER_STEP * tq
    nq = seq // rows
    return pl.pallas_call(
        _sb_kernel,
        grid=(batch, pairs, nq),
        in_specs=[pl.BlockSpec((rows, LANES), lambda b, p, i: (b * nq + i, p)),
                  pl.BlockSpec((seq, LANES), lambda b, p, i: (b, pairs + p)),
                  pl.BlockSpec((seq, LANES), lambda b, p, i: (b, 2 * pairs + p)),
                  pl.BlockSpec((tq, tq), lambda b, p, i: (0, 0))],
        out_specs=pl.BlockSpec((rows, LANES), lambda b, p, i: (b * nq + i, p)),
        out_shape=jax.ShapeDtypeStruct((n, width), BF16),
        scratch_shapes=[pltpu.VMEM((2 * rows, LANES), BF16),
                        pltpu.VMEM((2 * rows, LANES), F32),
                        pltpu.VMEM((2 * rows, 1), F32)],
        compiler_params=_params(3),
        name="stick_breaking_attention",
    )(qkv, qkv, qkv, suffix)


def _ret_proj_kernel(x_ref, g_ref, sc_ref, sh_ref, w_ref, cos_ref, sin_ref, qs_ref, ks_ref,
                     q_ref, k_ref, v_ref, gate_ref):
    tm = x_ref.shape[0]
    n_heads, chunk, _ = qs_ref.shape
    w_qk = q_ref.shape[1]
    dk = w_qk // n_heads
    half = dk // 2
    groups = [slice(r0, r0 + chunk) for r0 in range(0, tm, chunk)]
    hbs = [_norm_mod(x_ref[rows, :], g_ref[...], sc_ref[0], sh_ref[0]).astype(BF16) for rows in groups]

    def rot(x, rows, scale):
        cos = cos_ref[rows, :]
        sin = sin_ref[rows, :]
        x1, x2 = x[:, :half], x[:, half:]
        return jnp.concatenate([(x1 * cos - x2 * sin) * scale, (x1 * sin + x2 * cos) * scale], axis=-1)

    for h in range(n_heads):
        cols = slice(h * dk, (h + 1) * dk)
        for rows, hb in zip(groups, hbs):
            q_ref[rows, cols] = rot(_dot(hb, w_ref[:, cols]), rows, qs_ref[h]).astype(BF16)
    for h in range(n_heads):
        cols = slice(h * dk, (h + 1) * dk)
        for rows, hb in zip(groups, hbs):
            pre = _dot(hb, w_ref[:, w_qk + h * dk:w_qk + (h + 1) * dk])
            k_ref[rows, cols] = rot(pre, rows, ks_ref[h]).astype(BF16)
    step = 512
    off = 2 * w_qk
    for ref in (v_ref, gate_ref):
        for o in range(0, ref.shape[1], step):
            for rows, hb in zip(groups, hbs):
                ref[rows, o:o + step] = _dot(hb, w_ref[:, off + o:off + o + step]).astype(ref.dtype)
        off += ref.shape[1]


def _ret_proj(x, gain, sc, sh, w, cos, sin, q_scale, k_scale, n_qk, n_v, seq):
    n, d = x.shape
    tm = ROW_TILE_PROJ
    per_b = seq // tm
    bmap = lambda i: (i // per_b, 0, 0)
    row = lambda i: (i, 0)
    const = lambda i: (0, 0)
    pos = lambda i: (i % per_b, 0)
    return pl.pallas_call(
        _ret_proj_kernel,
        grid=(n // tm,),
        in_specs=[pl.BlockSpec((tm, d), row),
                  pl.BlockSpec((1, d), const),
                  pl.BlockSpec((1, 1, d), bmap),
                  pl.BlockSpec((1, 1, d), bmap),
                  _resident(w.shape, const),
                  pl.BlockSpec((tm, cos.shape[1]), pos),
                  pl.BlockSpec((tm, cos.shape[1]), pos),
                  pl.BlockSpec(q_scale.shape, lambda i: (0, 0, 0)),
                  pl.BlockSpec(k_scale.shape, lambda i: (0, 0, 0))],
        out_specs=[pl.BlockSpec((tm, n_qk), row),
                   pl.BlockSpec((tm, n_qk), row),
                   pl.BlockSpec((tm, n_v), row),
                   pl.BlockSpec((tm, n_v), row)],
        out_shape=[jax.ShapeDtypeStruct((n, n_qk), BF16),
                   jax.ShapeDtypeStruct((n, n_qk), BF16),
                   jax.ShapeDtypeStruct((n, n_v), BF16),
                   jax.ShapeDtypeStruct((n, n_v), F32)],
        compiler_params=_params(1),
        name="retention_in_proj",
    )(x, gain, sc, sh, w, cos, sin, q_scale, k_scale)


def _ret_kernel(q_ref, k_ref, v_ref, gate_ref, decay_ref, o_ref, state_ref):
    t = pl.program_id(1)
    c = RET_CHUNK
    n_heads = state_ref.shape[0]
    dk = q_ref.shape[1] // n_heads
    dv = v_ref.shape[1] // n_heads

    @pl.when(t == 0)
    def _():
        state_ref[...] = jnp.zeros_like(state_ref)

    row = lax.broadcasted_iota(jnp.int32, (c, c), 0)
    col = lax.broadcasted_iota(jnp.int32, (c, c), 1)
    causal = row >= col

    for r0 in range(0, q_ref.shape[0], c):
        rows = slice(r0, r0 + c)
        for h in range(n_heads):
            q = q_ref[rows, h * dk:(h + 1) * dk]
            k = k_ref[rows, h * dk:(h + 1) * dk]
            v = v_ref[rows, h * dv:(h + 1) * dv]
            intra = jnp.where(causal, _dot_nt(q, k), 0.0)
            state = state_ref[h]
            o = _dot(intra.astype(BF16), v) + _dot(q, state.astype(BF16))
            k_t = k.astype(F32).T.astype(BF16)
            state_ref[h] = (state + _dot(k_t, v)) * decay_ref[h]
            ms = jnp.mean(o * o, axis=-1, keepdims=True)
            o = o * lax.rsqrt(ms + EPS)
            gate = gate_ref[rows, h * dv:(h + 1) * dv]
            o_ref[rows, h * dv:(h + 1) * dv] = (o * _silu(gate)).astype(o_ref.dtype)


def _retention(q, k, v, gate, chunk_decay, batch, seq):
    n, w_qk = q.shape
    n_heads = chunk_decay.shape[0]
    w_v = v.shape[1]
    c = RET_CHUNKS_PER_STEP * RET_CHUNK
    nt = seq // c
    rows = lambda b, t: (b * nt + t, 0)
    return pl.pallas_call(
        _ret_kernel,
        grid=(batch, nt),
        in_specs=[pl.BlockSpec((c, w_qk), rows),
                  pl.BlockSpec((c, w_qk), rows),
                  pl.BlockSpec((c, w_v), rows),
                  pl.BlockSpec((c, w_v), rows),
                  pl.BlockSpec(chunk_decay.shape, lambda b, t: (0, 0, 0))],
        out_specs=pl.BlockSpec((c, w_v), rows),
        out_shape=jax.ShapeDtypeStruct((n, w_v), BF16),
        scratch_shapes=[pltpu.VMEM((n_heads, w_qk // n_heads, w_v // n_heads), F32)],
        compiler_params=_params(2),
        name="retention_chunkwise",
    )(q, k, v, gate, chunk_decay)


def _ret_fused_kernel(x_ref, g_ref, sc_ref, sh_ref, w_ref, cos_ref, sin_ref, qs_ref, ks_ref, decay_ref,
                      o_ref, state_ref, *, per_batch):
    i = pl.program_id(0)
    tm = x_ref.shape[0]
    n_heads, chunk, _ = qs_ref.shape
    w_v = o_ref.shape[1]
    dv = w_v // n_heads
    w_qk = (w_ref.shape[1] - 2 * w_v) // 2
    dk = w_qk // n_heads
    half = dk // 2

    @pl.when(i % per_batch == 0)
    def _():
        state_ref[...] = jnp.zeros_like(state_ref)

    groups = [slice(r0, r0 + chunk) for r0 in range(0, tm, chunk)]
    hbs = [_norm_mod(x_ref[rows, :], g_ref[...], sc_ref[0], sh_ref[0]).astype(BF16) for rows in groups]
    row = lax.broadcasted_iota(jnp.int32, (chunk, chunk), 0)
    col = lax.broadcasted_iota(jnp.int32, (chunk, chunk), 1)
    causal = row >= col

    def rot(x, rows, scale):
        cos = cos_ref[rows, :]
        sin = sin_ref[rows, :]
        x1, x2 = x[:, :half], x[:, half:]
        return jnp.concatenate([(x1 * cos - x2 * sin) * scale, (x1 * sin + x2 * cos) * scale], axis=-1)

    for rows, hb in zip(groups, hbs):
        for h in range(n_heads):
            q = rot(_dot(hb, w_ref[:, h * dk:(h + 1) * dk]), rows, qs_ref[h]).astype(BF16)
            k = rot(_dot(hb, w_ref[:, w_qk + h * dk:w_qk + (h + 1) * dk]), rows, ks_ref[h])
            v = _dot(hb, w_ref[:, 2 * w_qk + h * dv:2 * w_qk + (h + 1) * dv]).astype(BF16)
            gate = _dot(hb, w_ref[:, 2 * w_qk + w_v + h * dv:2 * w_qk + w_v + (h + 1) * dv])
            kb = k.astype(BF16)
            intra = jnp.where(causal, _dot_nt(q, kb), 0.0)
            state = state_ref[h]
            o = _dot(intra.astype(BF16), v) + _dot(q, state.astype(BF16))
            k_t = kb.astype(F32).T.astype(BF16)
            state_ref[h] = (state + _dot(k_t, v)) * decay_ref[h]
            ms = jnp.mean(o * o, axis=-1, keepdims=True)
            o = o * lax.rsqrt(ms + EPS)
            o_ref[rows, h * dv:(h + 1) * dv] = (o * _silu(gate)).astype(o_ref.dtype)


def _ret_fused(x, gain, sc, sh, w, cos, sin, q_scale, k_scale, chunk_decay, n_v, seq):
    n, d = x.shape
    tm = ROW_TILE_PROJ
    per_b = seq // tm
    n_heads = q_scale.shape[0]
    bmap = lambda i: (i // per_b, 0, 0)
    row = lambda i: (i, 0)
    const = lambda i: (0, 0)
    const3 = lambda i: (0, 0, 0)
    pos = lambda i: (i % per_b, 0)
    dk = (w.shape[1] - 2 * n_v) // (2 * n_heads)
    return pl.pallas_call(
        functools.partial(_ret_fused_kernel, per_batch=per_b),
        grid=(n // tm,),
        in_specs=[pl.BlockSpec((tm, d), row),
                  pl.BlockSpec((1, d), const),
                  pl.BlockSpec((1, 1, d), bmap),
                  pl.BlockSpec((1, 1, d), bmap),
                  _resident(w.shape, const),
                  pl.BlockSpec((tm, cos.shape[1]), pos),
                  pl.BlockSpec((tm, cos.shape[1]), pos),
                  pl.BlockSpec(q_scale.shape, const3),
                  pl.BlockSpec(k_scale.shape, const3),
                  pl.BlockSpec(chunk_decay.shape, const3)],
        out_specs=pl.BlockSpec((tm, n_v), row),
        out_shape=jax.ShapeDtypeStruct((n, n_v), BF16),
        scratch_shapes=[pltpu.VMEM((n_heads, dk, n_v // n_heads), F32)],
        compiler_params=_params(1),
        name="retention_fused",
    )(x, gain, sc, sh, w, cos, sin, q_scale, k_scale, chunk_decay)


def _post_kernel(*refs, n_mix):
    x_ref = refs[0]
    o_refs = refs[1:1 + n_mix]
    wm_refs = refs[1 + n_mix:1 + 2 * n_mix]
    gtm_ref, g_ref, sc_ref, sh_ref, gtf_ref, win_ref, wout_ref, out_ref, acc_ref = refs[1 + 2 * n_mix:]
    y = _dot(o_refs[0][...], wm_refs[0][...])
    for o_ref, w_ref in zip(o_refs[1:], wm_refs[1:]):
        y = y + _dot(o_ref[...], w_ref[...])
    x1 = x_ref[...] + gtm_ref[0] * y
    hb = _norm_mod(x1, g_ref[...], sc_ref[0], sh_ref[0]).astype(BF16)
    hidden = wout_ref.shape[0]
    for idx, off in enumerate(range(0, hidden, FFN_CHUNK)):
        g = _dot(hb, win_ref[:, off:off + FFN_CHUNK])
        u = _dot(hb, win_ref[:, hidden + off:hidden + off + FFN_CHUNK])
        part = _dot((_silu(g) * u).astype(BF16), wout_ref[off:off + FFN_CHUNK, :])
        if idx == 0:
            acc_ref[...] = part
        else:
            acc_ref[...] += part
    out_ref[...] = x1 + gtf_ref[0] * acc_ref[...]


def _post(x, mix_outs, mix_ws, gt_m, gain, sc, sh, gt_f, w_in, w_out, seq):
    n, d = x.shape
    tm = ROW_TILE_FFN
    per_b = seq // tm
    bmap = lambda i: (i // per_b, 0, 0)
    row = lambda i: (i, 0)
    const = lambda i: (0, 0)
    vec = pl.BlockSpec((1, 1, d), bmap)
    n_mix = len(mix_outs)
    return pl.pallas_call(
        functools.partial(_post_kernel, n_mix=n_mix),
        grid=(n // tm,),
        in_specs=([pl.BlockSpec((tm, d), row)]
                  + [pl.BlockSpec((tm, o.shape[1]), row) for o in mix_outs]
                  + [_resident(w.shape, const) for w in mix_ws]
                  + [vec, pl.BlockSpec((1, d), const), vec, vec, vec,
                     _resident(w_in.shape, const), _resident(w_out.shape, const)]),
        out_specs=pl.BlockSpec((tm, d), row),
        out_shape=jax.ShapeDtypeStruct((n, d), F32),
        scratch_shapes=[pltpu.VMEM((tm, d), F32)],
        compiler_params=_params(1),
        name="out_proj_swiglu",
    )(x, *mix_outs, *mix_ws, gt_m, gain, sc, sh, gt_f, w_in, w_out)


def kernel(x, c, ada_w, ada_b, norm_mix, norm_ffn, hyb_w_in, hyb_conv, gdn_a_log, gdn_dt_bias, gdn_norm, sb_q_norm, sb_k_norm, hyb_w_out, ret_w_in, ret_w_out, ffn_w_in, ffn_w_out):
    batch, seq, d = x.shape
    depth = ada_w.shape[0]
    gdn_heads = gdn_a_log.shape[1]
    gdn_dv = gdn_norm.shape[1]
    gdn_w = gdn_heads * gdn_dv
    sb_dh = sb_q_norm.shape[1]
    assert sb_dh == SB_DH
    sb_w = hyb_w_out.shape[1] - gdn_w
    sb_heads = sb_w // sb_dh
    ret_heads = 4
    ret_qk = ret_w_in.shape[2] // 6
    ret_v = 2 * ret_qk
    ret_dk = ret_qk // ret_heads

    xf = x.reshape(batch * seq, d)
    mod = _modulation(c, ada_w, ada_b)

    head_mean = jnp.kron(jnp.eye(MXU_DIM // sb_dh, dtype=F32), jnp.full((sb_dh, sb_dh), 1.0 / sb_dh, F32)).astype(BF16)
    idx = jnp.arange(SB_TILE)
    suffix = (idx[:, None] > idx[None, :]).astype(BF16)
    pos = jnp.arange(seq, dtype=F32)
    inv = 1.0 / (ROPE_BASE ** (jnp.arange(0, ret_dk, 2, dtype=F32) / ret_dk))
    ang = pos[:, None] * inv[None, :]
    cos, sin = jnp.cos(ang), jnp.sin(ang)
    log_gamma = jnp.log1p(-jnp.exp2(-5.0 - jnp.arange(ret_heads, dtype=F32)))
    steps = jnp.arange(1, RET_CHUNK + 1, dtype=F32)
    ret_q_scale = jnp.broadcast_to(jnp.exp(log_gamma[:, None] * steps[None, :])[:, :, None],
                                   (ret_heads, RET_CHUNK, ret_dk // 2))
    ret_k_scale = jnp.broadcast_to((jnp.exp(-log_gamma[:, None] * steps[None, :]) * ret_dk ** -0.5)[:, :, None],
                                   (ret_heads, RET_CHUNK, ret_dk // 2))
    ret_chunk_decay = jnp.broadcast_to(jnp.exp(log_gamma * RET_CHUNK)[:, None, None], (ret_heads, 1, ret_v // ret_heads))

    for l in range(depth):
        sh_m, sc_m, gt_m, sh_f, sc_f, gt_f = [m.reshape(batch, 1, d) for m in jnp.split(mod[l], 6, axis=-1)]
        i = l // 2
        gain_m = norm_mix[l].reshape(1, d)
        if l % 2 == 0:
            w_in = hyb_w_in[i]
            n_main = 4 * gdn_w
            n_ab = 2 * gdn_heads
            ab_cols = jnp.zeros((d, LANES), F32).at[:, :n_ab].set(w_in[:, n_main:n_main + n_ab])
            w_re = jnp.concatenate([w_in[:, :n_main], ab_cols, w_in[:, n_main + n_ab:]], axis=1).astype(BF16)
            gdn_proj, sb_qkv = _hyb_proj(
                xf, gain_m, sc_m, sh_m, w_re, hyb_conv[i], head_mean,
                jnp.tile(sb_q_norm[i], sb_heads).reshape(1, sb_w),
                jnp.tile(sb_k_norm[i], sb_heads).reshape(1, sb_w), seq, gdn_dv)
            o_a = _gdn(gdn_proj, gdn_a_log[i], gdn_dt_bias[i], gdn_norm[i], batch, seq)
            o_b = _sb_attention(sb_qkv, suffix, batch, seq)
            w_out = hyb_w_out[i].astype(BF16)
            mix_outs = [o_a, o_b]
            mix_ws = [w_out[:gdn_w], w_out[gdn_w:]]
        else:
            o_r = _ret_fused(xf, gain_m, sc_m, sh_m, ret_w_in[i].astype(BF16), cos, sin,
                             ret_q_scale, ret_k_scale, ret_chunk_decay, ret_v, seq)
            mix_outs = [o_r]
            mix_ws = [ret_w_out[i].astype(BF16)]
        xf = _post(xf, mix_outs, mix_ws, gt_m, norm_ffn[l].reshape(1, d), sc_f, sh_f, gt_f,
                   ffn_w_in[l].astype(BF16), ffn_w_out[l].astype(BF16), seq)
    return xf.reshape(batch, seq, d)
```
